```python
import jax
import jax.numpy as jnp
from jax import lax
import numpy as np

D_MODEL = 1024
BATCH = 16
SEQ = 4096
DEPTH = 1

ATTN_WIDTH = D_MODEL // 2
ATTN_HEADS = 8
HEAD_DIM = ATTN_WIDTH // ATTN_HEADS
DILATED_BRANCHES = ((128, 1), (512, 4), (2048, 16))
ATTN_BLOCK = 128
SGU_WIDTH = D_MODEL - ATTN_WIDTH
SGU_GROUPS = 4
SGU_GROUP_DIM = SGU_WIDTH // SGU_GROUPS
CHUNK = 128
MIX_WIDTH = ATTN_WIDTH + SGU_WIDTH
IN_PROJ_WIDTH = 3 * ATTN_WIDTH + 2 * SGU_WIDTH
N_EXPERTS = 256
TOP_K = 8
N_EXPERT_GROUPS = 8
TOPK_GROUPS = 4
EXPERT_DIM = D_MODEL // 4
SHARED_DIM = EXPERT_DIM
ROUTED_SCALE = 2.5
EXPERT_BLOCK = 128
EPS = 1e-6

kernel_name = 'hybrid_dilated_attn_sgu_moe_adaln'


def _rmsnorm(x, g):
    xf = x.astype(jnp.float32)
    y = xf * lax.rsqrt(jnp.mean(xf * xf, axis=-1, keepdims=True) + EPS)
    return (y * g.astype(jnp.float32)).astype(x.dtype)


def _layernorm(x, g, b):
    xf = x.astype(jnp.float32)
    mu = jnp.mean(xf, axis=-1, keepdims=True)
    var = jnp.mean(jnp.square(xf - mu), axis=-1, keepdims=True)
    y = (xf - mu) * lax.rsqrt(var + EPS) * g.astype(jnp.float32) + b.astype(jnp.float32)
    return y.astype(x.dtype)


def _modulate(n, shift, scale):
    return n * (1 + scale[:, None, :]) + shift[:, None, :]


def _alibi_slopes():
    return jnp.exp2(-8.0 * jnp.arange(1, ATTN_HEADS + 1, dtype=jnp.float32) / ATTN_HEADS)


def _strided_window_branch(q, k, v, window, dil, slopes):
    B, S, H, Dh = q.shape
    steps = window // dil
    L = S // dil
    nb = -(-L // ATTN_BLOCK)
    Lp = nb * ATTN_BLOCK

    def to_classes(a):
        a = a.astype(jnp.float32).reshape(B, L, dil, H, Dh).transpose(0, 2, 3, 1, 4)
        a = jnp.pad(a, ((0, 0), (0, 0), (0, 0), (0, Lp - L), (0, 0)))
        return a.reshape(B, dil, H, nb, ATTN_BLOCK, Dh)

    def with_prev(a):
        prev = jnp.pad(a, ((0, 0), (0, 0), (0, 0), (1, 0), (0, 0), (0, 0)))[:, :, :, :-1]
        return jnp.concatenate([prev, a], axis=4)

    qb = to_classes(q)
    kc = with_prev(to_classes(k))
    vc = with_prev(to_classes(v))
    s = jnp.einsum('brhnqd,brhnkd->brhnqk', qb, kc) * (HEAD_DIM ** -0.5)
    back = ATTN_BLOCK + jnp.arange(ATTN_BLOCK)[:, None] - jnp.arange(2 * ATTN_BLOCK)[None, :]
    valid = (back >= 0) & (back <= steps)
    not_before_start = (jnp.arange(nb)[:, None, None] > 0) | (jnp.arange(2 * ATTN_BLOCK)[None, None, :] >= ATTN_BLOCK)
    mask = valid[None] & not_before_start
    bias = -slopes[:, None, None, None] * (back * dil).astype(jnp.float32)
    s = jnp.where(mask, s + bias, -jnp.inf)
    m = jnp.max(s, axis=-1, keepdims=True)
    p = jnp.exp(s - m)
    denom = jnp.sum(p, axis=-1, keepdims=True)
    o = jnp.einsum('brhnqk,brhnkd->brhnqd', p, vc) / denom
    lse = (m + jnp.log(denom))[..., 0]
    o = o.reshape(B, dil, H, Lp, Dh)[:, :, :, :L].transpose(0, 3, 1, 2, 4).reshape(B, S, H, Dh)
    lse = lse.reshape(B, dil, H, Lp)[:, :, :, :L].transpose(0, 3, 1, 2).reshape(B, S, H)
    return o, lse


def _dilated_attention(q, k, v):
    slopes = _alibi_slopes()
    outs, lses = [], []
    for window, dil in DILATED_BRANCHES:
        o, lse = _strided_window_branch(q, k, v, window, dil, slopes)
        outs.append(o)
        lses.append(lse)
    w = jax.nn.softmax(jnp.stack(lses, axis=0), axis=0)
    return jnp.einsum('kbsh,kbshd->bshd', w, jnp.stack(outs, axis=0))


def _spatial_gating(u, z, ln_g, ln_b, w_s, b_s):
    B, S, _ = u.shape
    u = jax.nn.gelu(u)
    z = _layernorm(jax.nn.gelu(z), ln_g, ln_b)
    zc = z.reshape(B, S // CHUNK, CHUNK, SGU_GROUPS, SGU_GROUP_DIM)
    causal = jnp.tril(jnp.ones((CHUNK, CHUNK), dtype=w_s.dtype))
    mixed = jnp.einsum('gts,bcsgd->bctgd', w_s * causal, zc) + b_s.T[None, None, :, :, None]
    return u * mixed.reshape(B, S, SGU_WIDTH)


def _hybrid_mixer(n, w_in, sgu_ln_gain, sgu_ln_bias, w_spatial, b_spatial, g_out_attn, g_out_sgu, w_out):
    B, S, _ = n.shape
    proj = jnp.einsum('bsd,de->bse', n, w_in)
    q, k, v, u, z = jnp.split(
        proj, [ATTN_WIDTH, 2 * ATTN_WIDTH, 3 * ATTN_WIDTH, 3 * ATTN_WIDTH + SGU_WIDTH], axis=-1)
    heads = lambda a: a.reshape(B, S, ATTN_HEADS, HEAD_DIM)
    attn = _dilated_attention(heads(q), heads(k), heads(v)).reshape(B, S, ATTN_WIDTH).astype(n.dtype)
    sgu = _spatial_gating(u, z, sgu_ln_gain, sgu_ln_bias, w_spatial, b_spatial)
    mixed = jnp.concatenate([_rmsnorm(attn, g_out_attn), _rmsnorm(sgu, g_out_sgu)], axis=-1)
    return jnp.einsum('bse,ed->bsd', mixed, w_out)


def _swiglu(x, wg, wu, wd):
    return (jax.nn.silu(x @ wg) * (x @ wu)) @ wd


def _route(xf, w_router, b_router):
    N = xf.shape[0]
    scores = jax.nn.sigmoid((xf @ w_router).astype(jnp.float32))
    choice = scores + b_router.astype(jnp.float32)
    per_group = N_EXPERTS // N_EXPERT_GROUPS
    gscore = jnp.sum(lax.top_k(choice.reshape(N, N_EXPERT_GROUPS, per_group), 2)[0], axis=-1)
    kth = lax.top_k(gscore, TOPK_GROUPS)[0][:, -1:]
    keep = jnp.repeat(gscore >= kth, per_group, axis=-1)
    _, idx = lax.top_k(jnp.where(keep, choice, -jnp.inf), TOP_K)
    sel = jnp.take_along_axis(scores, idx, axis=-1)
    gates = sel / jnp.sum(sel, axis=-1, keepdims=True) * ROUTED_SCALE
    return idx, gates


def _routed_experts(xf, idx, gates, w_gate, w_up, w_down):
    N, D = xf.shape
    A = N * TOP_K
    n_blocks = (A + N_EXPERTS * (EXPERT_BLOCK - 1) + EXPERT_BLOCK - 1) // EXPERT_BLOCK
    P = n_blocks * EXPERT_BLOCK
    flat_e = idx.reshape(A).astype(jnp.int32)
    flat_tok = jnp.arange(A, dtype=jnp.int32) // TOP_K
    flat_g = gates.reshape(A).astype(xf.dtype)
    order = jnp.argsort(flat_e)
    e_sorted = flat_e[order]
    counts = jnp.zeros((N_EXPERTS,), jnp.int32).at[flat_e].add(1)
    starts = jnp.cumsum(counts) - counts
    padded = (counts + EXPERT_BLOCK - 1) // EXPERT_BLOCK * EXPERT_BLOCK
    pends = jnp.cumsum(padded)
    pstarts = pends - padded
    dest = pstarts[e_sorted] + jnp.arange(A, dtype=jnp.int32) - starts[e_sorted]
    row_tok = jnp.zeros((P,), jnp.int32).at[dest].set(flat_tok[order])
    row_gate = jnp.zeros((P,), xf.dtype).at[dest].set(flat_g[order])
    block_start = jnp.arange(n_blocks, dtype=jnp.int32) * EXPERT_BLOCK
    block_e = jnp.minimum(jnp.searchsorted(pends, block_start, side='right'), N_EXPERTS - 1).astype(jnp.int32)

    def body(y, blk):
        tok, g, e = blk
        xt = xf[tok]
        out = _swiglu(xt, w_gate[e], w_up[e], w_down[e]) * g[:, None]
        return y.at[tok].add(out.astype(y.dtype)), None

    y0 = jnp.zeros((N, D), xf.dtype)
    y, _ = lax.scan(body, y0, (row_tok.reshape(n_blocks, EXPERT_BLOCK),
                               row_gate.reshape(n_blocks, EXPERT_BLOCK), block_e))
    return y


def _moe_ffn(n, w_router, b_router, w_exp_gate, w_exp_up, w_exp_down, w_sh_gate, w_sh_up, w_sh_down):
    B, S, D = n.shape
    xf = n.reshape(B * S, D)
    idx, gates = _route(xf, w_router, b_router)
    routed = _routed_experts(xf, idx, gates, w_exp_gate, w_exp_up, w_exp_down)
    shared = _swiglu(xf, w_sh_gate, w_sh_up, w_sh_down)
    return (routed + shared).reshape(B, S, D)


def setup_inputs(seed: int = 0) -> dict:
    key = jax.random.key(seed)
    ks = jax.random.split(key, 23)
    f32 = jnp.float32
    nrm = lambda k, shape, s: jax.random.normal(k, shape, f32) * s
    L = DEPTH
    return {
        'x': nrm(ks[0], (BATCH, SEQ, D_MODEL), 1.0),
        'c': nrm(ks[1], (BATCH, D_MODEL), 1.0),
        'w_ada': nrm(ks[2], (L, D_MODEL, 6 * D_MODEL), 0.5 * D_MODEL ** -0.5),
        'b_ada': nrm(ks[3], (L, 6 * D_MODEL), 0.02),
        'g_norm_mix': 1.0 + nrm(ks[4], (L, D_MODEL), 0.02),
        'w_in': nrm(ks[5], (L, D_MODEL, IN_PROJ_WIDTH), D_MODEL ** -0.5),
        'sgu_ln_gain': 1.0 + nrm(ks[6], (L, SGU_WIDTH), 0.02),
        'sgu_ln_bias': nrm(ks[7], (L, SGU_WIDTH), 0.02),
        'w_spatial': nrm(ks[8], (L, SGU_GROUPS, CHUNK, CHUNK), CHUNK ** -0.5),
        'b_spatial': 1.0 + nrm(ks[9], (L, SGU_GROUPS, CHUNK), 0.02),
        'g_out_attn': 1.0 + nrm(ks[10], (L, ATTN_WIDTH), 0.02),
        'g_out_sgu': 1.0 + nrm(ks[11], (L, SGU_WIDTH), 0.02),
        'w_out': nrm(ks[12], (L, MIX_WIDTH, D_MODEL), MIX_WIDTH ** -0.5),
        'g_norm_ffn': 1.0 + nrm(ks[13], (L, D_MODEL), 0.02),
        'w_router': nrm(ks[14], (L, D_MODEL, N_EXPERTS), D_MODEL ** -0.5),
        'b_router': nrm(ks[15], (L, N_EXPERTS), 0.01),
        'w_exp_gate': nrm(ks[16], (L, N_EXPERTS, D_MODEL, EXPERT_DIM), D_MODEL ** -0.5),
        'w_exp_up': nrm(ks[17], (L, N_EXPERTS, D_MODEL, EXPERT_DIM), D_MODEL ** -0.5),
        'w_exp_down': nrm(ks[18], (L, N_EXPERTS, EXPERT_DIM, D_MODEL), EXPERT_DIM ** -0.5),
        'w_sh_gate': nrm(ks[19], (L, D_MODEL, SHARED_DIM), D_MODEL ** -0.5),
        'w_sh_up': nrm(ks[20], (L, D_MODEL, SHARED_DIM), D_MODEL ** -0.5),
        'w_sh_down': nrm(ks[21], (L, SHARED_DIM, D_MODEL), SHARED_DIM ** -0.5),
        'g_final': 1.0 + nrm(ks[22], (D_MODEL,), 0.02),
    }


def reference(x, c, w_ada, b_ada, g_norm_mix, w_in, sgu_ln_gain, sgu_ln_bias, w_spatial, b_spatial,
              g_out_attn, g_out_sgu, w_out, g_norm_ffn, w_router, b_router, w_exp_gate, w_exp_up,
              w_exp_down, w_sh_gate, w_sh_up, w_sh_down, g_final):
    h = x
    cond = jax.nn.silu(c)
    for l in range(DEPTH):
        mod = cond @ w_ada[l] + b_ada[l]
        sh1, sc1, gt1, sh2, sc2, gt2 = jnp.split(mod, 6, axis=-1)
        n = _modulate(_rmsnorm(h, g_norm_mix[l]), sh1, sc1)
        mix = _hybrid_mixer(n, w_in[l], sgu_ln_gain[l], sgu_ln_bias[l], w_spatial[l], b_spatial[l],
                            g_out_attn[l], g_out_sgu[l], w_out[l])
        h = h + gt1[:, None, :] * mix
        n = _modulate(_rmsnorm(h, g_norm_ffn[l]), sh2, sc2)
        ffn = _moe_ffn(n, w_router[l], b_router[l], w_exp_gate[l], w_exp_up[l], w_exp_down[l],
                       w_sh_gate[l], w_sh_up[l], w_sh_down[l])
        h = h + gt2[:, None, :] * ffn
    return _rmsnorm(h, g_final)
```

```python
import functools

import jax
import jax.numpy as jnp
from jax import lax
from jax.experimental import pallas as pl
from jax.experimental.pallas import tpu as pltpu

F32 = jnp.float32
BF16 = jnp.bfloat16

D_MODEL = 1024
ATTN_WIDTH = 512
ATTN_HEADS = 8
HEAD_DIM = 64
SGU_WIDTH = 512
SGU_GROUPS = 4
SGU_GROUP_DIM = 128
CHUNK = 128
DILATED_BRANCHES = ((128, 1), (512, 4), (2048, 16))
ATTN_BLOCK = 128
N_EXPERTS = 256
TOP_K = 8
N_EXPERT_GROUPS = 8
GROUP_SIZE = N_EXPERTS // N_EXPERT_GROUPS
TOPK_GROUPS = 4
EXPERT_DIM = 256
ROUTED_SCALE = 2.5
EPS = 1e-6

LANES = 128
HALF = D_MODEL // 2
ROW_BLOCK = 256
NEG_BIG = -1e30
VMEM_LIMIT = 56 * 1024 * 1024

TM_PROJ = 512
TM_ROUTE = 512
TM_MOVE = 256


def _dot(a, b):
    return jnp.dot(a, b, preferred_element_type=F32)


def _dot_nt(a, b):
    return lax.dot_general(a, b, (((1,), (1,)), ((), ())), preferred_element_type=F32)


def _rms(x, g):
    return x * lax.rsqrt(jnp.mean(x * x, axis=-1, keepdims=True) + EPS) * g


def _pack_halves(x):
    return pltpu.pack_elementwise([x[:, :HALF], x[:, HALF:]], packed_dtype=BF16)


def _unpack_halves(w):
    lo = pltpu.unpack_elementwise(w, index=0, packed_dtype=BF16, unpacked_dtype=F32)
    hi = pltpu.unpack_elementwise(w, index=1, packed_dtype=BF16, unpacked_dtype=F32)
    return lo, hi


def _params(sem=None):
    return pltpu.CompilerParams(dimension_semantics=sem, vmem_limit_bytes=VMEM_LIMIT)


def _ada_kernel(c_ref, w_ref, b_ref, o_ref):
    c = c_ref[...]
    cond = c * jax.nn.sigmoid(c)
    ch = cond.astype(BF16)
    cl = (cond - ch.astype(F32)).astype(BF16)
    w = w_ref[...]
    wh = w.astype(BF16)
    wl = (w - wh.astype(F32)).astype(BF16)
    o_ref[...] = _dot(ch, wh) + _dot(cl, wh) + _dot(ch, wl) + b_ref[...]


def _ada(c, w_ada, b_ada):
    B = c.shape[0]
    n_out = w_ada.shape[1]
    tn = D_MODEL
    return pl.pallas_call(
        _ada_kernel,
        out_shape=jax.ShapeDtypeStruct((B, n_out), F32),
        grid=(n_out // tn,),
        in_specs=[
            pl.BlockSpec((B, D_MODEL), lambda j: (0, 0)),
            pl.BlockSpec((D_MODEL, tn), lambda j: (0, j)),
            pl.BlockSpec((1, tn), lambda j: (0, j)),
        ],
        out_specs=pl.BlockSpec((B, tn), lambda j: (0, j)),
        compiler_params=_params(("arbitrary",)),
        name="ada",
    )(c, w_ada, b_ada.reshape(1, n_out))


def _inproj_kernel(x_ref, mod_ref, g_ref, w_ref, q_ref, k_ref, v_ref, u_ref, z_ref):
    x = x_ref[0]
    shift = mod_ref[0, 0:1, :]
    scale = mod_ref[0, 1:2, :]
    n = _rms(x, g_ref[...]) * (1.0 + scale) + shift
    p = _dot(n.astype(BF16), w_ref[...])
    q_ref[0] = (p[:, 0:ATTN_WIDTH] * (HEAD_DIM ** -0.5)).astype(BF16)
    k_ref[0] = p[:, ATTN_WIDTH:2 * ATTN_WIDTH].astype(BF16)
    v_ref[0] = p[:, 2 * ATTN_WIDTH:3 * ATTN_WIDTH].astype(BF16)
    u_ref[0] = p[:, 3 * ATTN_WIDTH:3 * ATTN_WIDTH + SGU_WIDTH].astype(BF16)
    z_ref[0] = p[:, 3 * ATTN_WIDTH + SGU_WIDTH:].astype(BF16)


def _inproj(x, mod3, g_norm, w_in_bf):
    B, S, _ = x.shape
    tm = TM_PROJ
    n_in = w_in_bf.shape[1]
    out = jax.ShapeDtypeStruct((B, S, ATTN_WIDTH), BF16)
    tile = pl.BlockSpec((1, tm, ATTN_WIDTH), lambda b, i: (b, i, 0))
    return pl.pallas_call(
        _inproj_kernel,
        out_shape=(out,) * 5,
        grid=(B, S // tm),
        in_specs=[
            pl.BlockSpec((1, tm, D_MODEL), lambda b, i: (b, i, 0)),
            pl.BlockSpec((1, 6, D_MODEL), lambda b, i: (b, 0, 0)),
            pl.BlockSpec((1, D_MODEL), lambda b, i: (0, 0)),
            pl.BlockSpec((D_MODEL, n_in), lambda b, i: (0, 0)),
        ],
        out_specs=(tile,) * 5,
        compiler_params=_params(("arbitrary", "arbitrary")),
        name="inproj",
    )(x, mod3, g_norm.reshape(1, D_MODEL), w_in_bf)


def _attn_kernel(slope_ref, q_ref, k_ref, v_ref, o_ref, qf, kf, vf, oacc, lacc, tbl, *, seq):
    qf[...] = q_ref[0].astype(F32)
    kf[...] = k_ref[0].astype(F32)
    vf[...] = v_ref[0].astype(F32)

    lane = lax.broadcasted_iota(jnp.int32, (1, LANES), 1)
    head0 = lane < HEAD_DIM
    slopes = slope_ref[0]
    slope_h = (slopes[:, 0:1], slopes[:, HEAD_DIM:HEAD_DIM + 1])

    qr = lax.broadcasted_iota(jnp.int32, (ATTN_BLOCK, 2 * ATTN_BLOCK), 0)
    kc = lax.broadcasted_iota(jnp.int32, (ATTN_BLOCK, 2 * ATTN_BLOCK), 1)
    for bi, (window, dil) in enumerate(DILATED_BRANCHES):
        steps = window // dil
        for var in range(2):
            back = qr - kc + var * ATTN_BLOCK
            valid = (back >= 0) & (back <= steps)
            dist = (back * dil).astype(F32)
            for hh in range(2):
                tbl[bi, var, hh] = jnp.where(valid, -slope_h[hh] * dist, NEG_BIG)

    for bi, (window, dil) in enumerate(DILATED_BRANCHES):
        cls_len = seq // dil
        nb = cls_len // ATTN_BLOCK
        nb_shift = nb.bit_length() - 1

        def rows(start, size, dil=dil):
            if dil == 1:
                return pl.ds(pl.multiple_of(start, ATTN_BLOCK), size)
            return pl.ds(start, size, stride=dil)

        def body(it, carry, bi=bi, dil=dil, nb=nb, nb_shift=nb_shift, rows=rows):
            r = lax.shift_right_logical(it, nb_shift)
            i = it & (nb - 1)
            var = jnp.minimum(i, 1)
            q_rows = rows(i * ATTN_BLOCK * dil + r, ATTN_BLOCK)
            k_rows = rows((i - var) * ATTN_BLOCK * dil + r, 2 * ATTN_BLOCK)
            q2 = qf[q_rows, :]
            kb = kf[k_rows, :].astype(BF16)
            v2 = vf[k_rows, :]
            o = None
            stats = []
            for hh in range(2):
                mine = head0 if hh == 0 else jnp.logical_not(head0)
                s = _dot_nt(jnp.where(mine, q2, 0.0).astype(BF16), kb) + tbl[bi, var, hh]
                m = jnp.max(s, axis=-1, keepdims=True)
                p = jnp.exp(s - m)
                den = jnp.sum(p, axis=-1, keepdims=True)
                pv = _dot(p.astype(BF16), jnp.where(mine, v2, 0.0).astype(BF16))
                o = pv if o is None else o + pv
                stats.append((m, den))
            den2 = jnp.where(head0, stats[0][1], stats[1][1])
            o = o / den2
            lse = jnp.where(head0, stats[0][0], stats[1][0]) + jnp.log(den2)
            if bi == 0:
                oacc[q_rows, :] = o
                lacc[q_rows, :] = lse
            else:
                l_old = lacc[q_rows, :]
                m2 = jnp.maximum(l_old, lse)
                a = jnp.exp(l_old - m2)
                b = jnp.exp(lse - m2)
                tot = a + b
                oacc[q_rows, :] = (oacc[q_rows, :] * a + o * b) / tot
                lacc[q_rows, :] = m2 + jnp.log(tot)
            return carry

        lax.fori_loop(0, dil * nb, body, 0)

    o_ref[0] = oacc[...].astype(BF16)


def _attention(q, k, v, slopes_lane):
    B, S, _ = q.shape
    for window, dil in DILATED_BRANCHES:
        assert window // dil <= ATTN_BLOCK
        cls_len = S // dil
        assert S % dil == 0 and cls_len % ATTN_BLOCK == 0 and cls_len >= 2 * ATTN_BLOCK
        assert (cls_len // ATTN_BLOCK) & (cls_len // ATTN_BLOCK - 1) == 0
    n_pairs = ATTN_WIDTH // LANES
    tile = pl.BlockSpec((1, S, LANES), lambda b, p: (b, 0, p))
    return pl.pallas_call(
        functools.partial(_attn_kernel, seq=S),
        out_shape=jax.ShapeDtypeStruct((B, S, ATTN_WIDTH), BF16),
        grid=(B, n_pairs),
        in_specs=[pl.BlockSpec((1, 1, LANES), lambda b, p: (p, 0, 0)), tile, tile, tile],
        out_specs=tile,
        scratch_shapes=[pltpu.VMEM((S, LANES), F32)] * 5
        + [pltpu.VMEM((len(DILATED_BRANCHES), 2, 2, ATTN_BLOCK, 2 * ATTN_BLOCK), F32)],
        compiler_params=_params(("arbitrary", "arbitrary")),
        name="attn",
    )(slopes_lane, q, k, v)


def _mix_kernel(attn_ref, u_ref, z_ref, x_ref, mod_ref, ga_ref, gs_ref, lng_ref, lnb_ref, wsp_ref,
                bsp_ref, wout_ref, gffn_ref, wrt_ref, wsgu_ref, wsd_ref,
                hs_ref, n2p_ref, lg_ref):
    tm = x_ref.shape[1]
    nc = tm // CHUNK
    a_n = _rms(attn_ref[0].astype(F32), ga_ref[...])

    ug = jax.nn.gelu(u_ref[0].astype(F32))
    zg = jax.nn.gelu(z_ref[0].astype(F32))
    mu = jnp.mean(zg, axis=-1, keepdims=True)
    zc = zg - mu
    var = jnp.mean(zc * zc, axis=-1, keepdims=True)
    zb = (zc * lax.rsqrt(var + EPS) * lng_ref[...] + lnb_ref[...]).astype(BF16)

    row = lax.broadcasted_iota(jnp.int32, (CHUNK, CHUNK), 0)
    col = lax.broadcasted_iota(jnp.int32, (CHUNK, CHUNK), 1)
    per_group = []
    for g in range(SGU_GROUPS):
        wc = jnp.where(row >= col, wsp_ref[g], 0.0).astype(BF16)
        lanes = slice(g * SGU_GROUP_DIM, (g + 1) * SGU_GROUP_DIM)
        zcat = jnp.concatenate([zb[c * CHUNK:(c + 1) * CHUNK, lanes] for c in range(nc)], axis=1)
        per_group.append(_dot(wc, zcat) + bsp_ref[:, g:g + 1])
    mixed = jnp.concatenate(
        [jnp.concatenate([per_group[g][:, c * CHUNK:(c + 1) * CHUNK] for g in range(SGU_GROUPS)], axis=1)
         for c in range(nc)], axis=0)
    s_n = _rms(ug * mixed, gs_ref[...])

    mix = (_dot(a_n.astype(BF16), wout_ref[0:ATTN_WIDTH, :])
           + _dot(s_n.astype(BF16), wout_ref[ATTN_WIDTH:, :]))
    gate1 = mod_ref[0, 2:3, :]
    shift2 = mod_ref[0, 3:4, :]
    scale2 = mod_ref[0, 4:5, :]
    gate2 = mod_ref[0, 5:6, :]
    h1 = x_ref[0] + gate1 * mix
    n2 = _rms(h1, gffn_ref[...]) * (1.0 + scale2) + shift2
    n2b = n2.astype(BF16)

    lg_ref[...] = _dot_nt(wrt_ref[...], n2b)
    gu = _dot(n2b, wsgu_ref[...])
    gsh = gu[:, :EXPERT_DIM]
    act = (gsh * jax.nn.sigmoid(gsh)) * gu[:, EXPERT_DIM:]
    shared = _dot(act.astype(BF16), wsd_ref[...])
    hs_ref[0] = h1 + gate2 * shared
    n2p_ref[...] = _pack_halves(n2)


def _mix(attn, u, z, x, mod3, g_out_attn, g_out_sgu, ln_g, ln_b, w_spatial, b_spatial_t, w_out_bf,
         g_norm_ffn, w_router_t_bf, w_sh_gu_bf, w_sh_d_bf):
    B, S, _ = x.shape
    tm = TM_PROJ
    nt = S // tm
    N = B * S
    half_tile = pl.BlockSpec((1, tm, ATTN_WIDTH), lambda b, i: (b, i, 0))
    full_tile = pl.BlockSpec((1, tm, D_MODEL), lambda b, i: (b, i, 0))

    def const(shape):
        return pl.BlockSpec(shape, lambda b, i: (0,) * len(shape))

    return pl.pallas_call(
        _mix_kernel,
        out_shape=(
            jax.ShapeDtypeStruct((B, S, D_MODEL), F32),
            jax.ShapeDtypeStruct((N, HALF), jnp.uint32),
            jax.ShapeDtypeStruct((N_EXPERTS, N), F32),
        ),
        grid=(B, nt),
        in_specs=[
            half_tile, half_tile, half_tile, full_tile,
            pl.BlockSpec((1, 6, D_MODEL), lambda b, i: (b, 0, 0)),
            const((1, ATTN_WIDTH)), const((1, SGU_WIDTH)), const((1, SGU_WIDTH)), const((1, SGU_WIDTH)),
            const((SGU_GROUPS, CHUNK, CHUNK)), const((CHUNK, SGU_GROUPS)),
            const((ATTN_WIDTH + SGU_WIDTH, D_MODEL)), const((1, D_MODEL)),
            const((N_EXPERTS, D_MODEL)), const((D_MODEL, 2 * EXPERT_DIM)), const((EXPERT_DIM, D_MODEL)),
        ],
        out_specs=(
            full_tile,
            pl.BlockSpec((tm, HALF), lambda b, i: (b * nt + i, 0)),
            pl.BlockSpec((N_EXPERTS, tm), lambda b, i: (0, b * nt + i)),
        ),
        compiler_params=_params(("arbitrary", "arbitrary")),
        name="mix",
    )(attn, u, z, x, mod3, g_out_attn.reshape(1, -1), g_out_sgu.reshape(1, -1), ln_g.reshape(1, -1),
      ln_b.reshape(1, -1), w_spatial, b_spatial_t, w_out_bf, g_norm_ffn.reshape(1, -1), w_router_t_bf,
      w_sh_gu_bf, w_sh_d_bf)


def _first_max(v, iota, size):
    mx = jnp.max(v, axis=0, keepdims=True)
    am = jnp.min(jnp.where(v == mx, iota, size), axis=0, keepdims=True)
    return mx, am


def _route_kernel(lg_ref, br_ref, idx_ref, gate_ref, rank_ref, cnt_ref, carry, tri):
    step = pl.program_id(0)
    tr = lg_ref.shape[1]

    @pl.when(step == 0)
    def _():
        carry[...] = jnp.zeros_like(carry)
        before = (lax.broadcasted_iota(jnp.int32, (tr, tr), 0)
                  < lax.broadcasted_iota(jnp.int32, (tr, tr), 1))
        tri[...] = jnp.where(before, 1.0, 0.0).astype(BF16)

    scores = jax.nn.sigmoid(lg_ref[...])
    choice = scores + br_ref[...]

    iota_g = lax.broadcasted_iota(jnp.int32, (GROUP_SIZE, tr), 0)
    gs = []
    for g in range(N_EXPERT_GROUPS):
        cg = choice[g * GROUP_SIZE:(g + 1) * GROUP_SIZE, :]
        m1, am = _first_max(cg, iota_g, GROUP_SIZE)
        m2 = jnp.max(jnp.where(iota_g == am, -jnp.inf, cg), axis=0, keepdims=True)
        gs.append(m1 + m2)
    gscore = jnp.concatenate(gs, axis=0)

    iota_n = lax.broadcasted_iota(jnp.int32, (N_EXPERT_GROUPS, tr), 0)
    t = gscore
    for _ in range(TOPK_GROUPS - 1):
        _, am = _first_max(t, iota_n, N_EXPERT_GROUPS)
        t = jnp.where(iota_n == am, -jnp.inf, t)
    kth = jnp.max(t, axis=0, keepdims=True)
    keep = gscore >= kth

    v = jnp.concatenate(
        [jnp.where(keep[g:g + 1, :], choice[g * GROUP_SIZE:(g + 1) * GROUP_SIZE, :], -jnp.inf)
         for g in range(N_EXPERT_GROUPS)], axis=0)
    iota_e = lax.broadcasted_iota(jnp.int32, (N_EXPERTS, tr), 0)
    idxs, sels = [], []
    chosen = jnp.zeros((N_EXPERTS, tr), F32)
    for _ in range(TOP_K):
        _, am = _first_max(v, iota_e, N_EXPERTS)
        hit = iota_e == am
        idxs.append(am)
        sels.append(jnp.sum(jnp.where(hit, scores, 0.0), axis=0, keepdims=True))
        chosen = jnp.where(hit, 1.0, chosen)
        v = jnp.where(hit, -jnp.inf, v)
    sel = jnp.concatenate(sels, axis=0)
    idx_ref[...] = jnp.concatenate(idxs, axis=0)
    gate_ref[...] = sel / jnp.sum(sel, axis=0, keepdims=True) * ROUTED_SCALE

    earlier = _dot(chosen.astype(BF16), tri[...]) + carry[...]
    rank_ref[...] = jnp.concatenate(
        [jnp.sum(jnp.where(iota_e == am, earlier, 0.0), axis=0, keepdims=True) for am in idxs],
        axis=0).astype(jnp.int32)
    carry[...] = carry[...] + jnp.sum(chosen, axis=1, keepdims=True)
    cnt_ref[...] = jnp.broadcast_to(carry[...], cnt_ref.shape)


def _route(logits_t, b_router):
    N = logits_t.shape[1]
    tr = TM_ROUTE
    kt = pl.BlockSpec((TOP_K, tr), lambda i: (0, i))
    return pl.pallas_call(
        _route_kernel,
        out_shape=(
            jax.ShapeDtypeStruct((TOP_K, N), jnp.int32),
            jax.ShapeDtypeStruct((TOP_K, N), F32),
            jax.ShapeDtypeStruct((TOP_K, N), jnp.int32),
            jax.ShapeDtypeStruct((N_EXPERTS, LANES), F32),
        ),
        grid=(N // tr,),
        in_specs=[
            pl.BlockSpec((N_EXPERTS, tr), lambda i: (0, i)),
            pl.BlockSpec((N_EXPERTS, 1), lambda i: (0, 0)),
        ],
        out_specs=(kt, kt, kt, pl.BlockSpec((N_EXPERTS, LANES), lambda i: (0, 0))),
        scratch_shapes=[pltpu.VMEM((N_EXPERTS, 1), F32), pltpu.VMEM((tr, tr), BF16)],
        compiler_params=_params(("arbitrary",)),
        name="route",
    )(logits_t, b_router.reshape(N_EXPERTS, 1))


def _dest_kernel(idx_ref, rank_ref, ps_ref, dest_ref):
    tr = idx_ref.shape[1]
    iota_e = lax.broadcasted_iota(jnp.int32, (N_EXPERTS, tr), 0)
    idx = idx_ref[...]
    start = jnp.concatenate(
        [jnp.sum(jnp.where(iota_e == idx[k:k + 1, :], ps_ref[...], 0.0), axis=0, keepdims=True)
         for k in range(TOP_K)], axis=0)
    dest_ref[...] = start.astype(jnp.int32) + rank_ref[...]


def _dest(idx_t, rank_t, pstart):
    N = idx_t.shape[1]
    tr = TM_ROUTE
    kt = pl.BlockSpec((TOP_K, tr), lambda i: (0, i))
    return pl.pallas_call(
        _dest_kernel,
        out_shape=jax.ShapeDtypeStruct((TOP_K, N), jnp.int32),
        grid=(N // tr,),
        in_specs=[kt, kt, pl.BlockSpec((N_EXPERTS, 1), lambda i: (0, 0))],
        out_specs=kt,
        compiler_params=_params(("arbitrary",)),
        name="dest",
    )(idx_t, rank_t, pstart.astype(F32).reshape(N_EXPERTS, 1))


def _row_copy(src, dst, sem):
    return pltpu.make_async_copy(src, dst, sem)


def _disp_kernel(zb_ref, dest_ref, x_ref, xs_ref, zbuf, sem):
    tm = x_ref.shape[0]

    @pl.when(pl.program_id(0) == 0)
    def _():
        zbuf[...] = jnp.zeros_like(zbuf)

        def zero(i, started):
            blk = zb_ref[i]

            @pl.when(blk >= 0)
            def _():
                start = pl.multiple_of(blk * ROW_BLOCK, ROW_BLOCK)
                _row_copy(zbuf, xs_ref.at[pl.ds(start, ROW_BLOCK)], sem).start()

            return started + jnp.where(blk >= 0, 1, 0)

        def done(i, carry):
            _row_copy(zbuf, xs_ref.at[pl.ds(0, ROW_BLOCK)], sem).wait()
            return carry

        started = lax.fori_loop(0, zb_ref.shape[0], zero, 0)
        lax.fori_loop(0, started, done, 0)

    def issue(t, carry):
        for k in range(TOP_K):
            _row_copy(x_ref.at[pl.ds(t, 1)], xs_ref.at[pl.ds(dest_ref[k, t], 1)], sem).start()
        return carry

    lax.fori_loop(0, tm, issue, 0)
    for k in range(TOP_K):
        _row_copy(x_ref, xs_ref.at[pl.ds(0, tm)], sem).wait()


def _dispatch(zero_blocks, dest_t, n2p, n_rows):
    N = n2p.shape[0]
    tm = TM_MOVE
    grid_spec = pltpu.PrefetchScalarGridSpec(
        num_scalar_prefetch=1,
        grid=(N // tm,),
        in_specs=[
            pl.BlockSpec((TOP_K, tm), lambda i, zb: (0, i), memory_space=pltpu.SMEM),
            pl.BlockSpec((tm, HALF), lambda i, zb: (i, 0)),
        ],
        out_specs=pl.BlockSpec(memory_space=pl.ANY),
        scratch_shapes=[pltpu.VMEM((ROW_BLOCK, HALF), jnp.uint32), pltpu.SemaphoreType.DMA],
    )
    return pl.pallas_call(
        _disp_kernel,
        out_shape=jax.ShapeDtypeStruct((n_rows, HALF), jnp.uint32),
        grid_spec=grid_spec,
        compiler_params=_params(("arbitrary",)),
        name="disp",
    )(zero_blocks, dest_t, n2p)


def _experts_kernel(be_ref, bv_ref, nu_ref, xs_ref, wg_ref, wu_ref, wd_ref, ys_ref, wgb, wub, wdb):
    j = pl.program_id(0)

    @pl.when(j < nu_ref[0])
    def _():
        e = be_ref[j]
        prev = be_ref[jnp.maximum(j - 1, 0)]

        @pl.when(jnp.logical_or(j == 0, e != prev))
        def _():
            wgb[...] = wg_ref[...].astype(BF16)
            wub[...] = wu_ref[...].astype(BF16)
            wdb[...] = wd_ref[...].astype(BF16)

        lo, hi = _unpack_halves(xs_ref[...])
        live = lax.broadcasted_iota(jnp.int32, lo.shape, 0) < bv_ref[j]
        lo = jnp.where(live, lo, 0.0).astype(BF16)
        hi = jnp.where(live, hi, 0.0).astype(BF16)
        g = _dot(lo, wgb[0:HALF, :]) + _dot(hi, wgb[HALF:, :])
        u = _dot(lo, wub[0:HALF, :]) + _dot(hi, wub[HALF:, :])
        act = (g * jax.nn.sigmoid(g)) * u
        ys_ref[...] = _pack_halves(_dot(act.astype(BF16), wdb[...]))

    @pl.when(j >= nu_ref[0])
    def _():
        ys_ref[...] = jnp.zeros_like(ys_ref)


def _experts(block_e, block_valid, n_used, xs, w_gate, w_up, w_down):
    n_blocks = xs.shape[0] // ROW_BLOCK

    def row_map(j, be, bv, nu):
        return (jnp.minimum(j, nu[0] - 1), 0)

    def out_map(j, be, bv, nu):
        return (j, 0)

    def w_map(j, be, bv, nu):
        return (be[jnp.minimum(j, nu[0] - 1)], 0, 0)

    grid_spec = pltpu.PrefetchScalarGridSpec(
        num_scalar_prefetch=3,
        grid=(n_blocks,),
        in_specs=[
            pl.BlockSpec((ROW_BLOCK, HALF), row_map),
            pl.BlockSpec((None, D_MODEL, EXPERT_DIM), w_map),
            pl.BlockSpec((None, D_MODEL, EXPERT_DIM), w_map),
            pl.BlockSpec((None, EXPERT_DIM, D_MODEL), w_map),
        ],
        out_specs=pl.BlockSpec((ROW_BLOCK, HALF), out_map),
        scratch_shapes=[
            pltpu.VMEM((D_MODEL, EXPERT_DIM), BF16),
            pltpu.VMEM((D_MODEL, EXPERT_DIM), BF16),
            pltpu.VMEM((EXPERT_DIM, D_MODEL), BF16),
        ],
    )
    return pl.pallas_call(
        _experts_kernel,
        out_shape=jax.ShapeDtypeStruct(xs.shape, jnp.uint32),
        grid_spec=grid_spec,
        compiler_params=_params(("arbitrary",)),
        name="experts",
    )(block_e, block_valid, n_used, xs, w_gate, w_up, w_down)


def _comb_kernel(dest_ref, ys_ref, hs_ref, gate_ref, mod_ref, gf_ref, o_ref, buf, sem):
    tm = hs_ref.shape[0]

    def issue(t, carry):
        for k in range(TOP_K):
            _row_copy(ys_ref.at[pl.ds(dest_ref[k, t], 1)], buf.at[k, pl.ds(t, 1)], sem).start()
        return carry

    lax.fori_loop(0, tm, issue, 0)
    for k in range(TOP_K):
        _row_copy(ys_ref.at[pl.ds(0, tm)], buf.at[k], sem).wait()

    gates = gate_ref[...]
    lo = jnp.zeros((tm, HALF), F32)
    hi = jnp.zeros((tm, HALF), F32)
    for k in range(TOP_K):
        lo_k, hi_k = _unpack_halves(buf[k])
        gk = gates[:, k:k + 1]
        lo = lo + gk * lo_k
        hi = hi + gk * hi_k
    routed = jnp.concatenate([lo, hi], axis=1)
    h2 = hs_ref[...] + mod_ref[0, 5:6, :] * routed
    o_ref[...] = _rms(h2, gf_ref[...])


def _combine(dest_t, ys, hs2, gates_nk, mod3, g_final, seq):
    N = hs2.shape[0]
    tm = TM_MOVE
    per_seq = seq // tm
    return pl.pallas_call(
        _comb_kernel,
        out_shape=jax.ShapeDtypeStruct((N, D_MODEL), F32),
        grid=(N // tm,),
        in_specs=[
            pl.BlockSpec((TOP_K, tm), lambda i: (0, i), memory_space=pltpu.SMEM),
            pl.BlockSpec(memory_space=pl.ANY),
            pl.BlockSpec((tm, D_MODEL), lambda i: (i, 0)),
            pl.BlockSpec((tm, TOP_K), lambda i: (i, 0)),
            pl.BlockSpec((1, 6, D_MODEL), lambda i: (i // per_seq, 0, 0)),
            pl.BlockSpec((1, D_MODEL), lambda i: (0, 0)),
        ],
        out_specs=pl.BlockSpec((tm, D_MODEL), lambda i: (i, 0)),
        scratch_shapes=[pltpu.VMEM((TOP_K, tm, HALF), jnp.uint32), pltpu.SemaphoreType.DMA],
        compiler_params=_params(("arbitrary",)),
        name="comb",
    )(dest_t, ys, hs2, gates_nk, mod3, g_final.reshape(1, D_MODEL))


def _layer(x, mod3, g_norm_mix, w_in, sgu_ln_gain, sgu_ln_bias, w_spatial, b_spatial, g_out_attn,
           g_out_sgu, w_out, g_norm_ffn, w_router, b_router, w_exp_gate, w_exp_up, w_exp_down,
           w_sh_gate, w_sh_up, w_sh_down, g_final):
    B, S, _ = x.shape
    N = B * S

    q, k, v, u, z = _inproj(x, mod3, g_norm_mix, w_in.astype(BF16))
    head_of_lane = jnp.arange(ATTN_WIDTH) // HEAD_DIM
    slopes = jnp.exp2(-8.0 * jnp.arange(1, ATTN_HEADS + 1, dtype=F32) / ATTN_HEADS)
    slopes_lane = slopes[head_of_lane].reshape(ATTN_WIDTH // LANES, 1, LANES)
    attn = _attention(q, k, v, slopes_lane)

    hs, n2p, logits_t = _mix(
        attn, u, z, x, mod3, g_out_attn, g_out_sgu, sgu_ln_gain, sgu_ln_bias, w_spatial, b_spatial.T,
        w_out.astype(BF16), g_norm_ffn, w_router.T.astype(BF16),
        jnp.concatenate([w_sh_gate, w_sh_up], axis=1).astype(BF16), w_sh_down.astype(BF16))

    idx_t, gates_t, rank_t, counts = _route(logits_t, b_router)

    n_blocks = (N * TOP_K + N_EXPERTS * (ROW_BLOCK - 1)) // ROW_BLOCK
    cnt = counts[:, 0].astype(jnp.int32)
    padded = (cnt + ROW_BLOCK - 1) // ROW_BLOCK * ROW_BLOCK
    pends = jnp.cumsum(padded)
    pstart = pends - padded
    block_row = jnp.arange(n_blocks, dtype=jnp.int32) * ROW_BLOCK
    block_e = jnp.minimum(jnp.searchsorted(pends, block_row, side="right"), N_EXPERTS - 1).astype(jnp.int32)
    block_valid = jnp.clip(cnt[block_e] - (block_row - pstart[block_e]), 0, ROW_BLOCK).astype(jnp.int32)
    n_used = (pends[-1:] // ROW_BLOCK).astype(jnp.int32)

    last_block = jnp.where(cnt > 0, pends // ROW_BLOCK - 1, -1)
    tail_block = n_used[0] + jnp.arange(n_blocks - N * TOP_K // ROW_BLOCK)
    tail_block = jnp.where(tail_block < n_blocks, tail_block, -1)
    zero_blocks = jnp.concatenate([last_block, tail_block]).astype(jnp.int32)

    dest_t = _dest(idx_t, rank_t, pstart)
    xs = _dispatch(zero_blocks, dest_t, n2p, n_blocks * ROW_BLOCK)
    ys = _experts(block_e, block_valid, n_used, xs, w_exp_gate, w_exp_up, w_exp_down)
    out = _combine(dest_t, ys, hs.reshape(N, D_MODEL), gates_t.T, mod3, g_final, S)
    return out.reshape(B, S, D_MODEL)


def kernel(x, c, w_ada, b_ada, g_norm_mix, w_in, sgu_ln_gain, sgu_ln_bias, w_spatial, b_spatial, g_out_attn, g_out_sgu, w_out, g_norm_ffn, w_router, b_router, w_exp_gate, w_exp_up, w_exp_down, w_sh_gate, w_sh_up, w_sh_down, g_final):
    assert w_ada.shape[0] == 1, "single-layer stack"
    B = x.shape[0]
    mod3 = _ada(c, w_ada[0], b_ada[0]).reshape(B, 6, D_MODEL)
    return _layer(x, mod3, g_norm_mix[0], w_in[0], sgu_ln_gain[0], sgu_ln_bias[0], w_spatial[0],
                  b_spatial[0], g_out_attn[0], g_out_sgu[0], w_out[0], g_norm_ffn[0], w_router[0],
                  b_router[0], w_exp_gate[0], w_exp_up[0], w_exp_down[0], w_sh_gate[0], w_sh_up[0],
                  w_sh_down[0], g_final)
```

```python
import functools

import jax
import jax.numpy as jnp
from jax import lax
from jax.experimental import pallas as pl
from jax.experimental.pallas import tpu as pltpu

F32 = jnp.float32
BF16 = jnp.bfloat16

D_MODEL = 1024
ATTN_WIDTH = 512
ATTN_HEADS = 8
HEAD_DIM = 64
SGU_WIDTH = 512
SGU_GROUPS = 4
SGU_GROUP_DIM = 128
CHUNK = 128
DILATED_BRANCHES = ((128, 1), (512, 4), (2048, 16))
ATTN_BLOCK = 128
N_EXPERTS = 256
TOP_K = 8
N_EXPERT_GROUPS = 8
GROUP_SIZE = N_EXPERTS // N_EXPERT_GROUPS
TOPK_GROUPS = 4
EXPERT_DIM = 256
ROUTED_SCALE = 2.5
EPS = 1e-6

LANES = 128
HALF = D_MODEL // 2
ROW_BLOCK = 256
NEG_BIG = -1e30
VMEM_LIMIT = 56 * 1024 * 1024

TM_PROJ = 512
TM_ROUTE = 512
TM_MOVE = 256
ATTN_UNROLL = 4


def _dot(a, b):
    return jnp.dot(a, b, preferred_element_type=F32)


def _dot_nt(a, b):
    return lax.dot_general(a, b, (((1,), (1,)), ((), ())), preferred_element_type=F32)


def _rms(x, g):
    return x * lax.rsqrt(jnp.mean(x * x, axis=-1, keepdims=True) + EPS) * g


def _pack_halves(x):
    return pltpu.pack_elementwise([x[:, :HALF], x[:, HALF:]], packed_dtype=BF16)


def _unpack_halves(w):
    lo = pltpu.unpack_elementwise(w, index=0, packed_dtype=BF16, unpacked_dtype=F32)
    hi = pltpu.unpack_elementwise(w, index=1, packed_dtype=BF16, unpacked_dtype=F32)
    return lo, hi


def _params(sem=None):
    return pltpu.CompilerParams(dimension_semantics=sem, vmem_limit_bytes=VMEM_LIMIT)


def _ada_kernel(c_ref, w_ref, b_ref, o_ref):
    c = c_ref[...]
    cond = c * jax.nn.sigmoid(c)
    ch = cond.astype(BF16)
    cl = (cond - ch.astype(F32)).astype(BF16)
    w = w_ref[...]
    wh = w.astype(BF16)
    wl = (w - wh.astype(F32)).astype(BF16)
    o_ref[...] = _dot(ch, wh) + _dot(cl, wh) + _dot(ch, wl) + b_ref[...]


def _ada(c, w_ada, b_ada):
    B = c.shape[0]
    n_out = w_ada.shape[1]
    tn = D_MODEL
    return pl.pallas_call(
        _ada_kernel,
        out_shape=jax.ShapeDtypeStruct((B, n_out), F32),
        grid=(n_out // tn,),
        in_specs=[
            pl.BlockSpec((B, D_MODEL), lambda j: (0, 0)),
            pl.BlockSpec((D_MODEL, tn), lambda j: (0, j)),
            pl.BlockSpec((1, tn), lambda j: (0, j)),
        ],
        out_specs=pl.BlockSpec((B, tn), lambda j: (0, j)),
        compiler_params=_params(("arbitrary",)),
        name="ada",
    )(c, w_ada, b_ada.reshape(1, n_out))


def _inproj_kernel(x_ref, mod_ref, g_ref, w_ref, q_ref, k_ref, v_ref, u_ref, z_ref):
    x = x_ref[0]
    shift = mod_ref[0, 0:1, :]
    scale = mod_ref[0, 1:2, :]
    n = _rms(x, g_ref[...]) * (1.0 + scale) + shift
    p = _dot(n.astype(BF16), w_ref[...])
    q_ref[0] = (p[:, 0:ATTN_WIDTH] * (HEAD_DIM ** -0.5)).astype(BF16)
    k_ref[0] = p[:, ATTN_WIDTH:2 * ATTN_WIDTH].astype(BF16)
    v_ref[0] = p[:, 2 * ATTN_WIDTH:3 * ATTN_WIDTH].astype(BF16)
    u_ref[0] = p[:, 3 * ATTN_WIDTH:3 * ATTN_WIDTH + SGU_WIDTH].astype(BF16)
    z_ref[0] = p[:, 3 * ATTN_WIDTH + SGU_WIDTH:].astype(BF16)


def _inproj(x, mod3, g_norm, w_in_bf):
    B, S, _ = x.shape
    tm = TM_PROJ
    n_in = w_in_bf.shape[1]
    out = jax.ShapeDtypeStruct((B, S, ATTN_WIDTH), BF16)
    tile = pl.BlockSpec((1, tm, ATTN_WIDTH), lambda b, i: (b, i, 0))
    return pl.pallas_call(
        _inproj_kernel,
        out_shape=(out,) * 5,
        grid=(B, S // tm),
        in_specs=[
            pl.BlockSpec((1, tm, D_MODEL), lambda b, i: (b, i, 0)),
            pl.BlockSpec((1, 6, D_MODEL), lambda b, i: (b, 0, 0)),
            pl.BlockSpec((1, D_MODEL), lambda b, i: (0, 0)),
            pl.BlockSpec((D_MODEL, n_in), lambda b, i: (0, 0)),
        ],
        out_specs=(tile,) * 5,
        compiler_params=_params(("arbitrary", "arbitrary")),
        name="inproj",
    )(x, mod3, g_norm.reshape(1, D_MODEL), w_in_bf)


def _attn_kernel(slope_ref, q_ref, k_ref, v_ref, o_ref, qf, kf, vf, oacc, lacc, tbl, *, seq):
    qf[...] = q_ref[0].astype(F32)
    kf[...] = k_ref[0].astype(F32)
    vf[...] = v_ref[0].astype(F32)

    lane = lax.broadcasted_iota(jnp.int32, (1, LANES), 1)
    head0 = lane < HEAD_DIM
    slopes = slope_ref[0]
    slope_h = (slopes[:, 0:1], slopes[:, HEAD_DIM:HEAD_DIM + 1])

    qr = lax.broadcasted_iota(jnp.int32, (ATTN_BLOCK, 2 * ATTN_BLOCK), 0)
    kc = lax.broadcasted_iota(jnp.int32, (ATTN_BLOCK, 2 * ATTN_BLOCK), 1)
    for bi, (window, dil) in enumerate(DILATED_BRANCHES):
        steps = window // dil
        for var in range(2):
            back = qr - kc + var * ATTN_BLOCK
            valid = (back >= 0) & (back <= steps)
            dist = (back * dil).astype(F32)
            for hh in range(2):
                tbl[bi, var, hh * ATTN_BLOCK:(hh + 1) * ATTN_BLOCK, :] = jnp.where(
                    valid, -slope_h[hh] * dist, NEG_BIG)

    for bi, (window, dil) in enumerate(DILATED_BRANCHES):
        cls_len = seq // dil
        nb = cls_len // ATTN_BLOCK
        nb_shift = nb.bit_length() - 1

        def rows(start, size, dil=dil):
            if dil == 1:
                return pl.ds(pl.multiple_of(start, ATTN_BLOCK), size)
            return pl.ds(start, size, stride=dil)

        def block(it, bi=bi, dil=dil, nb=nb, nb_shift=nb_shift, rows=rows):
            r = lax.shift_right_logical(it, nb_shift)
            i = it & (nb - 1)
            var = jnp.minimum(i, 1)
            q_rows = rows(i * ATTN_BLOCK * dil + r, ATTN_BLOCK)
            k_rows = rows((i - var) * ATTN_BLOCK * dil + r, 2 * ATTN_BLOCK)
            q2 = qf[q_rows, :]
            kb = kf[k_rows, :].astype(BF16)
            v2 = vf[k_rows, :]
            qs = jnp.concatenate([jnp.where(head0, q2, 0.0), jnp.where(head0, 0.0, q2)], axis=0)
            s = _dot_nt(qs.astype(BF16), kb) + tbl[bi, var]
            m = jnp.max(s, axis=-1, keepdims=True)
            p = jnp.exp(s - m)
            den = jnp.sum(p, axis=-1, keepdims=True)
            pb = p.astype(BF16)
            vs = jnp.concatenate([jnp.where(head0, v2, 0.0), jnp.where(head0, 0.0, v2)], axis=0)
            o = _dot(jnp.concatenate([pb[:ATTN_BLOCK], pb[ATTN_BLOCK:]], axis=1), vs.astype(BF16))
            den2 = jnp.where(head0, den[:ATTN_BLOCK], den[ATTN_BLOCK:])
            lse = jnp.where(head0, m[:ATTN_BLOCK], m[ATTN_BLOCK:]) + jnp.log(den2)
            return q_rows, o / den2, lse

        def body(step, carry, bi=bi, block=block):
            done = [block(step * ATTN_UNROLL + j) for j in range(ATTN_UNROLL)]
            if bi > 0:
                merged = []
                for q_rows, o, lse in done:
                    l_old = lacc[q_rows, :]
                    m2 = jnp.maximum(l_old, lse)
                    a = jnp.exp(l_old - m2)
                    b = jnp.exp(lse - m2)
                    tot = a + b
                    merged.append((q_rows, (oacc[q_rows, :] * a + o * b) / tot, m2 + jnp.log(tot)))
                done = merged
            for q_rows, o, lse in done:
                oacc[q_rows, :] = o
                lacc[q_rows, :] = lse
            return carry

        assert (dil * nb) % ATTN_UNROLL == 0
        lax.fori_loop(0, dil * nb // ATTN_UNROLL, body, 0)

    o_ref[0] = oacc[...].astype(BF16)


def _attention(q, k, v, slopes_lane):
    B, S, _ = q.shape
    for window, dil in DILATED_BRANCHES:
        assert window // dil <= ATTN_BLOCK
        cls_len = S // dil
        assert S % dil == 0 and cls_len % ATTN_BLOCK == 0 and cls_len >= 2 * ATTN_BLOCK
        assert (cls_len // ATTN_BLOCK) & (cls_len // ATTN_BLOCK - 1) == 0
    n_pairs = ATTN_WIDTH // LANES
    tile = pl.BlockSpec((1, S, LANES), lambda b, p: (b, 0, p))
    return pl.pallas_call(
        functools.partial(_attn_kernel, seq=S),
        out_shape=jax.ShapeDtypeStruct((B, S, ATTN_WIDTH), BF16),
        grid=(B, n_pairs),
        in_specs=[pl.BlockSpec((1, 1, LANES), lambda b, p: (p, 0, 0)), tile, tile, tile],
        out_specs=tile,
        scratch_shapes=[pltpu.VMEM((S, LANES), F32)] * 5
        + [pltpu.VMEM((len(DILATED_BRANCHES), 2, 2 * ATTN_BLOCK, 2 * ATTN_BLOCK), F32)],
        compiler_params=_params(("arbitrary", "arbitrary")),
        name="attn",
    )(slopes_lane, q, k, v)


def _mix_kernel(attn_ref, u_ref, z_ref, x_ref, mod_ref, ga_ref, gs_ref, lng_ref, lnb_ref, wsp_ref,
                bsp_ref, wout_ref, gffn_ref, wrt_ref, wsgu_ref, wsd_ref,
                hs_ref, n2p_ref, lg_ref):
    tm = x_ref.shape[1]
    nc = tm // CHUNK
    a_n = _rms(attn_ref[0].astype(F32), ga_ref[...])

    ug = jax.nn.gelu(u_ref[0].astype(F32))
    zg = jax.nn.gelu(z_ref[0].astype(F32))
    mu = jnp.mean(zg, axis=-1, keepdims=True)
    zc = zg - mu
    var = jnp.mean(zc * zc, axis=-1, keepdims=True)
    zb = (zc * lax.rsqrt(var + EPS) * lng_ref[...] + lnb_ref[...]).astype(BF16)

    row = lax.broadcasted_iota(jnp.int32, (CHUNK, CHUNK), 0)
    col = lax.broadcasted_iota(jnp.int32, (CHUNK, CHUNK), 1)
    per_group = []
    for g in range(SGU_GROUPS):
        wc = jnp.where(row >= col, wsp_ref[g], 0.0).astype(BF16)
        lanes = slice(g * SGU_GROUP_DIM, (g + 1) * SGU_GROUP_DIM)
        zcat = jnp.concatenate([zb[c * CHUNK:(c + 1) * CHUNK, lanes] for c in range(nc)], axis=1)
        per_group.append(_dot(wc, zcat) + bsp_ref[:, g:g + 1])
    mixed = jnp.concatenate(
        [jnp.concatenate([per_group[g][:, c * CHUNK:(c + 1) * CHUNK] for g in range(SGU_GROUPS)], axis=1)
         for c in range(nc)], axis=0)
    s_n = _rms(ug * mixed, gs_ref[...])

    mix = (_dot(a_n.astype(BF16), wout_ref[0:ATTN_WIDTH, :])
           + _dot(s_n.astype(BF16), wout_ref[ATTN_WIDTH:, :]))
    gate1 = mod_ref[0, 2:3, :]
    shift2 = mod_ref[0, 3:4, :]
    scale2 = mod_ref[0, 4:5, :]
    gate2 = mod_ref[0, 5:6, :]
    h1 = x_ref[0] + gate1 * mix
    n2 = _rms(h1, gffn_ref[...]) * (1.0 + scale2) + shift2
    n2b = n2.astype(BF16)

    lg_ref[...] = _dot_nt(wrt_ref[...], n2b)
    gu = _dot(n2b, wsgu_ref[...])
    gsh = gu[:, :EXPERT_DIM]
    act = (gsh * jax.nn.sigmoid(gsh)) * gu[:, EXPERT_DIM:]
    shared = _dot(act.astype(BF16), wsd_ref[...])
    hs_ref[0] = h1 + gate2 * shared
    n2p_ref[...] = _pack_halves(n2)


def _mix(attn, u, z, x, mod3, g_out_attn, g_out_sgu, ln_g, ln_b, w_spatial, b_spatial_t, w_out_bf,
         g_norm_ffn, w_router_t_bf, w_sh_gu_bf, w_sh_d_bf):
    B, S, _ = x.shape
    tm = TM_PROJ
    nt = S // tm
    N = B * S
    half_tile = pl.BlockSpec((1, tm, ATTN_WIDTH), lambda b, i: (b, i, 0))
    full_tile = pl.BlockSpec((1, tm, D_MODEL), lambda b, i: (b, i, 0))

    def const(shape):
        return pl.BlockSpec(shape, lambda b, i: (0,) * len(shape))

    return pl.pallas_call(
        _mix_kernel,
        out_shape=(
            jax.ShapeDtypeStruct((B, S, D_MODEL), F32),
            jax.ShapeDtypeStruct((N, HALF), jnp.uint32),
            jax.ShapeDtypeStruct((N_EXPERTS, N), F32),
        ),
        grid=(B, nt),
        in_specs=[
            half_tile, half_tile, half_tile, full_tile,
            pl.BlockSpec((1, 6, D_MODEL), lambda b, i: (b, 0, 0)),
            const((1, ATTN_WIDTH)), const((1, SGU_WIDTH)), const((1, SGU_WIDTH)), const((1, SGU_WIDTH)),
            const((SGU_GROUPS, CHUNK, CHUNK)), const((CHUNK, SGU_GROUPS)),
            const((ATTN_WIDTH + SGU_WIDTH, D_MODEL)), const((1, D_MODEL)),
            const((N_EXPERTS, D_MODEL)), const((D_MODEL, 2 * EXPERT_DIM)), const((EXPERT_DIM, D_MODEL)),
        ],
        out_specs=(
            full_tile,
            pl.BlockSpec((tm, HALF), lambda b, i: (b * nt + i, 0)),
            pl.BlockSpec((N_EXPERTS, tm), lambda b, i: (0, b * nt + i)),
        ),
        compiler_params=_params(("arbitrary", "arbitrary")),
        name="mix",
    )(attn, u, z, x, mod3, g_out_attn.reshape(1, -1), g_out_sgu.reshape(1, -1), ln_g.reshape(1, -1),
      ln_b.reshape(1, -1), w_spatial, b_spatial_t, w_out_bf, g_norm_ffn.reshape(1, -1), w_router_t_bf,
      w_sh_gu_bf, w_sh_d_bf)


def _first_max(v, iota, size):
    mx = jnp.max(v, axis=0, keepdims=True)
    am = jnp.min(jnp.where(v == mx, iota, size), axis=0, keepdims=True)
    return mx, am


def _route_kernel(lg_ref, br_ref, idx_ref, gate_ref, rank_ref, cnt_ref, carry, tri):
    step = pl.program_id(0)
    tr = lg_ref.shape[1]

    @pl.when(step == 0)
    def _():
        carry[...] = jnp.zeros_like(carry)
        before = (lax.broadcasted_iota(jnp.int32, (tr, tr), 0)
                  < lax.broadcasted_iota(jnp.int32, (tr, tr), 1))
        tri[...] = jnp.where(before, 1.0, 0.0).astype(BF16)

    scores = jax.nn.sigmoid(lg_ref[...])
    choice = scores + br_ref[...]

    iota_g = lax.broadcasted_iota(jnp.int32, (GROUP_SIZE, tr), 0)
    gs = []
    for g in range(N_EXPERT_GROUPS):
        cg = choice[g * GROUP_SIZE:(g + 1) * GROUP_SIZE, :]
        m1, am = _first_max(cg, iota_g, GROUP_SIZE)
        m2 = jnp.max(jnp.where(iota_g == am, -jnp.inf, cg), axis=0, keepdims=True)
        gs.append(m1 + m2)
    gscore = jnp.concatenate(gs, axis=0)

    iota_n = lax.broadcasted_iota(jnp.int32, (N_EXPERT_GROUPS, tr), 0)
    t = gscore
    for _ in range(TOPK_GROUPS - 1):
        _, am = _first_max(t, iota_n, N_EXPERT_GROUPS)
        t = jnp.where(iota_n == am, -jnp.inf, t)
    kth = jnp.max(t, axis=0, keepdims=True)
    keep = gscore >= kth

    v = jnp.concatenate(
        [jnp.where(keep[g:g + 1, :], choice[g * GROUP_SIZE:(g + 1) * GROUP_SIZE, :], -jnp.inf)
         for g in range(N_EXPERT_GROUPS)], axis=0)
    iota_e = lax.broadcasted_iota(jnp.int32, (N_EXPERTS, tr), 0)
    idxs, sels = [], []
    chosen = jnp.zeros((N_EXPERTS, tr), F32)
    for _ in range(TOP_K):
        _, am = _first_max(v, iota_e, N_EXPERTS)
        hit = iota_e == am
        idxs.append(am)
        sels.append(jnp.sum(jnp.where(hit, scores, 0.0), axis=0, keepdims=True))
        chosen = jnp.where(hit, 1.0, chosen)
        v = jnp.where(hit, -jnp.inf, v)
    sel = jnp.concatenate(sels, axis=0)
    idx_ref[...] = jnp.concatenate(idxs, axis=0)
    gate_ref[...] = sel / jnp.sum(sel, axis=0, keepdims=True) * ROUTED_SCALE

    earlier = _dot(chosen.astype(BF16), tri[...]) + carry[...]
    rank_ref[...] = jnp.concatenate(
        [jnp.sum(jnp.where(iota_e == am, earlier, 0.0), axis=0, keepdims=True) for am in idxs],
        axis=0).astype(jnp.int32)
    carry[...] = carry[...] + jnp.sum(chosen, axis=1, keepdims=True)
    cnt_ref[...] = jnp.broadcast_to(carry[...], cnt_ref.shape)


def _route(logits_t, b_router):
    N = logits_t.shape[1]
    tr = TM_ROUTE
    kt = pl.BlockSpec((TOP_K, tr), lambda i: (0, i))
    return pl.pallas_call(
        _route_kernel,
        out_shape=(
            jax.ShapeDtypeStruct((TOP_K, N), jnp.int32),
            jax.ShapeDtypeStruct((TOP_K, N), F32),
            jax.ShapeDtypeStruct((TOP_K, N), jnp.int32),
            jax.ShapeDtypeStruct((N_EXPERTS, LANES), F32),
        ),
        grid=(N // tr,),
        in_specs=[
            pl.BlockSpec((N_EXPERTS, tr), lambda i: (0, i)),
            pl.BlockSpec((N_EXPERTS, 1), lambda i: (0, 0)),
        ],
        out_specs=(kt, kt, kt, pl.BlockSpec((N_EXPERTS, LANES), lambda i: (0, 0))),
        scratch_shapes=[pltpu.VMEM((N_EXPERTS, 1), F32), pltpu.VMEM((tr, tr), BF16)],
        compiler_params=_params(("arbitrary",)),
        name="route",
    )(logits_t, b_router.reshape(N_EXPERTS, 1))


def _dest_kernel(idx_ref, rank_ref, ps_ref, dest_ref):
    tr = idx_ref.shape[1]
    iota_e = lax.broadcasted_iota(jnp.int32, (N_EXPERTS, tr), 0)
    idx = idx_ref[...]
    start = jnp.concatenate(
        [jnp.sum(jnp.where(iota_e == idx[k:k + 1, :], ps_ref[...], 0.0), axis=0, keepdims=True)
         for k in range(TOP_K)], axis=0)
    dest_ref[...] = start.astype(jnp.int32) + rank_ref[...]


def _dest(idx_t, rank_t, pstart):
    N = idx_t.shape[1]
    tr = TM_ROUTE
    kt = pl.BlockSpec((TOP_K, tr), lambda i: (0, i))
    return pl.pallas_call(
        _dest_kernel,
        out_shape=jax.ShapeDtypeStruct((TOP_K, N), jnp.int32),
        grid=(N // tr,),
        in_specs=[kt, kt, pl.BlockSpec((N_EXPERTS, 1), lambda i: (0, 0))],
        out_specs=kt,
        compiler_params=_params(("arbitrary",)),
        name="dest",
    )(idx_t, rank_t, pstart.astype(F32).reshape(N_EXPERTS, 1))


def _row_copy(src, dst, sem):
    return pltpu.make_async_copy(src, dst, sem)


def _disp_kernel(zb_ref, dest_ref, x_ref, xs_ref, zbuf, sem):
    tm = x_ref.shape[0]

    @pl.when(pl.program_id(0) == 0)
    def _():
        zbuf[...] = jnp.zeros_like(zbuf)

        def zero(i, started):
            blk = zb_ref[i]

            @pl.when(blk >= 0)
            def _():
                start = pl.multiple_of(blk * ROW_BLOCK, ROW_BLOCK)
                _row_copy(zbuf, xs_ref.at[pl.ds(start, ROW_BLOCK)], sem).start()

            return started + jnp.where(blk >= 0, 1, 0)

        def done(i, carry):
            _row_copy(zbuf, xs_ref.at[pl.ds(0, ROW_BLOCK)], sem).wait()
            return carry

        started = lax.fori_loop(0, zb_ref.shape[0], zero, 0)
        lax.fori_loop(0, started, done, 0)

    def issue(t, carry):
        for k in range(TOP_K):
            _row_copy(x_ref.at[pl.ds(t, 1)], xs_ref.at[pl.ds(dest_ref[k, t], 1)], sem).start()
        return carry

    lax.fori_loop(0, tm, issue, 0)
    for k in range(TOP_K):
        _row_copy(x_ref, xs_ref.at[pl.ds(0, tm)], sem).wait()


def _dispatch(zero_blocks, dest_t, n2p, n_rows):
    N = n2p.shape[0]
    tm = TM_MOVE
    grid_spec = pltpu.PrefetchScalarGridSpec(
        num_scalar_prefetch=1,
        grid=(N // tm,),
        in_specs=[
            pl.BlockSpec((TOP_K, tm), lambda i, zb: (0, i), memory_space=pltpu.SMEM),
            pl.BlockSpec((tm, HALF), lambda i, zb: (i, 0)),
        ],
        out_specs=pl.BlockSpec(memory_space=pl.ANY),
        scratch_shapes=[pltpu.VMEM((ROW_BLOCK, HALF), jnp.uint32), pltpu.SemaphoreType.DMA],
    )
    return pl.pallas_call(
        _disp_kernel,
        out_shape=jax.ShapeDtypeStruct((n_rows, HALF), jnp.uint32),
        grid_spec=grid_spec,
        compiler_params=_params(("arbitrary",)),
        name="disp",
    )(zero_blocks, dest_t, n2p)


def _experts_kernel(be_ref, bv_ref, nu_ref, xs_ref, wg_ref, wu_ref, wd_ref, ys_ref, wgb, wub, wdb):
    j = pl.program_id(0)

    @pl.when(j < nu_ref[0])
    def _():
        e = be_ref[j]
        prev = be_ref[jnp.maximum(j - 1, 0)]

        @pl.when(jnp.logical_or(j == 0, e != prev))
        def _():
            wgb[...] = wg_ref[...].astype(BF16)
            wub[...] = wu_ref[...].astype(BF16)
            wdb[...] = wd_ref[...].astype(BF16)

        lo, hi = _unpack_halves(xs_ref[...])
        live = lax.broadcasted_iota(jnp.int32, lo.shape, 0) < bv_ref[j]
        lo = jnp.where(live, lo, 0.0).astype(BF16)
        hi = jnp.where(live, hi, 0.0).astype(BF16)
        g = _dot(lo, wgb[0:HALF, :]) + _dot(hi, wgb[HALF:, :])
        u = _dot(lo, wub[0:HALF, :]) + _dot(hi, wub[HALF:, :])
        act = (g * jax.nn.sigmoid(g)) * u
        ys_ref[...] = _pack_halves(_dot(act.astype(BF16), wdb[...]))

    @pl.when(j >= nu_ref[0])
    def _():
        ys_ref[...] = jnp.zeros_like(ys_ref)


def _experts(block_e, block_valid, n_used, xs, w_gate, w_up, w_down):
    n_blocks = xs.shape[0] // ROW_BLOCK

    def row_map(j, be, bv, nu):
        return (jnp.minimum(j, nu[0] - 1), 0)

    def out_map(j, be, bv, nu):
        return (j, 0)

    def w_map(j, be, bv, nu):
        return (be[jnp.minimum(j, nu[0] - 1)], 0, 0)

    grid_spec = pltpu.PrefetchScalarGridSpec(
        num_scalar_prefetch=3,
        grid=(n_blocks,),
        in_specs=[
            pl.BlockSpec((ROW_BLOCK, HALF), row_map),
            pl.BlockSpec((None, D_MODEL, EXPERT_DIM), w_map),
            pl.BlockSpec((None, D_MODEL, EXPERT_DIM), w_map),
            pl.BlockSpec((None, EXPERT_DIM, D_MODEL), w_map),
        ],
        out_specs=pl.BlockSpec((ROW_BLOCK, HALF), out_map),
        scratch_shapes=[
            pltpu.VMEM((D_MODEL, EXPERT_DIM), BF16),
            pltpu.VMEM((D_MODEL, EXPERT_DIM), BF16),
            pltpu.VMEM((EXPERT_DIM, D_MODEL), BF16),
        ],
    )
    return pl.pallas_call(
        _experts_kernel,
        out_shape=jax.ShapeDtypeStruct(xs.shape, jnp.uint32),
        grid_spec=grid_spec,
        compiler_params=_params(("arbitrary",)),
        name="experts",
    )(block_e, block_valid, n_used, xs, w_gate, w_up, w_down)


def _comb_kernel(dest_ref, ys_ref, hs_ref, gate_ref, mod_ref, gf_ref, o_ref, buf, sem):
    tm = hs_ref.shape[0]

    def issue(t, carry):
        for k in range(TOP_K):
            _row_copy(ys_ref.at[pl.ds(dest_ref[k, t], 1)], buf.at[k, pl.ds(t, 1)], sem).start()
        return carry

    lax.fori_loop(0, tm, issue, 0)
    for k in range(TOP_K):
        _row_copy(ys_ref.at[pl.ds(0, tm)], buf.at[k], sem).wait()

    gates = gate_ref[...]
    lo = jnp.zeros((tm, HALF), F32)
    hi = jnp.zeros((tm, HALF), F32)
    for k in range(TOP_K):
        lo_k, hi_k = _unpack_halves(buf[k])
        gk = gates[:, k:k + 1]
        lo = lo + gk * lo_k
        hi = hi + gk * hi_k
    routed = jnp.concatenate([lo, hi], axis=1)
    h2 = hs_ref[...] + mod_ref[0, 5:6, :] * routed
    o_ref[...] = _rms(h2, gf_ref[...])


def _combine(dest_t, ys, hs2, gates_nk, mod3, g_final, seq):
    N = hs2.shape[0]
    tm = TM_MOVE
    per_seq = seq // tm
    return pl.pallas_call(
        _comb_kernel,
        out_shape=jax.ShapeDtypeStruct((N, D_MODEL), F32),
        grid=(N // tm,),
        in_specs=[
            pl.BlockSpec((TOP_K, tm), lambda i: (0, i), memory_space=pltpu.SMEM),
            pl.BlockSpec(memory_space=pl.ANY),
            pl.BlockSpec((tm, D_MODEL), lambda i: (i, 0)),
            pl.BlockSpec((tm, TOP_K), lambda i: (i, 0)),
            pl.BlockSpec((1, 6, D_MODEL), lambda i: (i // per_seq, 0, 0)),
            pl.BlockSpec((1, D_MODEL), lambda i: (0, 0)),
        ],
        out_specs=pl.BlockSpec((tm, D_MODEL), lambda i: (i, 0)),
        scratch_shapes=[pltpu.VMEM((TOP_K, tm, HALF), jnp.uint32), pltpu.SemaphoreType.DMA],
        compiler_params=_params(("arbitrary",)),
        name="comb",
    )(dest_t, ys, hs2, gates_nk, mod3, g_final.reshape(1, D_MODEL))


def _layer(x, mod3, g_norm_mix, w_in, sgu_ln_gain, sgu_ln_bias, w_spatial, b_spatial, g_out_attn,
           g_out_sgu, w_out, g_norm_ffn, w_router, b_router, w_exp_gate, w_exp_up, w_exp_down,
           w_sh_gate, w_sh_up, w_sh_down, g_final):
    B, S, _ = x.shape
    N = B * S

    q, k, v, u, z = _inproj(x, mod3, g_norm_mix, w_in.astype(BF16))
    head_of_lane = jnp.arange(ATTN_WIDTH) // HEAD_DIM
    slopes = jnp.exp2(-8.0 * jnp.arange(1, ATTN_HEADS + 1, dtype=F32) / ATTN_HEADS)
    slopes_lane = slopes[head_of_lane].reshape(ATTN_WIDTH // LANES, 1, LANES)
    attn = _attention(q, k, v, slopes_lane)

    hs, n2p, logits_t = _mix(
        attn, u, z, x, mod3, g_out_attn, g_out_sgu, sgu_ln_gain, sgu_ln_bias, w_spatial, b_spatial.T,
        w_out.astype(BF16), g_norm_ffn, w_router.T.astype(BF16),
        jnp.concatenate([w_sh_gate, w_sh_up], axis=1).astype(BF16), w_sh_down.astype(BF16))

    idx_t, gates_t, rank_t, counts = _route(logits_t, b_router)

    n_blocks = (N * TOP_K + N_EXPERTS * (ROW_BLOCK - 1)) // ROW_BLOCK
    cnt = counts[:, 0].astype(jnp.int32)
    padded = (cnt + ROW_BLOCK - 1) // ROW_BLOCK * ROW_BLOCK
    pends = jnp.cumsum(padded)
    pstart = pends - padded
    block_row = jnp.arange(n_blocks, dtype=jnp.int32) * ROW_BLOCK
    block_e = jnp.sum(pends[None, :] <= block_row[:, None], axis=1, dtype=jnp.int32)
    block_e = jnp.minimum(block_e, N_EXPERTS - 1)
    own = block_e[:, None] == jnp.arange(N_EXPERTS, dtype=jnp.int32)[None, :]
    live_end = jnp.sum(jnp.where(own, (pstart + cnt)[None, :], 0), axis=1)
    block_valid = jnp.clip(live_end - block_row, 0, ROW_BLOCK).astype(jnp.int32)
    n_used = (pends[-1:] // ROW_BLOCK).astype(jnp.int32)

    last_block = jnp.where(cnt > 0, pends // ROW_BLOCK - 1, -1)
    tail_block = n_used[0] + jnp.arange(n_blocks - N * TOP_K // ROW_BLOCK)
    tail_block = jnp.where(tail_block < n_blocks, tail_block, -1)
    zero_blocks = jnp.concatenate([last_block, tail_block]).astype(jnp.int32)

    dest_t = _dest(idx_t, rank_t, pstart)
    xs = _dispatch(zero_blocks, dest_t, n2p, n_blocks * ROW_BLOCK)
    ys = _experts(block_e, block_valid, n_used, xs, w_exp_gate, w_exp_up, w_exp_down)
    out = _combine(dest_t, ys, hs.reshape(N, D_MODEL), gates_t.T, mod3, g_final, S)
    return out.reshape(B, S, D_MODEL)


def kernel(x, c, w_ada, b_ada, g_norm_mix, w_in, sgu_ln_gain, sgu_ln_bias, w_spatial, b_spatial, g_out_attn, g_out_sgu, w_out, g_norm_ffn, w_router, b_router, w_exp_gate, w_exp_up, w_exp_down, w_sh_gate, w_sh_up, w_sh_down, g_final):
    assert w_ada.shape[0] == 1, "single-layer stack"
    B = x.shape[0]
    mod3 = _ada(c, w_ada[0], b_ada[0]).reshape(B, 6, D_MODEL)
    return _layer(x, mod3, g_norm_mix[0], w_in[0], sgu_ln_gain[0], sgu_ln_bias[0], w_spatial[0],
                  b_spatial[0], g_out_attn[0], g_out_sgu[0], w_out[0], g_norm_ffn[0], w_router[0],
                  b_router[0], w_exp_gate[0], w_exp_up[0], w_exp_down[0], w_sh_gate[0], w_sh_up[0],
                  w_sh_down[0], g_final)
```

```python
import functools

import jax
import jax.numpy as jnp
from jax import lax
from jax.experimental import pallas as pl
from jax.experimental.pallas import tpu as pltpu

F32 = jnp.float32
BF16 = jnp.bfloat16

D_MODEL = 1024
ATTN_WIDTH = 512
ATTN_HEADS = 8
HEAD_DIM = 64
SGU_WIDTH = 512
SGU_GROUPS = 4
SGU_GROUP_DIM = 128
CHUNK = 128
DILATED_BRANCHES = ((128, 1), (512, 4), (2048, 16))
ATTN_BLOCK = 128
N_EXPERTS = 256
TOP_K = 8
N_EXPERT_GROUPS = 8
GROUP_SIZE = N_EXPERTS // N_EXPERT_GROUPS
TOPK_GROUPS = 4
EXPERT_DIM = 256
ROUTED_SCALE = 2.5
EPS = 1e-6

LANES = 128
HALF = D_MODEL // 2
ROW_BLOCK = 256
NEG_BIG = -1e30
VMEM_LIMIT = 56 * 1024 * 1024

TM_PROJ = 512
TM_ROUTE = 512
TM_MOVE = 256
ATTN_UNROLL = 4


def _dot(a, b):
    return jnp.dot(a, b, preferred_element_type=F32)


def _dot_nt(a, b):
    return lax.dot_general(a, b, (((1,), (1,)), ((), ())), preferred_element_type=F32)


def _rms(x, g):
    return x * lax.rsqrt(jnp.mean(x * x, axis=-1, keepdims=True) + EPS) * g


def _pack_halves(x):
    return pltpu.pack_elementwise([x[:, :HALF], x[:, HALF:]], packed_dtype=BF16)


def _unpack_halves(w):
    lo = pltpu.unpack_elementwise(w, index=0, packed_dtype=BF16, unpacked_dtype=F32)
    hi = pltpu.unpack_elementwise(w, index=1, packed_dtype=BF16, unpacked_dtype=F32)
    return lo, hi


ROW_SPLIT = HALF // LANES
BLOCK_SUBROWS = ROW_BLOCK * ROW_SPLIT


def _store_rows(ref, packed):
    rows = packed.shape[0]
    for c in range(ROW_SPLIT):
        ref[pl.ds(c, rows, stride=ROW_SPLIT), :] = packed[:, c * LANES:(c + 1) * LANES]


def _load_rows(ref, rows):
    return jnp.concatenate([ref[pl.ds(c, rows, stride=ROW_SPLIT), :] for c in range(ROW_SPLIT)], axis=1)


def _params(sem=None):
    return pltpu.CompilerParams(dimension_semantics=sem, vmem_limit_bytes=VMEM_LIMIT)


def _ada_kernel(c_ref, w_ref, b_ref, o_ref):
    c = c_ref[...]
    cond = c * jax.nn.sigmoid(c)
    ch = cond.astype(BF16)
    cl = (cond - ch.astype(F32)).astype(BF16)
    w = w_ref[...]
    wh = w.astype(BF16)
    wl = (w - wh.astype(F32)).astype(BF16)
    o_ref[...] = _dot(ch, wh) + _dot(cl, wh) + _dot(ch, wl) + b_ref[...]


def _ada(c, w_ada, b_ada):
    B = c.shape[0]
    n_out = w_ada.shape[1]
    tn = D_MODEL
    return pl.pallas_call(
        _ada_kernel,
        out_shape=jax.ShapeDtypeStruct((B, n_out), F32),
        grid=(n_out // tn,),
        in_specs=[
            pl.BlockSpec((B, D_MODEL), lambda j: (0, 0)),
            pl.BlockSpec((D_MODEL, tn), lambda j: (0, j)),
            pl.BlockSpec((1, tn), lambda j: (0, j)),
        ],
        out_specs=pl.BlockSpec((B, tn), lambda j: (0, j)),
        compiler_params=_params(("arbitrary",)),
        name="ada",
    )(c, w_ada, b_ada.reshape(1, n_out))


def _inproj_kernel(x_ref, mod_ref, g_ref, w_ref, q_ref, k_ref, v_ref, u_ref, z_ref):
    x = x_ref[0]
    shift = mod_ref[0, 0:1, :]
    scale = mod_ref[0, 1:2, :]
    n = _rms(x, g_ref[...]) * (1.0 + scale) + shift
    p = _dot(n.astype(BF16), w_ref[...])
    q_ref[0] = (p[:, 0:ATTN_WIDTH] * (HEAD_DIM ** -0.5)).astype(BF16)
    k_ref[0] = p[:, ATTN_WIDTH:2 * ATTN_WIDTH].astype(BF16)
    v_ref[0] = p[:, 2 * ATTN_WIDTH:3 * ATTN_WIDTH].astype(BF16)
    u_ref[0] = p[:, 3 * ATTN_WIDTH:3 * ATTN_WIDTH + SGU_WIDTH].astype(BF16)
    z_ref[0] = p[:, 3 * ATTN_WIDTH + SGU_WIDTH:].astype(BF16)


def _inproj(x, mod3, g_norm, w_in_bf):
    B, S, _ = x.shape
    tm = TM_PROJ
    n_in = w_in_bf.shape[1]
    out = jax.ShapeDtypeStruct((B, S, ATTN_WIDTH), BF16)
    tile = pl.BlockSpec((1, tm, ATTN_WIDTH), lambda b, i: (b, i, 0))
    return pl.pallas_call(
        _inproj_kernel,
        out_shape=(out,) * 5,
        grid=(B, S // tm),
        in_specs=[
            pl.BlockSpec((1, tm, D_MODEL), lambda b, i: (b, i, 0)),
            pl.BlockSpec((1, 6, D_MODEL), lambda b, i: (b, 0, 0)),
            pl.BlockSpec((1, D_MODEL), lambda b, i: (0, 0)),
            pl.BlockSpec((D_MODEL, n_in), lambda b, i: (0, 0)),
        ],
        out_specs=(tile,) * 5,
        compiler_params=_params(("arbitrary", "arbitrary")),
        name="inproj",
    )(x, mod3, g_norm.reshape(1, D_MODEL), w_in_bf)


def _attn_kernel(slope_ref, q_ref, k_ref, v_ref, o_ref, qf, kf, vf, oacc, lacc, tbl, *, seq):
    qf[...] = q_ref[0].astype(F32)
    kf[...] = k_ref[0].astype(F32)
    vf[...] = v_ref[0].astype(F32)

    lane = lax.broadcasted_iota(jnp.int32, (1, LANES), 1)
    head0 = lane < HEAD_DIM
    slopes = slope_ref[0]
    slope_h = (slopes[:, 0:1], slopes[:, HEAD_DIM:HEAD_DIM + 1])

    qr = lax.broadcasted_iota(jnp.int32, (ATTN_BLOCK, 2 * ATTN_BLOCK), 0)
    kc = lax.broadcasted_iota(jnp.int32, (ATTN_BLOCK, 2 * ATTN_BLOCK), 1)
    for bi, (window, dil) in enumerate(DILATED_BRANCHES):
        steps = window // dil
        for var in range(2):
            back = qr - kc + var * ATTN_BLOCK
            valid = (back >= 0) & (back <= steps)
            dist = (back * dil).astype(F32)
            for hh in range(2):
                tbl[bi, var, hh * ATTN_BLOCK:(hh + 1) * ATTN_BLOCK, :] = jnp.where(
                    valid, -slope_h[hh] * dist, NEG_BIG)

    for bi, (window, dil) in enumerate(DILATED_BRANCHES):
        cls_len = seq // dil
        nb = cls_len // ATTN_BLOCK
        nb_shift = nb.bit_length() - 1

        def rows(start, size, dil=dil):
            if dil == 1:
                return pl.ds(pl.multiple_of(start, ATTN_BLOCK), size)
            return pl.ds(start, size, stride=dil)

        def block(it, bi=bi, dil=dil, nb=nb, nb_shift=nb_shift, rows=rows):
            r = lax.shift_right_logical(it, nb_shift)
            i = it & (nb - 1)
            var = jnp.minimum(i, 1)
            q_rows = rows(i * ATTN_BLOCK * dil + r, ATTN_BLOCK)
            k_rows = rows((i - var) * ATTN_BLOCK * dil + r, 2 * ATTN_BLOCK)
            q2 = qf[q_rows, :]
            kb = kf[k_rows, :].astype(BF16)
            v2 = vf[k_rows, :]
            qs = jnp.concatenate([jnp.where(head0, q2, 0.0), jnp.where(head0, 0.0, q2)], axis=0)
            s = _dot_nt(qs.astype(BF16), kb) + tbl[bi, var]
            m = jnp.max(s, axis=-1, keepdims=True)
            p = jnp.exp(s - m)
            den = jnp.sum(p, axis=-1, keepdims=True)
            pb = p.astype(BF16)
            vs = jnp.concatenate([jnp.where(head0, v2, 0.0), jnp.where(head0, 0.0, v2)], axis=0)
            o = _dot(jnp.concatenate([pb[:ATTN_BLOCK], pb[ATTN_BLOCK:]], axis=1), vs.astype(BF16))
            den2 = jnp.where(head0, den[:ATTN_BLOCK], den[ATTN_BLOCK:])
            lse = jnp.where(head0, m[:ATTN_BLOCK], m[ATTN_BLOCK:]) + jnp.log(den2)
            return q_rows, o / den2, lse

        def body(step, carry, bi=bi, block=block):
            done = [block(step * ATTN_UNROLL + j) for j in range(ATTN_UNROLL)]
            if bi > 0:
                merged = []
                for q_rows, o, lse in done:
                    l_old = lacc[q_rows, :]
                    m2 = jnp.maximum(l_old, lse)
                    a = jnp.exp(l_old - m2)
                    b = jnp.exp(lse - m2)
                    tot = a + b
                    merged.append((q_rows, (oacc[q_rows, :] * a + o * b) / tot, m2 + jnp.log(tot)))
                done = merged
            for q_rows, o, lse in done:
                oacc[q_rows, :] = o
                lacc[q_rows, :] = lse
            return carry

        assert (dil * nb) % ATTN_UNROLL == 0
        lax.fori_loop(0, dil * nb // ATTN_UNROLL, body, 0)

    o_ref[0] = oacc[...].astype(BF16)


def _attention(q, k, v, slopes_lane):
    B, S, _ = q.shape
    for window, dil in DILATED_BRANCHES:
        assert window // dil <= ATTN_BLOCK
        cls_len = S // dil
        assert S % dil == 0 and cls_len % ATTN_BLOCK == 0 and cls_len >= 2 * ATTN_BLOCK
        assert (cls_len // ATTN_BLOCK) & (cls_len // ATTN_BLOCK - 1) == 0
    n_pairs = ATTN_WIDTH // LANES
    tile = pl.BlockSpec((1, S, LANES), lambda b, p: (b, 0, p))
    return pl.pallas_call(
        functools.partial(_attn_kernel, seq=S),
        out_shape=jax.ShapeDtypeStruct((B, S, ATTN_WIDTH), BF16),
        grid=(B, n_pairs),
        in_specs=[pl.BlockSpec((1, 1, LANES), lambda b, p: (p, 0, 0)), tile, tile, tile],
        out_specs=tile,
        scratch_shapes=[pltpu.VMEM((S, LANES), F32)] * 5
        + [pltpu.VMEM((len(DILATED_BRANCHES), 2, 2 * ATTN_BLOCK, 2 * ATTN_BLOCK), F32)],
        compiler_params=_params(("arbitrary", "arbitrary")),
        name="attn",
    )(slopes_lane, q, k, v)


def _mix_kernel(attn_ref, u_ref, z_ref, x_ref, mod_ref, ga_ref, gs_ref, lng_ref, lnb_ref, wsp_ref,
                bsp_ref, wout_ref, gffn_ref, wrt_ref, wsgu_ref, wsd_ref,
                hs_ref, n2p_ref, lg_ref):
    tm = x_ref.shape[1]
    nc = tm // CHUNK
    a_n = _rms(attn_ref[0].astype(F32), ga_ref[...])

    ug = jax.nn.gelu(u_ref[0].astype(F32))
    zg = jax.nn.gelu(z_ref[0].astype(F32))
    mu = jnp.mean(zg, axis=-1, keepdims=True)
    zc = zg - mu
    var = jnp.mean(zc * zc, axis=-1, keepdims=True)
    zb = (zc * lax.rsqrt(var + EPS) * lng_ref[...] + lnb_ref[...]).astype(BF16)

    row = lax.broadcasted_iota(jnp.int32, (CHUNK, CHUNK), 0)
    col = lax.broadcasted_iota(jnp.int32, (CHUNK, CHUNK), 1)
    per_group = []
    for g in range(SGU_GROUPS):
        wc = jnp.where(row >= col, wsp_ref[g], 0.0).astype(BF16)
        lanes = slice(g * SGU_GROUP_DIM, (g + 1) * SGU_GROUP_DIM)
        zcat = jnp.concatenate([zb[c * CHUNK:(c + 1) * CHUNK, lanes] for c in range(nc)], axis=1)
        per_group.append(_dot(wc, zcat) + bsp_ref[:, g:g + 1])
    mixed = jnp.concatenate(
        [jnp.concatenate([per_group[g][:, c * CHUNK:(c + 1) * CHUNK] for g in range(SGU_GROUPS)], axis=1)
         for c in range(nc)], axis=0)
    s_n = _rms(ug * mixed, gs_ref[...])

    mix = (_dot(a_n.astype(BF16), wout_ref[0:ATTN_WIDTH, :])
           + _dot(s_n.astype(BF16), wout_ref[ATTN_WIDTH:, :]))
    gate1 = mod_ref[0, 2:3, :]
    shift2 = mod_ref[0, 3:4, :]
    scale2 = mod_ref[0, 4:5, :]
    gate2 = mod_ref[0, 5:6, :]
    h1 = x_ref[0] + gate1 * mix
    n2 = _rms(h1, gffn_ref[...]) * (1.0 + scale2) + shift2
    n2b = n2.astype(BF16)

    lg_ref[...] = _dot_nt(wrt_ref[...], n2b)
    gu = _dot(n2b, wsgu_ref[...])
    gsh = gu[:, :EXPERT_DIM]
    act = (gsh * jax.nn.sigmoid(gsh)) * gu[:, EXPERT_DIM:]
    shared = _dot(act.astype(BF16), wsd_ref[...])
    hs_ref[0] = h1 + gate2 * shared
    _store_rows(n2p_ref, _pack_halves(n2))


def _mix(attn, u, z, x, mod3, g_out_attn, g_out_sgu, ln_g, ln_b, w_spatial, b_spatial_t, w_out_bf,
         g_norm_ffn, w_router_t_bf, w_sh_gu_bf, w_sh_d_bf):
    B, S, _ = x.shape
    tm = TM_PROJ
    nt = S // tm
    N = B * S
    half_tile = pl.BlockSpec((1, tm, ATTN_WIDTH), lambda b, i: (b, i, 0))
    full_tile = pl.BlockSpec((1, tm, D_MODEL), lambda b, i: (b, i, 0))

    def const(shape):
        return pl.BlockSpec(shape, lambda b, i: (0,) * len(shape))

    return pl.pallas_call(
        _mix_kernel,
        out_shape=(
            jax.ShapeDtypeStruct((B, S, D_MODEL), F32),
            jax.ShapeDtypeStruct((N * ROW_SPLIT, LANES), jnp.uint32),
            jax.ShapeDtypeStruct((N_EXPERTS, N), F32),
        ),
        grid=(B, nt),
        in_specs=[
            half_tile, half_tile, half_tile, full_tile,
            pl.BlockSpec((1, 6, D_MODEL), lambda b, i: (b, 0, 0)),
            const((1, ATTN_WIDTH)), const((1, SGU_WIDTH)), const((1, SGU_WIDTH)), const((1, SGU_WIDTH)),
            const((SGU_GROUPS, CHUNK, CHUNK)), const((CHUNK, SGU_GROUPS)),
            const((ATTN_WIDTH + SGU_WIDTH, D_MODEL)), const((1, D_MODEL)),
            const((N_EXPERTS, D_MODEL)), const((D_MODEL, 2 * EXPERT_DIM)), const((EXPERT_DIM, D_MODEL)),
        ],
        out_specs=(
            full_tile,
            pl.BlockSpec((tm * ROW_SPLIT, LANES), lambda b, i: (b * nt + i, 0)),
            pl.BlockSpec((N_EXPERTS, tm), lambda b, i: (0, b * nt + i)),
        ),
        compiler_params=_params(("arbitrary", "arbitrary")),
        name="mix",
    )(attn, u, z, x, mod3, g_out_attn.reshape(1, -1), g_out_sgu.reshape(1, -1), ln_g.reshape(1, -1),
      ln_b.reshape(1, -1), w_spatial, b_spatial_t, w_out_bf, g_norm_ffn.reshape(1, -1), w_router_t_bf,
      w_sh_gu_bf, w_sh_d_bf)


def _first_max(v, iota, size):
    mx = jnp.max(v, axis=0, keepdims=True)
    am = jnp.min(jnp.where(v == mx, iota, size), axis=0, keepdims=True)
    return mx, am


def _route_kernel(lg_ref, br_ref, idx_ref, gate_ref, rank_ref, cnt_ref, carry, tri):
    step = pl.program_id(0)
    tr = lg_ref.shape[1]

    @pl.when(step == 0)
    def _():
        carry[...] = jnp.zeros_like(carry)
        before = (lax.broadcasted_iota(jnp.int32, (tr, tr), 0)
                  < lax.broadcasted_iota(jnp.int32, (tr, tr), 1))
        tri[...] = jnp.where(before, 1.0, 0.0).astype(BF16)

    scores = jax.nn.sigmoid(lg_ref[...])
    choice = scores + br_ref[...]

    iota_g = lax.broadcasted_iota(jnp.int32, (GROUP_SIZE, tr), 0)
    gs = []
    for g in range(N_EXPERT_GROUPS):
        cg = choice[g * GROUP_SIZE:(g + 1) * GROUP_SIZE, :]
        m1, am = _first_max(cg, iota_g, GROUP_SIZE)
        m2 = jnp.max(jnp.where(iota_g == am, -jnp.inf, cg), axis=0, keepdims=True)
        gs.append(m1 + m2)
    gscore = jnp.concatenate(gs, axis=0)

    iota_n = lax.broadcasted_iota(jnp.int32, (N_EXPERT_GROUPS, tr), 0)
    t = gscore
    for _ in range(TOPK_GROUPS - 1):
        _, am = _first_max(t, iota_n, N_EXPERT_GROUPS)
        t = jnp.where(iota_n == am, -jnp.inf, t)
    kth = jnp.max(t, axis=0, keepdims=True)
    keep = gscore >= kth

    v = jnp.concatenate(
        [jnp.where(keep[g:g + 1, :], choice[g * GROUP_SIZE:(g + 1) * GROUP_SIZE, :], -jnp.inf)
         for g in range(N_EXPERT_GROUPS)], axis=0)
    iota_e = lax.broadcasted_iota(jnp.int32, (N_EXPERTS, tr), 0)
    idxs, sels = [], []
    chosen = jnp.zeros((N_EXPERTS, tr), F32)
    for _ in range(TOP_K):
        _, am = _first_max(v, iota_e, N_EXPERTS)
        hit = iota_e == am
        idxs.append(am)
        sels.append(jnp.sum(jnp.where(hit, scores, 0.0), axis=0, keepdims=True))
        chosen = jnp.where(hit, 1.0, chosen)
        v = jnp.where(hit, -jnp.inf, v)
    sel = jnp.concatenate(sels, axis=0)
    idx_ref[...] = jnp.concatenate(idxs, axis=0)
    gate_ref[...] = sel / jnp.sum(sel, axis=0, keepdims=True) * ROUTED_SCALE

    earlier = _dot(chosen.astype(BF16), tri[...]) + carry[...]
    rank_ref[...] = jnp.concatenate(
        [jnp.sum(jnp.where(iota_e == am, earlier, 0.0), axis=0, keepdims=True) for am in idxs],
        axis=0).astype(jnp.int32)
    carry[...] = carry[...] + jnp.sum(chosen, axis=1, keepdims=True)
    cnt_ref[...] = jnp.broadcast_to(carry[...], cnt_ref.shape)


def _route(logits_t, b_router):
    N = logits_t.shape[1]
    tr = TM_ROUTE
    kt = pl.BlockSpec((TOP_K, tr), lambda i: (0, i))
    return pl.pallas_call(
        _route_kernel,
        out_shape=(
            jax.ShapeDtypeStruct((TOP_K, N), jnp.int32),
            jax.ShapeDtypeStruct((TOP_K, N), F32),
            jax.ShapeDtypeStruct((TOP_K, N), jnp.int32),
            jax.ShapeDtypeStruct((N_EXPERTS, LANES), F32),
        ),
        grid=(N // tr,),
        in_specs=[
            pl.BlockSpec((N_EXPERTS, tr), lambda i: (0, i)),
            pl.BlockSpec((N_EXPERTS, 1), lambda i: (0, 0)),
        ],
        out_specs=(kt, kt, kt, pl.BlockSpec((N_EXPERTS, LANES), lambda i: (0, 0))),
        scratch_shapes=[pltpu.VMEM((N_EXPERTS, 1), F32), pltpu.VMEM((tr, tr), BF16)],
        compiler_params=_params(("arbitrary",)),
        name="route",
    )(logits_t, b_router.reshape(N_EXPERTS, 1))


def _dest_kernel(idx_ref, rank_ref, ps_ref, dest_ref):
    tr = idx_ref.shape[1]
    iota_e = lax.broadcasted_iota(jnp.int32, (N_EXPERTS, tr), 0)
    idx = idx_ref[...]
    start = jnp.concatenate(
        [jnp.sum(jnp.where(iota_e == idx[k:k + 1, :], ps_ref[...], 0.0), axis=0, keepdims=True)
         for k in range(TOP_K)], axis=0)
    dest_ref[...] = (start.astype(jnp.int32) + rank_ref[...]) * ROW_SPLIT


def _dest(idx_t, rank_t, pstart):
    N = idx_t.shape[1]
    tr = TM_ROUTE
    kt = pl.BlockSpec((TOP_K, tr), lambda i: (0, i))
    return pl.pallas_call(
        _dest_kernel,
        out_shape=jax.ShapeDtypeStruct((TOP_K, N), jnp.int32),
        grid=(N // tr,),
        in_specs=[kt, kt, pl.BlockSpec((N_EXPERTS, 1), lambda i: (0, 0))],
        out_specs=kt,
        compiler_params=_params(("arbitrary",)),
        name="dest",
    )(idx_t, rank_t, pstart.astype(F32).reshape(N_EXPERTS, 1))


def _row_copy(src, dst, sem):
    return pltpu.make_async_copy(src, dst, sem)


def _token_rows(start):
    return pl.ds(pl.multiple_of(start, ROW_SPLIT), ROW_SPLIT)


def _block_rows(block):
    return pl.ds(pl.multiple_of(block * BLOCK_SUBROWS, BLOCK_SUBROWS), BLOCK_SUBROWS)


def _disp_kernel(zb_ref, dest_ref, x_ref, xs_ref, zbuf, sem):
    tm = x_ref.shape[0] // ROW_SPLIT

    @pl.when(pl.program_id(0) == 0)
    def _():
        zbuf[...] = jnp.zeros_like(zbuf)

        def zero(i, started):
            blk = zb_ref[i]

            @pl.when(blk >= 0)
            def _():
                _row_copy(zbuf, xs_ref.at[_block_rows(blk)], sem).start()

            return started + jnp.where(blk >= 0, 1, 0)

        def done(i, carry):
            _row_copy(zbuf, xs_ref.at[_block_rows(0)], sem).wait()
            return carry

        started = lax.fori_loop(0, zb_ref.shape[0], zero, 0)
        lax.fori_loop(0, started, done, 0)

    def issue(t, carry):
        src = x_ref.at[_token_rows(t * ROW_SPLIT)]
        for k in range(TOP_K):
            _row_copy(src, xs_ref.at[_token_rows(dest_ref[k, t])], sem).start(priority=k % 2)
        return carry

    lax.fori_loop(0, tm, issue, 0)
    for k in range(TOP_K):
        _row_copy(x_ref, xs_ref.at[pl.ds(0, tm * ROW_SPLIT)], sem).wait()


def _dispatch(zero_blocks, dest_t, n2p, n_rows):
    N = n2p.shape[0] // ROW_SPLIT
    tm = TM_MOVE
    grid_spec = pltpu.PrefetchScalarGridSpec(
        num_scalar_prefetch=1,
        grid=(N // tm,),
        in_specs=[
            pl.BlockSpec((TOP_K, tm), lambda i, zb: (0, i), memory_space=pltpu.SMEM),
            pl.BlockSpec((tm * ROW_SPLIT, LANES), lambda i, zb: (i, 0)),
        ],
        out_specs=pl.BlockSpec(memory_space=pl.ANY),
        scratch_shapes=[pltpu.VMEM((BLOCK_SUBROWS, LANES), jnp.uint32), pltpu.SemaphoreType.DMA],
    )
    return pl.pallas_call(
        _disp_kernel,
        out_shape=jax.ShapeDtypeStruct((n_rows * ROW_SPLIT, LANES), jnp.uint32),
        grid_spec=grid_spec,
        compiler_params=_params(("arbitrary",)),
        name="disp",
    )(zero_blocks, dest_t, n2p)


def _experts_kernel(eb_ref, bv_ref, xs_ref, wg_ref, wu_ref, wd_ref, ys_ref,
                    xbuf, ybuf, wgb, wub, wdb, sem_in, sem_out):
    e = pl.program_id(0)
    n_used = eb_ref[N_EXPERTS]
    n_blocks = ys_ref.shape[0] // BLOCK_SUBROWS
    first = eb_ref[e]
    end = eb_ref[e + 1]

    def in_copy(g, slot):
        return _row_copy(xs_ref.at[_block_rows(g)], xbuf.at[slot], sem_in.at[slot])

    def out_copy(g, slot):
        return _row_copy(ybuf.at[slot], ys_ref.at[_block_rows(g)], sem_out.at[slot])

    @pl.when(jnp.logical_and(e == 0, n_used > 0))
    def _():
        in_copy(0, 0).start()

    @pl.when(end > first)
    def _():
        wgb[...] = wg_ref[...].astype(BF16)
        wub[...] = wu_ref[...].astype(BF16)
        wdb[...] = wd_ref[...].astype(BF16)

        def block(g, carry):
            slot = g & 1
            in_copy(g, slot).wait()

            @pl.when(g + 1 < n_used)
            def _():
                in_copy(g + 1, 1 - slot).start()

            @pl.when(g >= 2)
            def _():
                out_copy(g - 2, slot).wait()

            lo, hi = _unpack_halves(_load_rows(xbuf.at[slot], ROW_BLOCK))
            live = lax.broadcasted_iota(jnp.int32, lo.shape, 0) < bv_ref[g]
            lo = jnp.where(live, lo, 0.0).astype(BF16)
            hi = jnp.where(live, hi, 0.0).astype(BF16)
            gate = _dot(lo, wgb[0:HALF, :]) + _dot(hi, wgb[HALF:, :])
            up = _dot(lo, wub[0:HALF, :]) + _dot(hi, wub[HALF:, :])
            act = (gate * jax.nn.sigmoid(gate)) * up
            _store_rows(ybuf.at[slot], _pack_halves(_dot(act.astype(BF16), wdb[...])))
            out_copy(g, slot).start()
            return carry

        lax.fori_loop(first, end, block, 0)

    @pl.when(e == N_EXPERTS - 1)
    def _():
        for back in (1, 2):
            @pl.when(n_used >= back)
            def _(back=back):
                out_copy(n_used - back, (n_used - back) & 1).wait()

        xbuf[0] = jnp.zeros((BLOCK_SUBROWS, LANES), jnp.uint32)

        def zero(g, carry):
            _row_copy(xbuf.at[0], ys_ref.at[_block_rows(g)], sem_out.at[0]).start()
            return carry

        def done(g, carry):
            _row_copy(xbuf.at[0], ys_ref.at[_block_rows(g)], sem_out.at[0]).wait()
            return carry

        lax.fori_loop(n_used, n_blocks, zero, 0)
        lax.fori_loop(n_used, n_blocks, done, 0)


def _experts(expert_block, block_valid, xs, w_gate, w_up, w_down):
    def w_map(e, eb, bv):
        return (e, 0, 0)

    grid_spec = pltpu.PrefetchScalarGridSpec(
        num_scalar_prefetch=2,
        grid=(N_EXPERTS,),
        in_specs=[
            pl.BlockSpec(memory_space=pl.ANY),
            pl.BlockSpec((None, D_MODEL, EXPERT_DIM), w_map),
            pl.BlockSpec((None, D_MODEL, EXPERT_DIM), w_map),
            pl.BlockSpec((None, EXPERT_DIM, D_MODEL), w_map),
        ],
        out_specs=pl.BlockSpec(memory_space=pl.ANY),
        scratch_shapes=[
            pltpu.VMEM((2, BLOCK_SUBROWS, LANES), jnp.uint32),
            pltpu.VMEM((2, BLOCK_SUBROWS, LANES), jnp.uint32),
            pltpu.VMEM((D_MODEL, EXPERT_DIM), BF16),
            pltpu.VMEM((D_MODEL, EXPERT_DIM), BF16),
            pltpu.VMEM((EXPERT_DIM, D_MODEL), BF16),
            pltpu.SemaphoreType.DMA((2,)),
            pltpu.SemaphoreType.DMA((2,)),
        ],
    )
    return pl.pallas_call(
        _experts_kernel,
        out_shape=jax.ShapeDtypeStruct(xs.shape, jnp.uint32),
        grid_spec=grid_spec,
        compiler_params=_params(("arbitrary",)),
        name="experts",
    )(expert_block, block_valid, xs, w_gate, w_up, w_down)


def _comb_kernel(dest_ref, next_ref, ys_ref, hs_ref, gate_ref, mod_ref, gf_ref, o_ref, buf, sem):
    i = pl.program_id(0)
    tm = hs_ref.shape[0]
    slot = i & 1

    def gather(rows_ref, into):
        def issue(t, carry):
            for k in range(TOP_K):
                _row_copy(ys_ref.at[_token_rows(rows_ref[k, t])],
                          buf.at[into, k, _token_rows(t * ROW_SPLIT)], sem.at[into]).start(priority=k % 2)
            return carry

        lax.fori_loop(0, tm, issue, 0)

    @pl.when(i == 0)
    def _():
        gather(dest_ref, 0)

    @pl.when(i + 1 < pl.num_programs(0))
    def _():
        gather(next_ref, 1 - slot)

    for k in range(TOP_K):
        _row_copy(ys_ref.at[pl.ds(0, tm * ROW_SPLIT)], buf.at[slot, k], sem.at[slot]).wait()

    gates = gate_ref[...]
    lo = jnp.zeros((tm, HALF), F32)
    hi = jnp.zeros((tm, HALF), F32)
    for k in range(TOP_K):
        lo_k, hi_k = _unpack_halves(_load_rows(buf.at[slot, k], tm))
        gk = gates[:, k:k + 1]
        lo = lo + gk * lo_k
        hi = hi + gk * hi_k
    routed = jnp.concatenate([lo, hi], axis=1)
    h2 = hs_ref[...] + mod_ref[0, 5:6, :] * routed
    o_ref[...] = _rms(h2, gf_ref[...])


def _combine(dest_t, ys, hs2, gates_nk, mod3, g_final, seq):
    N = hs2.shape[0]
    tm = TM_MOVE
    per_seq = seq // tm
    n_tiles = N // tm
    return pl.pallas_call(
        _comb_kernel,
        out_shape=jax.ShapeDtypeStruct((N, D_MODEL), F32),
        grid=(n_tiles,),
        in_specs=[
            pl.BlockSpec((TOP_K, tm), lambda i: (0, i), memory_space=pltpu.SMEM),
            pl.BlockSpec((TOP_K, tm), lambda i: (0, jnp.minimum(i + 1, n_tiles - 1)), memory_space=pltpu.SMEM),
            pl.BlockSpec(memory_space=pl.ANY),
            pl.BlockSpec((tm, D_MODEL), lambda i: (i, 0)),
            pl.BlockSpec((tm, TOP_K), lambda i: (i, 0)),
            pl.BlockSpec((1, 6, D_MODEL), lambda i: (i // per_seq, 0, 0)),
            pl.BlockSpec((1, D_MODEL), lambda i: (0, 0)),
        ],
        out_specs=pl.BlockSpec((tm, D_MODEL), lambda i: (i, 0)),
        scratch_shapes=[pltpu.VMEM((2, TOP_K, tm * ROW_SPLIT, LANES), jnp.uint32),
                        pltpu.SemaphoreType.DMA((2,))],
        compiler_params=_params(("arbitrary",)),
        name="comb",
    )(dest_t, dest_t, ys, hs2, gates_nk, mod3, g_final.reshape(1, D_MODEL))


def _layer(x, mod3, g_norm_mix, w_in, sgu_ln_gain, sgu_ln_bias, w_spatial, b_spatial, g_out_attn,
           g_out_sgu, w_out, g_norm_ffn, w_router, b_router, w_exp_gate, w_exp_up, w_exp_down,
           w_sh_gate, w_sh_up, w_sh_down, g_final):
    B, S, _ = x.shape
    N = B * S

    q, k, v, u, z = _inproj(x, mod3, g_norm_mix, w_in.astype(BF16))
    head_of_lane = jnp.arange(ATTN_WIDTH) // HEAD_DIM
    slopes = jnp.exp2(-8.0 * jnp.arange(1, ATTN_HEADS + 1, dtype=F32) / ATTN_HEADS)
    slopes_lane = slopes[head_of_lane].reshape(ATTN_WIDTH // LANES, 1, LANES)
    attn = _attention(q, k, v, slopes_lane)

    hs, n2p, logits_t = _mix(
        attn, u, z, x, mod3, g_out_attn, g_out_sgu, sgu_ln_gain, sgu_ln_bias, w_spatial, b_spatial.T,
        w_out.astype(BF16), g_norm_ffn, w_router.T.astype(BF16),
        jnp.concatenate([w_sh_gate, w_sh_up], axis=1).astype(BF16), w_sh_down.astype(BF16))

    idx_t, gates_t, rank_t, counts = _route(logits_t, b_router)

    n_blocks = (N * TOP_K + N_EXPERTS * (ROW_BLOCK - 1)) // ROW_BLOCK
    cnt = counts[:, 0].astype(jnp.int32)
    padded = (cnt + ROW_BLOCK - 1) // ROW_BLOCK * ROW_BLOCK
    pends = jnp.cumsum(padded)
    pstart = pends - padded
    block_row = jnp.arange(n_blocks, dtype=jnp.int32) * ROW_BLOCK
    block_e = jnp.sum(pends[None, :] <= block_row[:, None], axis=1, dtype=jnp.int32)
    block_e = jnp.minimum(block_e, N_EXPERTS - 1)
    own = block_e[:, None] == jnp.arange(N_EXPERTS, dtype=jnp.int32)[None, :]
    live_end = jnp.sum(jnp.where(own, (pstart + cnt)[None, :], 0), axis=1)
    block_valid = jnp.clip(live_end - block_row, 0, ROW_BLOCK).astype(jnp.int32)
    n_used = (pends[-1:] // ROW_BLOCK).astype(jnp.int32)
    expert_block = jnp.concatenate([pstart // ROW_BLOCK, n_used]).astype(jnp.int32)

    last_block = jnp.where(cnt > 0, pends // ROW_BLOCK - 1, -1)
    tail_block = n_used[0] + jnp.arange(n_blocks - N * TOP_K // ROW_BLOCK)
    tail_block = jnp.where(tail_block < n_blocks, tail_block, -1)
    zero_blocks = jnp.concatenate([last_block, tail_block]).astype(jnp.int32)

    dest_t = _dest(idx_t, rank_t, pstart)
    xs = _dispatch(zero_blocks, dest_t, n2p, n_blocks * ROW_BLOCK)
    ys = _experts(expert_block, block_valid, xs, w_exp_gate, w_exp_up, w_exp_down)
    out = _combine(dest_t, ys, hs.reshape(N, D_MODEL), gates_t.T, mod3, g_final, S)
    return out.reshape(B, S, D_MODEL)


def kernel(x, c, w_ada, b_ada, g_norm_mix, w_in, sgu_ln_gain, sgu_ln_bias, w_spatial, b_spatial, g_out_attn, g_out_sgu, w_out, g_norm_ffn, w_router, b_router, w_exp_gate, w_exp_up, w_exp_down, w_sh_gate, w_sh_up, w_sh_down, g_final):
    assert w_ada.shape[0] == 1, "single-layer stack"
    B = x.shape[0]
    mod3 = _ada(c, w_ada[0], b_ada[0]).reshape(B, 6, D_MODEL)
    return _layer(x, mod3, g_norm_mix[0], w_in[0], sgu_ln_gain[0], sgu_ln_bias[0], w_spatial[0],
                  b_spatial[0], g_out_attn[0], g_out_sgu[0], w_out[0], g_norm_ffn[0], w_router[0],
                  b_router[0], w_exp_gate[0], w_exp_up[0], w_exp_down[0], w_sh_gate[0], w_sh_up[0],
                  w_sh_down[0], g_final)
```

```python
import functools

import jax
import jax.numpy as jnp
from jax import lax
from jax.experimental import pallas as pl
from jax.experimental.pallas import tpu as pltpu

F32 = jnp.float32
BF16 = jnp.bfloat16

D_MODEL = 1024
ATTN_WIDTH = 512
ATTN_HEADS = 8
HEAD_DIM = 64
SGU_WIDTH = 512
SGU_GROUPS = 4
SGU_GROUP_DIM = 128
CHUNK = 128
DILATED_BRANCHES = ((128, 1), (512, 4), (2048, 16))
ATTN_BLOCK = 128
N_EXPERTS = 256
TOP_K = 8
N_EXPERT_GROUPS = 8
GROUP_SIZE = N_EXPERTS // N_EXPERT_GROUPS
TOPK_GROUPS = 4
EXPERT_DIM = 256
ROUTED_SCALE = 2.5
EPS = 1e-6

LANES = 128
HALF = D_MODEL // 2
ROW_BLOCK = 256
NEG_BIG = -1e30
VMEM_LIMIT = 56 * 1024 * 1024

TM_PROJ = 512
TM_ROUTE = 512
TM_MOVE = 256
ATTN_UNROLL = 8
RING = 4


def _dot(a, b):
    return jnp.dot(a, b, preferred_element_type=F32)


def _dot_nt(a, b):
    return lax.dot_general(a, b, (((1,), (1,)), ((), ())), preferred_element_type=F32)


def _rms(x, g):
    return x * lax.rsqrt(jnp.mean(x * x, axis=-1, keepdims=True) + EPS) * g


def _pack_halves(x):
    return pltpu.pack_elementwise([x[:, :HALF], x[:, HALF:]], packed_dtype=BF16)


def _unpack_halves(w):
    lo = pltpu.unpack_elementwise(w, index=0, packed_dtype=BF16, unpacked_dtype=F32)
    hi = pltpu.unpack_elementwise(w, index=1, packed_dtype=BF16, unpacked_dtype=F32)
    return lo, hi


ROW_SPLIT = HALF // LANES
BLOCK_SUBROWS = ROW_BLOCK * ROW_SPLIT


def _store_rows(ref, packed):
    rows = packed.shape[0]
    for c in range(ROW_SPLIT):
        ref[pl.ds(c, rows, stride=ROW_SPLIT), :] = packed[:, c * LANES:(c + 1) * LANES]


def _load_rows(ref, rows):
    return jnp.concatenate([ref[pl.ds(c, rows, stride=ROW_SPLIT), :] for c in range(ROW_SPLIT)], axis=1)


def _params(sem=None):
    return pltpu.CompilerParams(dimension_semantics=sem, vmem_limit_bytes=VMEM_LIMIT)


def _ada_kernel(c_ref, w_ref, b_ref, o_ref):
    c = c_ref[...]
    cond = c * jax.nn.sigmoid(c)
    ch = cond.astype(BF16)
    cl = (cond - ch.astype(F32)).astype(BF16)
    w = w_ref[...]
    wh = w.astype(BF16)
    wl = (w - wh.astype(F32)).astype(BF16)
    o_ref[...] = _dot(ch, wh) + _dot(cl, wh) + _dot(ch, wl) + b_ref[...]


def _ada(c, w_ada, b_ada):
    B = c.shape[0]
    n_out = w_ada.shape[1]
    tn = D_MODEL
    return pl.pallas_call(
        _ada_kernel,
        out_shape=jax.ShapeDtypeStruct((B, n_out), F32),
        grid=(n_out // tn,),
        in_specs=[
            pl.BlockSpec((B, D_MODEL), lambda j: (0, 0)),
            pl.BlockSpec((D_MODEL, tn), lambda j: (0, j)),
            pl.BlockSpec((1, tn), lambda j: (0, j)),
        ],
        out_specs=pl.BlockSpec((B, tn), lambda j: (0, j)),
        compiler_params=_params(("arbitrary",)),
        name="ada",
    )(c, w_ada, b_ada.reshape(1, n_out))


def _inproj_kernel(x_ref, mod_ref, g_ref, w_ref, q_ref, k_ref, v_ref, u_ref, z_ref):
    x = x_ref[0]
    shift = mod_ref[0, 0:1, :]
    scale = mod_ref[0, 1:2, :]
    n = _rms(x, g_ref[...]) * (1.0 + scale) + shift
    p = _dot(n.astype(BF16), w_ref[...])
    q_ref[0] = (p[:, 0:ATTN_WIDTH] * (HEAD_DIM ** -0.5)).astype(BF16)
    k_ref[0] = p[:, ATTN_WIDTH:2 * ATTN_WIDTH].astype(BF16)
    v_ref[0] = p[:, 2 * ATTN_WIDTH:3 * ATTN_WIDTH].astype(BF16)
    u_ref[0] = p[:, 3 * ATTN_WIDTH:3 * ATTN_WIDTH + SGU_WIDTH].astype(BF16)
    z_ref[0] = p[:, 3 * ATTN_WIDTH + SGU_WIDTH:].astype(BF16)


def _inproj(x, mod3, g_norm, w_in_bf):
    B, S, _ = x.shape
    tm = TM_PROJ
    n_in = w_in_bf.shape[1]
    out = jax.ShapeDtypeStruct((B, S, ATTN_WIDTH), BF16)
    tile = pl.BlockSpec((1, tm, ATTN_WIDTH), lambda b, i: (b, i, 0))
    return pl.pallas_call(
        _inproj_kernel,
        out_shape=(out,) * 5,
        grid=(B, S // tm),
        in_specs=[
            pl.BlockSpec((1, tm, D_MODEL), lambda b, i: (b, i, 0)),
            pl.BlockSpec((1, 6, D_MODEL), lambda b, i: (b, 0, 0)),
            pl.BlockSpec((1, D_MODEL), lambda b, i: (0, 0)),
            pl.BlockSpec((D_MODEL, n_in), lambda b, i: (0, 0)),
        ],
        out_specs=(tile,) * 5,
        compiler_params=_params(("arbitrary", "arbitrary")),
        name="inproj",
    )(x, mod3, g_norm.reshape(1, D_MODEL), w_in_bf)


def _attn_kernel(slope_ref, q_ref, k_ref, v_ref, o_ref, qf, kf, vf, oacc, lacc, tbl, *, seq):
    qf[...] = q_ref[0].astype(F32)
    kf[...] = k_ref[0].astype(F32)
    vf[...] = v_ref[0].astype(F32)

    lane = lax.broadcasted_iota(jnp.int32, (1, LANES), 1)
    head0 = lane < HEAD_DIM
    slopes = slope_ref[0]
    slope_h = (slopes[:, 0:1], slopes[:, HEAD_DIM:HEAD_DIM + 1])

    qr = lax.broadcasted_iota(jnp.int32, (ATTN_BLOCK, 2 * ATTN_BLOCK), 0)
    kc = lax.broadcasted_iota(jnp.int32, (ATTN_BLOCK, 2 * ATTN_BLOCK), 1)
    for bi, (window, dil) in enumerate(DILATED_BRANCHES):
        steps = window // dil
        for var in range(2):
            back = qr - kc + var * ATTN_BLOCK
            valid = (back >= 0) & (back <= steps)
            dist = (back * dil).astype(F32)
            for hh in range(2):
                tbl[bi, var, hh * ATTN_BLOCK:(hh + 1) * ATTN_BLOCK, :] = jnp.where(
                    valid, -slope_h[hh] * dist, NEG_BIG)

    for bi, (window, dil) in enumerate(DILATED_BRANCHES):
        cls_len = seq // dil
        nb = cls_len // ATTN_BLOCK
        nb_shift = nb.bit_length() - 1

        def rows(start, size, dil=dil):
            if dil == 1:
                return pl.ds(pl.multiple_of(start, ATTN_BLOCK), size)
            return pl.ds(start, size, stride=dil)

        def block(it, bi=bi, dil=dil, nb=nb, nb_shift=nb_shift, rows=rows):
            r = lax.shift_right_logical(it, nb_shift)
            i = it & (nb - 1)
            var = jnp.minimum(i, 1)
            q_rows = rows(i * ATTN_BLOCK * dil + r, ATTN_BLOCK)
            k_rows = rows((i - var) * ATTN_BLOCK * dil + r, 2 * ATTN_BLOCK)
            q2 = qf[q_rows, :]
            kb = kf[k_rows, :].astype(BF16)
            v2 = vf[k_rows, :]
            qs = jnp.concatenate([jnp.where(head0, q2, 0.0), jnp.where(head0, 0.0, q2)], axis=0)
            s = _dot_nt(qs.astype(BF16), kb) + tbl[bi, var]
            m = jnp.max(s, axis=-1, keepdims=True)
            p = jnp.exp(s - m)
            den = jnp.sum(p, axis=-1, keepdims=True)
            pb = p.astype(BF16)
            vs = jnp.concatenate([jnp.where(head0, v2, 0.0), jnp.where(head0, 0.0, v2)], axis=0)
            o = _dot(jnp.concatenate([pb[:ATTN_BLOCK], pb[ATTN_BLOCK:]], axis=1), vs.astype(BF16))
            den2 = jnp.where(head0, den[:ATTN_BLOCK], den[ATTN_BLOCK:])
            lse = jnp.where(head0, m[:ATTN_BLOCK], m[ATTN_BLOCK:]) + jnp.log(den2)
            return q_rows, o / den2, lse

        def body(step, carry, bi=bi, block=block):
            done = [block(step * ATTN_UNROLL + j) for j in range(ATTN_UNROLL)]
            if bi > 0:
                merged = []
                for q_rows, o, lse in done:
                    l_old = lacc[q_rows, :]
                    m2 = jnp.maximum(l_old, lse)
                    a = jnp.exp(l_old - m2)
                    b = jnp.exp(lse - m2)
                    tot = a + b
                    merged.append((q_rows, (oacc[q_rows, :] * a + o * b) / tot, m2 + jnp.log(tot)))
                done = merged
            for q_rows, o, lse in done:
                oacc[q_rows, :] = o
                lacc[q_rows, :] = lse
            return carry

        assert (dil * nb) % ATTN_UNROLL == 0
        lax.fori_loop(0, dil * nb // ATTN_UNROLL, body, 0)

    o_ref[0] = oacc[...].astype(BF16)


def _attention(q, k, v, slopes_lane):
    B, S, _ = q.shape
    for window, dil in DILATED_BRANCHES:
        assert window // dil <= ATTN_BLOCK
        cls_len = S // dil
        assert S % dil == 0 and cls_len % ATTN_BLOCK == 0 and cls_len >= 2 * ATTN_BLOCK
        assert (cls_len // ATTN_BLOCK) & (cls_len // ATTN_BLOCK - 1) == 0
    n_pairs = ATTN_WIDTH // LANES
    tile = pl.BlockSpec((1, S, LANES), lambda b, p: (b, 0, p))
    return pl.pallas_call(
        functools.partial(_attn_kernel, seq=S),
        out_shape=jax.ShapeDtypeStruct((B, S, ATTN_WIDTH), BF16),
        grid=(B, n_pairs),
        in_specs=[pl.BlockSpec((1, 1, LANES), lambda b, p: (p, 0, 0)), tile, tile, tile],
        out_specs=tile,
        scratch_shapes=[pltpu.VMEM((S, LANES), F32)] * 5
        + [pltpu.VMEM((len(DILATED_BRANCHES), 2, 2 * ATTN_BLOCK, 2 * ATTN_BLOCK), F32)],
        compiler_params=_params(("arbitrary", "arbitrary")),
        name="attn",
    )(slopes_lane, q, k, v)


def _mix_kernel(attn_ref, u_ref, z_ref, x_ref, mod_ref, ga_ref, gs_ref, lng_ref, lnb_ref, wsp_ref,
                bsp_ref, wout_ref, gffn_ref, wrt_ref, wsgu_ref, wsd_ref,
                hs_ref, n2p_ref, lg_ref):
    tm = x_ref.shape[1]
    nc = tm // CHUNK
    a_n = _rms(attn_ref[0].astype(F32), ga_ref[...])

    ug = jax.nn.gelu(u_ref[0].astype(F32))
    zg = jax.nn.gelu(z_ref[0].astype(F32))
    mu = jnp.mean(zg, axis=-1, keepdims=True)
    zc = zg - mu
    var = jnp.mean(zc * zc, axis=-1, keepdims=True)
    zb = (zc * lax.rsqrt(var + EPS) * lng_ref[...] + lnb_ref[...]).astype(BF16)

    row = lax.broadcasted_iota(jnp.int32, (CHUNK, CHUNK), 0)
    col = lax.broadcasted_iota(jnp.int32, (CHUNK, CHUNK), 1)
    per_group = []
    for g in range(SGU_GROUPS):
        wc = jnp.where(row >= col, wsp_ref[g], 0.0).astype(BF16)
        lanes = slice(g * SGU_GROUP_DIM, (g + 1) * SGU_GROUP_DIM)
        zcat = jnp.concatenate([zb[c * CHUNK:(c + 1) * CHUNK, lanes] for c in range(nc)], axis=1)
        per_group.append(_dot(wc, zcat) + bsp_ref[:, g:g + 1])
    mixed = jnp.concatenate(
        [jnp.concatenate([per_group[g][:, c * CHUNK:(c + 1) * CHUNK] for g in range(SGU_GROUPS)], axis=1)
         for c in range(nc)], axis=0)
    s_n = _rms(ug * mixed, gs_ref[...])

    mix = (_dot(a_n.astype(BF16), wout_ref[0:ATTN_WIDTH, :])
           + _dot(s_n.astype(BF16), wout_ref[ATTN_WIDTH:, :]))
    gate1 = mod_ref[0, 2:3, :]
    shift2 = mod_ref[0, 3:4, :]
    scale2 = mod_ref[0, 4:5, :]
    gate2 = mod_ref[0, 5:6, :]
    h1 = x_ref[0] + gate1 * mix
    n2 = _rms(h1, gffn_ref[...]) * (1.0 + scale2) + shift2
    n2b = n2.astype(BF16)

    lg_ref[...] = _dot_nt(wrt_ref[...], n2b)
    gu = _dot(n2b, wsgu_ref[...])
    gsh = gu[:, :EXPERT_DIM]
    act = (gsh * jax.nn.sigmoid(gsh)) * gu[:, EXPERT_DIM:]
    shared = _dot(act.astype(BF16), wsd_ref[...])
    hs_ref[0] = h1 + gate2 * shared
    _store_rows(n2p_ref, _pack_halves(n2))


def _mix(attn, u, z, x, mod3, g_out_attn, g_out_sgu, ln_g, ln_b, w_spatial, b_spatial_t, w_out_bf,
         g_norm_ffn, w_router_t_bf, w_sh_gu_bf, w_sh_d_bf):
    B, S, _ = x.shape
    tm = TM_PROJ
    nt = S // tm
    N = B * S
    half_tile = pl.BlockSpec((1, tm, ATTN_WIDTH), lambda b, i: (b, i, 0))
    full_tile = pl.BlockSpec((1, tm, D_MODEL), lambda b, i: (b, i, 0))

    def const(shape):
        return pl.BlockSpec(shape, lambda b, i: (0,) * len(shape))

    return pl.pallas_call(
        _mix_kernel,
        out_shape=(
            jax.ShapeDtypeStruct((B, S, D_MODEL), F32),
            jax.ShapeDtypeStruct((N * ROW_SPLIT, LANES), jnp.uint32),
            jax.ShapeDtypeStruct((N_EXPERTS, N), F32),
        ),
        grid=(B, nt),
        in_specs=[
            half_tile, half_tile, half_tile, full_tile,
            pl.BlockSpec((1, 6, D_MODEL), lambda b, i: (b, 0, 0)),
            const((1, ATTN_WIDTH)), const((1, SGU_WIDTH)), const((1, SGU_WIDTH)), const((1, SGU_WIDTH)),
            const((SGU_GROUPS, CHUNK, CHUNK)), const((CHUNK, SGU_GROUPS)),
            const((ATTN_WIDTH + SGU_WIDTH, D_MODEL)), const((1, D_MODEL)),
            const((N_EXPERTS, D_MODEL)), const((D_MODEL, 2 * EXPERT_DIM)), const((EXPERT_DIM, D_MODEL)),
        ],
        out_specs=(
            full_tile,
            pl.BlockSpec((tm * ROW_SPLIT, LANES), lambda b, i: (b * nt + i, 0)),
            pl.BlockSpec((N_EXPERTS, tm), lambda b, i: (0, b * nt + i)),
        ),
        compiler_params=_params(("arbitrary", "arbitrary")),
        name="mix",
    )(attn, u, z, x, mod3, g_out_attn.reshape(1, -1), g_out_sgu.reshape(1, -1), ln_g.reshape(1, -1),
      ln_b.reshape(1, -1), w_spatial, b_spatial_t, w_out_bf, g_norm_ffn.reshape(1, -1), w_router_t_bf,
      w_sh_gu_bf, w_sh_d_bf)


def _first_max(v, iota, size):
    mx = jnp.max(v, axis=0, keepdims=True)
    am = jnp.min(jnp.where(v == mx, iota, size), axis=0, keepdims=True)
    return mx, am


def _route_kernel(lg_ref, br_ref, idx_ref, gate_ref, rank_ref, cnt_ref, carry, tri):
    step = pl.program_id(0)
    tr = lg_ref.shape[1]

    @pl.when(step == 0)
    def _():
        carry[...] = jnp.zeros_like(carry)
        before = (lax.broadcasted_iota(jnp.int32, (tr, tr), 0)
                  < lax.broadcasted_iota(jnp.int32, (tr, tr), 1))
        tri[...] = jnp.where(before, 1.0, 0.0).astype(BF16)

    scores = jax.nn.sigmoid(lg_ref[...])
    choice = scores + br_ref[...]

    iota_g = lax.broadcasted_iota(jnp.int32, (GROUP_SIZE, tr), 0)
    gs = []
    for g in range(N_EXPERT_GROUPS):
        cg = choice[g * GROUP_SIZE:(g + 1) * GROUP_SIZE, :]
        m1, am = _first_max(cg, iota_g, GROUP_SIZE)
        m2 = jnp.max(jnp.where(iota_g == am, -jnp.inf, cg), axis=0, keepdims=True)
        gs.append(m1 + m2)
    gscore = jnp.concatenate(gs, axis=0)

    iota_n = lax.broadcasted_iota(jnp.int32, (N_EXPERT_GROUPS, tr), 0)
    t = gscore
    for _ in range(TOPK_GROUPS - 1):
        _, am = _first_max(t, iota_n, N_EXPERT_GROUPS)
        t = jnp.where(iota_n == am, -jnp.inf, t)
    kth = jnp.max(t, axis=0, keepdims=True)
    keep = gscore >= kth

    v = jnp.concatenate(
        [jnp.where(keep[g:g + 1, :], choice[g * GROUP_SIZE:(g + 1) * GROUP_SIZE, :], -jnp.inf)
         for g in range(N_EXPERT_GROUPS)], axis=0)
    iota_e = lax.broadcasted_iota(jnp.int32, (N_EXPERTS, tr), 0)
    idxs, sels = [], []
    chosen = jnp.zeros((N_EXPERTS, tr), F32)
    for _ in range(TOP_K):
        _, am = _first_max(v, iota_e, N_EXPERTS)
        hit = iota_e == am
        idxs.append(am)
        sels.append(jnp.sum(jnp.where(hit, scores, 0.0), axis=0, keepdims=True))
        chosen = jnp.where(hit, 1.0, chosen)
        v = jnp.where(hit, -jnp.inf, v)
    sel = jnp.concatenate(sels, axis=0)
    idx_ref[...] = jnp.concatenate(idxs, axis=0)
    gate_ref[...] = sel / jnp.sum(sel, axis=0, keepdims=True) * ROUTED_SCALE

    earlier = _dot(chosen.astype(BF16), tri[...]) + carry[...]
    rank_ref[...] = jnp.concatenate(
        [jnp.sum(jnp.where(iota_e == am, earlier, 0.0), axis=0, keepdims=True) for am in idxs],
        axis=0).astype(jnp.int32)
    carry[...] = carry[...] + jnp.sum(chosen, axis=1, keepdims=True)
    cnt_ref[...] = jnp.broadcast_to(carry[...], cnt_ref.shape)


def _route(logits_t, b_router):
    N = logits_t.shape[1]
    tr = TM_ROUTE
    kt = pl.BlockSpec((TOP_K, tr), lambda i: (0, i))
    return pl.pallas_call(
        _route_kernel,
        out_shape=(
            jax.ShapeDtypeStruct((TOP_K, N), jnp.int32),
            jax.ShapeDtypeStruct((TOP_K, N), F32),
            jax.ShapeDtypeStruct((TOP_K, N), jnp.int32),
            jax.ShapeDtypeStruct((N_EXPERTS, LANES), F32),
        ),
        grid=(N // tr,),
        in_specs=[
            pl.BlockSpec((N_EXPERTS, tr), lambda i: (0, i)),
            pl.BlockSpec((N_EXPERTS, 1), lambda i: (0, 0)),
        ],
        out_specs=(kt, kt, kt, pl.BlockSpec((N_EXPERTS, LANES), lambda i: (0, 0))),
        scratch_shapes=[pltpu.VMEM((N_EXPERTS, 1), F32), pltpu.VMEM((tr, tr), BF16)],
        compiler_params=_params(("arbitrary",)),
        name="route",
    )(logits_t, b_router.reshape(N_EXPERTS, 1))


def _dest_kernel(idx_ref, rank_ref, ps_ref, dest_ref):
    tr = idx_ref.shape[1]
    iota_e = lax.broadcasted_iota(jnp.int32, (N_EXPERTS, tr), 0)
    idx = idx_ref[...]
    start = jnp.concatenate(
        [jnp.sum(jnp.where(iota_e == idx[k:k + 1, :], ps_ref[...], 0.0), axis=0, keepdims=True)
         for k in range(TOP_K)], axis=0)
    dest_ref[...] = (start.astype(jnp.int32) + rank_ref[...]) * ROW_SPLIT


def _dest(idx_t, rank_t, pstart):
    N = idx_t.shape[1]
    tr = TM_ROUTE
    kt = pl.BlockSpec((TOP_K, tr), lambda i: (0, i))
    return pl.pallas_call(
        _dest_kernel,
        out_shape=jax.ShapeDtypeStruct((TOP_K, N), jnp.int32),
        grid=(N // tr,),
        in_specs=[kt, kt, pl.BlockSpec((N_EXPERTS, 1), lambda i: (0, 0))],
        out_specs=kt,
        compiler_params=_params(("arbitrary",)),
        name="dest",
    )(idx_t, rank_t, pstart.astype(F32).reshape(N_EXPERTS, 1))


def _row_copy(src, dst, sem):
    return pltpu.make_async_copy(src, dst, sem)


def _token_rows(start):
    return pl.ds(pl.multiple_of(start, ROW_SPLIT), ROW_SPLIT)


def _block_rows(block):
    return pl.ds(pl.multiple_of(block * BLOCK_SUBROWS, BLOCK_SUBROWS), BLOCK_SUBROWS)


def _disp_kernel(zb_ref, dest_ref, x_ref, xs_ref, zbuf, sem):
    tm = x_ref.shape[0] // ROW_SPLIT

    @pl.when(pl.program_id(0) == 0)
    def _():
        zbuf[...] = jnp.zeros_like(zbuf)

        def zero(i, started):
            blk = zb_ref[i]

            @pl.when(blk >= 0)
            def _():
                _row_copy(zbuf, xs_ref.at[_block_rows(blk)], sem).start()

            return started + jnp.where(blk >= 0, 1, 0)

        def done(i, carry):
            _row_copy(zbuf, xs_ref.at[_block_rows(0)], sem).wait()
            return carry

        started = lax.fori_loop(0, zb_ref.shape[0], zero, 0)
        lax.fori_loop(0, started, done, 0)

    def issue(t, carry):
        src = x_ref.at[_token_rows(t * ROW_SPLIT)]
        for k in range(TOP_K):
            _row_copy(src, xs_ref.at[_token_rows(dest_ref[k, t])], sem).start(priority=k % 2)
        return carry

    lax.fori_loop(0, tm, issue, 0)
    for k in range(TOP_K):
        _row_copy(x_ref, xs_ref.at[pl.ds(0, tm * ROW_SPLIT)], sem).wait()


def _dispatch(zero_blocks, dest_t, n2p, n_rows):
    N = n2p.shape[0] // ROW_SPLIT
    tm = TM_MOVE
    grid_spec = pltpu.PrefetchScalarGridSpec(
        num_scalar_prefetch=1,
        grid=(N // tm,),
        in_specs=[
            pl.BlockSpec((TOP_K, tm), lambda i, zb: (0, i), memory_space=pltpu.SMEM),
            pl.BlockSpec((tm * ROW_SPLIT, LANES), lambda i, zb: (i, 0)),
        ],
        out_specs=pl.BlockSpec(memory_space=pl.ANY),
        scratch_shapes=[pltpu.VMEM((BLOCK_SUBROWS, LANES), jnp.uint32), pltpu.SemaphoreType.DMA],
    )
    return pl.pallas_call(
        _disp_kernel,
        out_shape=jax.ShapeDtypeStruct((n_rows * ROW_SPLIT, LANES), jnp.uint32),
        grid_spec=grid_spec,
        compiler_params=_params(("arbitrary",)),
        name="disp",
    )(zero_blocks, dest_t, n2p)


def _experts_kernel(eb_ref, bv_ref, xs_ref, wg_ref, wu_ref, wd_ref, ys_ref,
                    xbuf, ybuf, wgb, wub, wdb, sem_in, sem_out):
    e = pl.program_id(0)
    n_used = eb_ref[N_EXPERTS]
    n_blocks = ys_ref.shape[0] // BLOCK_SUBROWS
    first = eb_ref[e]
    end = eb_ref[e + 1]

    def in_copy(g, slot):
        return _row_copy(xs_ref.at[_block_rows(g)], xbuf.at[slot], sem_in.at[slot])

    def out_copy(g, slot):
        return _row_copy(ybuf.at[slot], ys_ref.at[_block_rows(g)], sem_out.at[slot])

    @pl.when(e == 0)
    def _():
        for g in range(RING - 1):
            @pl.when(g < n_used)
            def _(g=g):
                in_copy(g, g).start()

    @pl.when(end > first)
    def _():
        wgb[...] = wg_ref[...].astype(BF16)
        wub[...] = wu_ref[...].astype(BF16)
        wdb[...] = wd_ref[...].astype(BF16)

        def block(g, carry):
            slot = g & (RING - 1)
            in_copy(g, slot).wait()

            @pl.when(g + RING - 1 < n_used)
            def _():
                in_copy(g + RING - 1, (g + RING - 1) & (RING - 1)).start()

            @pl.when(g >= RING)
            def _():
                out_copy(g - RING, slot).wait()

            lo, hi = _unpack_halves(_load_rows(xbuf.at[slot], ROW_BLOCK))
            live = lax.broadcasted_iota(jnp.int32, lo.shape, 0) < bv_ref[g]
            lo = jnp.where(live, lo, 0.0).astype(BF16)
            hi = jnp.where(live, hi, 0.0).astype(BF16)
            gate = _dot(lo, wgb[0:HALF, :]) + _dot(hi, wgb[HALF:, :])
            up = _dot(lo, wub[0:HALF, :]) + _dot(hi, wub[HALF:, :])
            act = (gate * jax.nn.sigmoid(gate)) * up
            _store_rows(ybuf.at[slot], _pack_halves(_dot(act.astype(BF16), wdb[...])))
            out_copy(g, slot).start()
            return carry

        lax.fori_loop(first, end, block, 0)

    @pl.when(e == N_EXPERTS - 1)
    def _():
        for back in range(1, RING + 1):
            @pl.when(n_used >= back)
            def _(back=back):
                out_copy(n_used - back, (n_used - back) & (RING - 1)).wait()

        xbuf[0] = jnp.zeros((BLOCK_SUBROWS, LANES), jnp.uint32)

        def zero(g, carry):
            _row_copy(xbuf.at[0], ys_ref.at[_block_rows(g)], sem_out.at[0]).start()
            return carry

        def done(g, carry):
            _row_copy(xbuf.at[0], ys_ref.at[_block_rows(g)], sem_out.at[0]).wait()
            return carry

        lax.fori_loop(n_used, n_blocks, zero, 0)
        lax.fori_loop(n_used, n_blocks, done, 0)


def _experts(expert_block, block_valid, xs, w_gate, w_up, w_down):
    def w_map(e, eb, bv):
        return (e, 0, 0)

    grid_spec = pltpu.PrefetchScalarGridSpec(
        num_scalar_prefetch=2,
        grid=(N_EXPERTS,),
        in_specs=[
            pl.BlockSpec(memory_space=pl.ANY),
            pl.BlockSpec((None, D_MODEL, EXPERT_DIM), w_map),
            pl.BlockSpec((None, D_MODEL, EXPERT_DIM), w_map),
            pl.BlockSpec((None, EXPERT_DIM, D_MODEL), w_map),
        ],
        out_specs=pl.BlockSpec(memory_space=pl.ANY),
        scratch_shapes=[
            pltpu.VMEM((RING, BLOCK_SUBROWS, LANES), jnp.uint32),
            pltpu.VMEM((RING, BLOCK_SUBROWS, LANES), jnp.uint32),
            pltpu.VMEM((D_MODEL, EXPERT_DIM), BF16),
            pltpu.VMEM((D_MODEL, EXPERT_DIM), BF16),
            pltpu.VMEM((EXPERT_DIM, D_MODEL), BF16),
            pltpu.SemaphoreType.DMA((RING,)),
            pltpu.SemaphoreType.DMA((RING,)),
        ],
    )
    return pl.pallas_call(
        _experts_kernel,
        out_shape=jax.ShapeDtypeStruct(xs.shape, jnp.uint32),
        grid_spec=grid_spec,
        compiler_params=_params(("arbitrary",)),
        name="experts",
    )(expert_block, block_valid, xs, w_gate, w_up, w_down)


def _comb_kernel(dest_ref, next_ref, ys_ref, hs_ref, gate_ref, mod_ref, gf_ref, o_ref, buf, sem):
    i = pl.program_id(0)
    tm = hs_ref.shape[0]
    slot = i & 1

    def gather(rows_ref, into):
        def issue(t, carry):
            for k in range(TOP_K):
                _row_copy(ys_ref.at[_token_rows(rows_ref[k, t])],
                          buf.at[into, k, _token_rows(t * ROW_SPLIT)], sem.at[into]).start(priority=k % 2)
            return carry

        lax.fori_loop(0, tm, issue, 0)

    @pl.when(i == 0)
    def _():
        gather(dest_ref, 0)

    @pl.when(i + 1 < pl.num_programs(0))
    def _():
        gather(next_ref, 1 - slot)

    for k in range(TOP_K):
        _row_copy(ys_ref.at[pl.ds(0, tm * ROW_SPLIT)], buf.at[slot, k], sem.at[slot]).wait()

    gates = gate_ref[...]
    lo = jnp.zeros((tm, HALF), F32)
    hi = jnp.zeros((tm, HALF), F32)
    for k in range(TOP_K):
        lo_k, hi_k = _unpack_halves(_load_rows(buf.at[slot, k], tm))
        gk = gates[:, k:k + 1]
        lo = lo + gk * lo_k
        hi = hi + gk * hi_k
    routed = jnp.concatenate([lo, hi], axis=1)
    h2 = hs_ref[...] + mod_ref[0, 5:6, :] * routed
    o_ref[...] = _rms(h2, gf_ref[...])


def _combine(dest_t, ys, hs2, gates_nk, mod3, g_final, seq):
    N = hs2.shape[0]
    tm = TM_MOVE
    per_seq = seq // tm
    n_tiles = N // tm
    return pl.pallas_call(
        _comb_kernel,
        out_shape=jax.ShapeDtypeStruct((N, D_MODEL), F32),
        grid=(n_tiles,),
        in_specs=[
            pl.BlockSpec((TOP_K, tm), lambda i: (0, i), memory_space=pltpu.SMEM),
            pl.BlockSpec((TOP_K, tm), lambda i: (0, jnp.minimum(i + 1, n_tiles - 1)), memory_space=pltpu.SMEM),
            pl.BlockSpec(memory_space=pl.ANY),
            pl.BlockSpec((tm, D_MODEL), lambda i: (i, 0)),
            pl.BlockSpec((tm, TOP_K), lambda i: (i, 0)),
            pl.BlockSpec((1, 6, D_MODEL), lambda i: (i // per_seq, 0, 0)),
            pl.BlockSpec((1, D_MODEL), lambda i: (0, 0)),
        ],
        out_specs=pl.BlockSpec((tm, D_MODEL), lambda i: (i, 0)),
        scratch_shapes=[pltpu.VMEM((2, TOP_K, tm * ROW_SPLIT, LANES), jnp.uint32),
                        pltpu.SemaphoreType.DMA((2,))],
        compiler_params=_params(("arbitrary",)),
        name="comb",
    )(dest_t, dest_t, ys, hs2, gates_nk, mod3, g_final.reshape(1, D_MODEL))


def _layer(x, mod3, g_norm_mix, w_in, sgu_ln_gain, sgu_ln_bias, w_spatial, b_spatial, g_out_attn,
           g_out_sgu, w_out, g_norm_ffn, w_router, b_router, w_exp_gate, w_exp_up, w_exp_down,
           w_sh_gate, w_sh_up, w_sh_down, g_final):
    B, S, _ = x.shape
    N = B * S

    q, k, v, u, z = _inproj(x, mod3, g_norm_mix, w_in.astype(BF16))
    head_of_lane = jnp.arange(ATTN_WIDTH) // HEAD_DIM
    slopes = jnp.exp2(-8.0 * jnp.arange(1, ATTN_HEADS + 1, dtype=F32) / ATTN_HEADS)
    slopes_lane = slopes[head_of_lane].reshape(ATTN_WIDTH // LANES, 1, LANES)
    attn = _attention(q, k, v, slopes_lane)

    hs, n2p, logits_t = _mix(
        attn, u, z, x, mod3, g_out_attn, g_out_sgu, sgu_ln_gain, sgu_ln_bias, w_spatial, b_spatial.T,
        w_out.astype(BF16), g_norm_ffn, w_router.T.astype(BF16),
        jnp.concatenate([w_sh_gate, w_sh_up], axis=1).astype(BF16), w_sh_down.astype(BF16))

    idx_t, gates_t, rank_t, counts = _route(logits_t, b_router)

    n_blocks = (N * TOP_K + N_EXPERTS * (ROW_BLOCK - 1)) // ROW_BLOCK
    cnt = counts[:, 0].astype(jnp.int32)
    padded = (cnt + ROW_BLOCK - 1) // ROW_BLOCK * ROW_BLOCK
    pends = jnp.cumsum(padded)
    pstart = pends - padded
    block_row = jnp.arange(n_blocks, dtype=jnp.int32) * ROW_BLOCK
    block_e = jnp.sum(pends[None, :] <= block_row[:, None], axis=1, dtype=jnp.int32)
    block_e = jnp.minimum(block_e, N_EXPERTS - 1)
    own = block_e[:, None] == jnp.arange(N_EXPERTS, dtype=jnp.int32)[None, :]
    live_end = jnp.sum(jnp.where(own, (pstart + cnt)[None, :], 0), axis=1)
    block_valid = jnp.clip(live_end - block_row, 0, ROW_BLOCK).astype(jnp.int32)
    n_used = (pends[-1:] // ROW_BLOCK).astype(jnp.int32)
    expert_block = jnp.concatenate([pstart // ROW_BLOCK, n_used]).astype(jnp.int32)

    last_block = jnp.where(cnt > 0, pends // ROW_BLOCK - 1, -1)
    tail_block = n_used[0] + jnp.arange(n_blocks - N * TOP_K // ROW_BLOCK)
    tail_block = jnp.where(tail_block < n_blocks, tail_block, -1)
    zero_blocks = jnp.concatenate([last_block, tail_block]).astype(jnp.int32)

    dest_t = _dest(idx_t, rank_t, pstart)
    xs = _dispatch(zero_blocks, dest_t, n2p, n_blocks * ROW_BLOCK)
    ys = _experts(expert_block, block_valid, xs, w_exp_gate, w_exp_up, w_exp_down)
    out = _combine(dest_t, ys, hs.reshape(N, D_MODEL), gates_t.T, mod3, g_final, S)
    return out.reshape(B, S, D_MODEL)


def kernel(x, c, w_ada, b_ada, g_norm_mix, w_in, sgu_ln_gain, sgu_ln_bias, w_spatial, b_spatial, g_out_attn, g_out_sgu, w_out, g_norm_ffn, w_router, b_router, w_exp_gate, w_exp_up, w_exp_down, w_sh_gate, w_sh_up, w_sh_down, g_final):
    assert w_ada.shape[0] == 1, "single-layer stack"
    B = x.shape[0]
    mod3 = _ada(c, w_ada[0], b_ada[0]).reshape(B, 6, D_MODEL)
    return _layer(x, mod3, g_norm_mix[0], w_in[0], sgu_ln_gain[0], sgu_ln_bias[0], w_spatial[0],
                  b_spatial[0], g_out_attn[0], g_out_sgu[0], w_out[0], g_norm_ffn[0], w_router[0],
                  b_router[0], w_exp_gate[0], w_exp_up[0], w_exp_down[0], w_sh_gate[0], w_sh_up[0],
                  w_sh_down[0], g_final)
```

```python
import functools

import jax
import jax.numpy as jnp
from jax import lax
from jax.experimental import pallas as pl
from jax.experimental.pallas import tpu as pltpu

F32 = jnp.float32
BF16 = jnp.bfloat16

D_MODEL = 1024
ATTN_WIDTH = 512
ATTN_HEADS = 8
HEAD_DIM = 64
SGU_WIDTH = 512
SGU_GROUPS = 4
SGU_GROUP_DIM = 128
CHUNK = 128
DILATED_BRANCHES = ((128, 1), (512, 4), (2048, 16))
ATTN_BLOCK = 128
N_EXPERTS = 256
TOP_K = 8
N_EXPERT_GROUPS = 8
GROUP_SIZE = N_EXPERTS // N_EXPERT_GROUPS
TOPK_GROUPS = 4
EXPERT_DIM = 256
ROUTED_SCALE = 2.5
EPS = 1e-6

LANES = 128
HALF = D_MODEL // 2
ROW_BLOCK = 512
NEG_BIG = -1e30
VMEM_LIMIT = 56 * 1024 * 1024

TM_PROJ = 512
TM_ROUTE = 512
TM_MOVE = 512
DISP_RING = 3
COMB_CHUNK = 128
ATTN_UNROLL = 8
RING = 4


def _dot(a, b):
    return jnp.dot(a, b, preferred_element_type=F32)


def _dot_nt(a, b):
    return lax.dot_general(a, b, (((1,), (1,)), ((), ())), preferred_element_type=F32)


def _rms(x, g):
    return x * lax.rsqrt(jnp.mean(x * x, axis=-1, keepdims=True) + EPS) * g


def _pack_halves(x):
    return pltpu.pack_elementwise([x[:, :HALF], x[:, HALF:]], packed_dtype=BF16)


def _unpack_halves(w):
    lo = pltpu.unpack_elementwise(w, index=0, packed_dtype=BF16, unpacked_dtype=F32)
    hi = pltpu.unpack_elementwise(w, index=1, packed_dtype=BF16, unpacked_dtype=F32)
    return lo, hi


ROW_SPLIT = HALF // LANES
BLOCK_SUBROWS = ROW_BLOCK * ROW_SPLIT


def _store_rows(ref, packed):
    rows = packed.shape[0]
    for c in range(ROW_SPLIT):
        ref[pl.ds(c, rows, stride=ROW_SPLIT), :] = packed[:, c * LANES:(c + 1) * LANES]


def _load_rows(ref, rows):
    return jnp.concatenate([ref[pl.ds(c, rows, stride=ROW_SPLIT), :] for c in range(ROW_SPLIT)], axis=1)


def _params(sem=None):
    return pltpu.CompilerParams(dimension_semantics=sem, vmem_limit_bytes=VMEM_LIMIT)


def _ada_kernel(c_ref, w_ref, b_ref, o_ref):
    c = c_ref[...]
    cond = c * jax.nn.sigmoid(c)
    ch = cond.astype(BF16)
    cl = (cond - ch.astype(F32)).astype(BF16)
    w = w_ref[...]
    wh = w.astype(BF16)
    wl = (w - wh.astype(F32)).astype(BF16)
    o_ref[...] = _dot(ch, wh) + _dot(cl, wh) + _dot(ch, wl) + b_ref[...]


def _ada(c, w_ada, b_ada):
    B = c.shape[0]
    n_out = w_ada.shape[1]
    tn = D_MODEL
    return pl.pallas_call(
        _ada_kernel,
        out_shape=jax.ShapeDtypeStruct((B, n_out), F32),
        grid=(n_out // tn,),
        in_specs=[
            pl.BlockSpec((B, D_MODEL), lambda j: (0, 0)),
            pl.BlockSpec((D_MODEL, tn), lambda j: (0, j)),
            pl.BlockSpec((1, tn), lambda j: (0, j)),
        ],
        out_specs=pl.BlockSpec((B, tn), lambda j: (0, j)),
        compiler_params=_params(("arbitrary",)),
        name="ada",
    )(c, w_ada, b_ada.reshape(1, n_out))


def _inproj_kernel(x_ref, mod_ref, g_ref, w_ref, q_ref, k_ref, v_ref, u_ref, z_ref):
    x = x_ref[0]
    shift = mod_ref[0, 0:1, :]
    scale = mod_ref[0, 1:2, :]
    n = _rms(x, g_ref[...]) * (1.0 + scale) + shift
    p = _dot(n.astype(BF16), w_ref[...])
    q_ref[0] = (p[:, 0:ATTN_WIDTH] * (HEAD_DIM ** -0.5)).astype(BF16)
    k_ref[0] = p[:, ATTN_WIDTH:2 * ATTN_WIDTH].astype(BF16)
    v_ref[0] = p[:, 2 * ATTN_WIDTH:3 * ATTN_WIDTH].astype(BF16)
    u_ref[0] = p[:, 3 * ATTN_WIDTH:3 * ATTN_WIDTH + SGU_WIDTH].astype(BF16)
    z_ref[0] = p[:, 3 * ATTN_WIDTH + SGU_WIDTH:].astype(BF16)


def _inproj(x, mod3, g_norm, w_in_bf):
    B, S, _ = x.shape
    tm = TM_PROJ
    n_in = w_in_bf.shape[1]
    out = jax.ShapeDtypeStruct((B, S, ATTN_WIDTH), BF16)
    tile = pl.BlockSpec((1, tm, ATTN_WIDTH), lambda b, i: (b, i, 0))
    return pl.pallas_call(
        _inproj_kernel,
        out_shape=(out,) * 5,
        grid=(B, S // tm),
        in_specs=[
            pl.BlockSpec((1, tm, D_MODEL), lambda b, i: (b, i, 0)),
            pl.BlockSpec((1, 6, D_MODEL), lambda b, i: (b, 0, 0)),
            pl.BlockSpec((1, D_MODEL), lambda b, i: (0, 0)),
            pl.BlockSpec((D_MODEL, n_in), lambda b, i: (0, 0)),
        ],
        out_specs=(tile,) * 5,
        compiler_params=_params(("arbitrary", "arbitrary")),
        name="inproj",
    )(x, mod3, g_norm.reshape(1, D_MODEL), w_in_bf)


def _attn_kernel(slope_ref, q_ref, k_ref, v_ref, o_ref, qf, kf, vf, oacc, lacc, tbl, *, seq):
    qf[...] = q_ref[0].astype(F32)
    kf[...] = k_ref[0].astype(F32)
    vf[...] = v_ref[0].astype(F32)

    lane = lax.broadcasted_iota(jnp.int32, (1, LANES), 1)
    head0 = lane < HEAD_DIM
    slopes = slope_ref[0]
    slope_h = (slopes[:, 0:1], slopes[:, HEAD_DIM:HEAD_DIM + 1])

    qr = lax.broadcasted_iota(jnp.int32, (ATTN_BLOCK, 2 * ATTN_BLOCK), 0)
    kc = lax.broadcasted_iota(jnp.int32, (ATTN_BLOCK, 2 * ATTN_BLOCK), 1)
    for bi, (window, dil) in enumerate(DILATED_BRANCHES):
        steps = window // dil
        for var in range(2):
            back = qr - kc + var * ATTN_BLOCK
            valid = (back >= 0) & (back <= steps)
            dist = (back * dil).astype(F32)
            for hh in range(2):
                tbl[bi, var, hh * ATTN_BLOCK:(hh + 1) * ATTN_BLOCK, :] = jnp.where(
                    valid, -slope_h[hh] * dist, NEG_BIG)

    for bi, (window, dil) in enumerate(DILATED_BRANCHES):
        cls_len = seq // dil
        nb = cls_len // ATTN_BLOCK
        nb_shift = nb.bit_length() - 1

        def rows(start, size, dil=dil):
            if dil == 1:
                return pl.ds(pl.multiple_of(start, ATTN_BLOCK), size)
            return pl.ds(start, size, stride=dil)

        def block(it, bi=bi, dil=dil, nb=nb, nb_shift=nb_shift, rows=rows):
            r = lax.shift_right_logical(it, nb_shift)
            i = it & (nb - 1)
            var = jnp.minimum(i, 1)
            q_rows = rows(i * ATTN_BLOCK * dil + r, ATTN_BLOCK)
            k_rows = rows((i - var) * ATTN_BLOCK * dil + r, 2 * ATTN_BLOCK)
            q2 = qf[q_rows, :]
            kb = kf[k_rows, :].astype(BF16)
            v2 = vf[k_rows, :]
            qs = jnp.concatenate([jnp.where(head0, q2, 0.0), jnp.where(head0, 0.0, q2)], axis=0)
            s = _dot_nt(qs.astype(BF16), kb) + tbl[bi, var]
            m = jnp.max(s, axis=-1, keepdims=True)
            p = jnp.exp(s - m)
            den = jnp.sum(p, axis=-1, keepdims=True)
            pb = p.astype(BF16)
            vs = jnp.concatenate([jnp.where(head0, v2, 0.0), jnp.where(head0, 0.0, v2)], axis=0)
            o = _dot(jnp.concatenate([pb[:ATTN_BLOCK], pb[ATTN_BLOCK:]], axis=1), vs.astype(BF16))
            den2 = jnp.where(head0, den[:ATTN_BLOCK], den[ATTN_BLOCK:])
            lse = jnp.where(head0, m[:ATTN_BLOCK], m[ATTN_BLOCK:]) + jnp.log(den2)
            return q_rows, o / den2, lse

        def body(step, carry, bi=bi, block=block):
            done = [block(step * ATTN_UNROLL + j) for j in range(ATTN_UNROLL)]
            if bi > 0:
                merged = []
                for q_rows, o, lse in done:
                    l_old = lacc[q_rows, :]
                    m2 = jnp.maximum(l_old, lse)
                    a = jnp.exp(l_old - m2)
                    b = jnp.exp(lse - m2)
                    tot = a + b
                    merged.append((q_rows, (oacc[q_rows, :] * a + o * b) / tot, m2 + jnp.log(tot)))
                done = merged
            for q_rows, o, lse in done:
                oacc[q_rows, :] = o
                lacc[q_rows, :] = lse
            return carry

        assert (dil * nb) % ATTN_UNROLL == 0
        lax.fori_loop(0, dil * nb // ATTN_UNROLL, body, 0)

    o_ref[0] = oacc[...].astype(BF16)


def _attention(q, k, v, slopes_lane):
    B, S, _ = q.shape
    for window, dil in DILATED_BRANCHES:
        assert window // dil <= ATTN_BLOCK
        cls_len = S // dil
        assert S % dil == 0 and cls_len % ATTN_BLOCK == 0 and cls_len >= 2 * ATTN_BLOCK
        assert (cls_len // ATTN_BLOCK) & (cls_len // ATTN_BLOCK - 1) == 0
    n_pairs = ATTN_WIDTH // LANES
    tile = pl.BlockSpec((1, S, LANES), lambda b, p: (b, 0, p))
    return pl.pallas_call(
        functools.partial(_attn_kernel, seq=S),
        out_shape=jax.ShapeDtypeStruct((B, S, ATTN_WIDTH), BF16),
        grid=(B, n_pairs),
        in_specs=[pl.BlockSpec((1, 1, LANES), lambda b, p: (p, 0, 0)), tile, tile, tile],
        out_specs=tile,
        scratch_shapes=[pltpu.VMEM((S, LANES), F32)] * 5
        + [pltpu.VMEM((len(DILATED_BRANCHES), 2, 2 * ATTN_BLOCK, 2 * ATTN_BLOCK), F32)],
        compiler_params=_params(("arbitrary", "arbitrary")),
        name="attn",
    )(slopes_lane, q, k, v)


def _mix_kernel(attn_ref, u_ref, z_ref, x_ref, mod_ref, ga_ref, gs_ref, lng_ref, lnb_ref, wsp_ref,
                bsp_ref, wout_ref, gffn_ref, wrt_ref, wsgu_ref, wsd_ref,
                hs_ref, n2p_ref, lg_ref):
    tm = x_ref.shape[1]
    nc = tm // CHUNK
    a_n = _rms(attn_ref[0].astype(F32), ga_ref[...])

    ug = jax.nn.gelu(u_ref[0].astype(F32))
    zg = jax.nn.gelu(z_ref[0].astype(F32))
    mu = jnp.mean(zg, axis=-1, keepdims=True)
    zc = zg - mu
    var = jnp.mean(zc * zc, axis=-1, keepdims=True)
    zb = (zc * lax.rsqrt(var + EPS) * lng_ref[...] + lnb_ref[...]).astype(BF16)

    row = lax.broadcasted_iota(jnp.int32, (CHUNK, CHUNK), 0)
    col = lax.broadcasted_iota(jnp.int32, (CHUNK, CHUNK), 1)
    per_group = []
    for g in range(SGU_GROUPS):
        wc = jnp.where(row >= col, wsp_ref[g], 0.0).astype(BF16)
        lanes = slice(g * SGU_GROUP_DIM, (g + 1) * SGU_GROUP_DIM)
        zcat = jnp.concatenate([zb[c * CHUNK:(c + 1) * CHUNK, lanes] for c in range(nc)], axis=1)
        per_group.append(_dot(wc, zcat) + bsp_ref[:, g:g + 1])
    mixed = jnp.concatenate(
        [jnp.concatenate([per_group[g][:, c * CHUNK:(c + 1) * CHUNK] for g in range(SGU_GROUPS)], axis=1)
         for c in range(nc)], axis=0)
    s_n = _rms(ug * mixed, gs_ref[...])

    mix = (_dot(a_n.astype(BF16), wout_ref[0:ATTN_WIDTH, :])
           + _dot(s_n.astype(BF16), wout_ref[ATTN_WIDTH:, :]))
    gate1 = mod_ref[0, 2:3, :]
    shift2 = mod_ref[0, 3:4, :]
    scale2 = mod_ref[0, 4:5, :]
    gate2 = mod_ref[0, 5:6, :]
    h1 = x_ref[0] + gate1 * mix
    n2 = _rms(h1, gffn_ref[...]) * (1.0 + scale2) + shift2
    n2b = n2.astype(BF16)

    lg_ref[...] = _dot_nt(wrt_ref[...], n2b)
    gu = _dot(n2b, wsgu_ref[...])
    gsh = gu[:, :EXPERT_DIM]
    act = (gsh * jax.nn.sigmoid(gsh)) * gu[:, EXPERT_DIM:]
    shared = _dot(act.astype(BF16), wsd_ref[...])
    hs_ref[0] = h1 + gate2 * shared
    _store_rows(n2p_ref, _pack_halves(n2))


def _mix(attn, u, z, x, mod3, g_out_attn, g_out_sgu, ln_g, ln_b, w_spatial, b_spatial_t, w_out_bf,
         g_norm_ffn, w_router_t_bf, w_sh_gu_bf, w_sh_d_bf):
    B, S, _ = x.shape
    tm = TM_PROJ
    nt = S // tm
    N = B * S
    half_tile = pl.BlockSpec((1, tm, ATTN_WIDTH), lambda b, i: (b, i, 0))
    full_tile = pl.BlockSpec((1, tm, D_MODEL), lambda b, i: (b, i, 0))

    def const(shape):
        return pl.BlockSpec(shape, lambda b, i: (0,) * len(shape))

    return pl.pallas_call(
        _mix_kernel,
        out_shape=(
            jax.ShapeDtypeStruct((B, S, D_MODEL), F32),
            jax.ShapeDtypeStruct((N * ROW_SPLIT, LANES), jnp.uint32),
            jax.ShapeDtypeStruct((N_EXPERTS, N), F32),
        ),
        grid=(B, nt),
        in_specs=[
            half_tile, half_tile, half_tile, full_tile,
            pl.BlockSpec((1, 6, D_MODEL), lambda b, i: (b, 0, 0)),
            const((1, ATTN_WIDTH)), const((1, SGU_WIDTH)), const((1, SGU_WIDTH)), const((1, SGU_WIDTH)),
            const((SGU_GROUPS, CHUNK, CHUNK)), const((CHUNK, SGU_GROUPS)),
            const((ATTN_WIDTH + SGU_WIDTH, D_MODEL)), const((1, D_MODEL)),
            const((N_EXPERTS, D_MODEL)), const((D_MODEL, 2 * EXPERT_DIM)), const((EXPERT_DIM, D_MODEL)),
        ],
        out_specs=(
            full_tile,
            pl.BlockSpec((tm * ROW_SPLIT, LANES), lambda b, i: (b * nt + i, 0)),
            pl.BlockSpec((N_EXPERTS, tm), lambda b, i: (0, b * nt + i)),
        ),
        compiler_params=_params(("arbitrary", "arbitrary")),
        name="mix",
    )(attn, u, z, x, mod3, g_out_attn.reshape(1, -1), g_out_sgu.reshape(1, -1), ln_g.reshape(1, -1),
      ln_b.reshape(1, -1), w_spatial, b_spatial_t, w_out_bf, g_norm_ffn.reshape(1, -1), w_router_t_bf,
      w_sh_gu_bf, w_sh_d_bf)


def _first_max(v, iota, size):
    mx = jnp.max(v, axis=0, keepdims=True)
    am = jnp.min(jnp.where(v == mx, iota, size), axis=0, keepdims=True)
    return mx, am


def _route_kernel(lg_ref, br_ref, idx_ref, gate_ref, rank_ref, cnt_ref, carry, tri):
    step = pl.program_id(0)
    tr = lg_ref.shape[1]

    @pl.when(step == 0)
    def _():
        carry[...] = jnp.zeros_like(carry)
        before = (lax.broadcasted_iota(jnp.int32, (tr, tr), 0)
                  < lax.broadcasted_iota(jnp.int32, (tr, tr), 1))
        tri[...] = jnp.where(before, 1.0, 0.0).astype(BF16)

    scores = jax.nn.sigmoid(lg_ref[...])
    choice = scores + br_ref[...]

    iota_g = lax.broadcasted_iota(jnp.int32, (GROUP_SIZE, tr), 0)
    gs = []
    for g in range(N_EXPERT_GROUPS):
        cg = choice[g * GROUP_SIZE:(g + 1) * GROUP_SIZE, :]
        m1, am = _first_max(cg, iota_g, GROUP_SIZE)
        m2 = jnp.max(jnp.where(iota_g == am, -jnp.inf, cg), axis=0, keepdims=True)
        gs.append(m1 + m2)
    gscore = jnp.concatenate(gs, axis=0)

    iota_n = lax.broadcasted_iota(jnp.int32, (N_EXPERT_GROUPS, tr), 0)
    t = gscore
    for _ in range(TOPK_GROUPS - 1):
        _, am = _first_max(t, iota_n, N_EXPERT_GROUPS)
        t = jnp.where(iota_n == am, -jnp.inf, t)
    kth = jnp.max(t, axis=0, keepdims=True)
    keep = gscore >= kth

    v = jnp.concatenate(
        [jnp.where(keep[g:g + 1, :], choice[g * GROUP_SIZE:(g + 1) * GROUP_SIZE, :], -jnp.inf)
         for g in range(N_EXPERT_GROUPS)], axis=0)
    iota_e = lax.broadcasted_iota(jnp.int32, (N_EXPERTS, tr), 0)
    idxs, sels = [], []
    chosen = jnp.zeros((N_EXPERTS, tr), F32)
    for _ in range(TOP_K):
        _, am = _first_max(v, iota_e, N_EXPERTS)
        hit = iota_e == am
        idxs.append(am)
        sels.append(jnp.sum(jnp.where(hit, scores, 0.0), axis=0, keepdims=True))
        chosen = jnp.where(hit, 1.0, chosen)
        v = jnp.where(hit, -jnp.inf, v)
    sel = jnp.concatenate(sels, axis=0)
    idx_ref[...] = jnp.concatenate(idxs, axis=0)
    gate_ref[...] = sel / jnp.sum(sel, axis=0, keepdims=True) * ROUTED_SCALE

    earlier = _dot(chosen.astype(BF16), tri[...]) + carry[...]
    rank_ref[...] = jnp.concatenate(
        [jnp.sum(jnp.where(iota_e == am, earlier, 0.0), axis=0, keepdims=True) for am in idxs],
        axis=0).astype(jnp.int32)
    carry[...] = carry[...] + jnp.sum(chosen, axis=1, keepdims=True)
    cnt_ref[...] = jnp.broadcast_to(carry[...], cnt_ref.shape)


def _route(logits_t, b_router):
    N = logits_t.shape[1]
    tr = TM_ROUTE
    kt = pl.BlockSpec((TOP_K, tr), lambda i: (0, i))
    return pl.pallas_call(
        _route_kernel,
        out_shape=(
            jax.ShapeDtypeStruct((TOP_K, N), jnp.int32),
            jax.ShapeDtypeStruct((TOP_K, N), F32),
            jax.ShapeDtypeStruct((TOP_K, N), jnp.int32),
            jax.ShapeDtypeStruct((N_EXPERTS, LANES), F32),
        ),
        grid=(N // tr,),
        in_specs=[
            pl.BlockSpec((N_EXPERTS, tr), lambda i: (0, i)),
            pl.BlockSpec((N_EXPERTS, 1), lambda i: (0, 0)),
        ],
        out_specs=(kt, kt, kt, pl.BlockSpec((N_EXPERTS, LANES), lambda i: (0, 0))),
        scratch_shapes=[pltpu.VMEM((N_EXPERTS, 1), F32), pltpu.VMEM((tr, tr), BF16)],
        compiler_params=_params(("arbitrary",)),
        name="route",
    )(logits_t, b_router.reshape(N_EXPERTS, 1))


def _dest_kernel(idx_ref, rank_ref, ps_ref, dest_ref):
    tr = idx_ref.shape[1]
    iota_e = lax.broadcasted_iota(jnp.int32, (N_EXPERTS, tr), 0)
    idx = idx_ref[...]
    start = jnp.concatenate(
        [jnp.sum(jnp.where(iota_e == idx[k:k + 1, :], ps_ref[...], 0.0), axis=0, keepdims=True)
         for k in range(TOP_K)], axis=0)
    dest_ref[...] = (start.astype(jnp.int32) + rank_ref[...]) * ROW_SPLIT


def _dest(idx_t, rank_t, pstart):
    N = idx_t.shape[1]
    tr = TM_ROUTE
    kt = pl.BlockSpec((TOP_K, tr), lambda i: (0, i))
    return pl.pallas_call(
        _dest_kernel,
        out_shape=jax.ShapeDtypeStruct((TOP_K, N), jnp.int32),
        grid=(N // tr,),
        in_specs=[kt, kt, pl.BlockSpec((N_EXPERTS, 1), lambda i: (0, 0))],
        out_specs=kt,
        compiler_params=_params(("arbitrary",)),
        name="dest",
    )(idx_t, rank_t, pstart.astype(F32).reshape(N_EXPERTS, 1))


def _row_copy(src, dst, sem):
    return pltpu.make_async_copy(src, dst, sem)


def _token_rows(start):
    return pl.ds(pl.multiple_of(start, ROW_SPLIT), ROW_SPLIT)


def _block_rows(block):
    return pl.ds(pl.multiple_of(block * BLOCK_SUBROWS, BLOCK_SUBROWS), BLOCK_SUBROWS)


def _disp_kernel(zb_ref, dest_ref, x_ref, xs_ref, xbuf, zbuf, sem_in, sem_out):
    i = pl.program_id(0)
    n = pl.num_programs(0)
    tile_rows = xbuf.shape[1]
    tm = tile_rows // ROW_SPLIT
    slot = lax.rem(i, DISP_RING)
    ahead = lax.rem(i + 1, DISP_RING)

    def load(tile, into):
        start = pl.multiple_of(tile * tile_rows, tile_rows)
        return _row_copy(x_ref.at[pl.ds(start, tile_rows)], xbuf.at[into], sem_in.at[into])

    def drain(of):
        for k in range(TOP_K):
            _row_copy(xbuf.at[of], xs_ref.at[pl.ds(0, tile_rows)], sem_out.at[of]).wait()

    @pl.when(i == 0)
    def _():
        zbuf[...] = jnp.zeros_like(zbuf)

        def zero(j, started):
            blk = zb_ref[j]

            @pl.when(blk >= 0)
            def _():
                _row_copy(zbuf, xs_ref.at[_block_rows(blk)], sem_out.at[0]).start()

            return started + jnp.where(blk >= 0, 1, 0)

        def done(j, carry):
            _row_copy(zbuf, xs_ref.at[_block_rows(0)], sem_out.at[0]).wait()
            return carry

        started = lax.fori_loop(0, zb_ref.shape[0], zero, 0)
        lax.fori_loop(0, started, done, 0)
        load(0, 0).start()

    @pl.when(i + 1 < n)
    def _():
        @pl.when(i + 1 >= DISP_RING)
        def _():
            drain(ahead)

        load(i + 1, ahead).start()

    load(i, slot).wait()

    def issue(t, carry):
        src = xbuf.at[slot, _token_rows(t * ROW_SPLIT)]
        base = t * TOP_K
        for k in range(TOP_K):
            _row_copy(src, xs_ref.at[_token_rows(dest_ref[base + k])], sem_out.at[slot]).start(priority=k % 2)
        return carry

    lax.fori_loop(0, tm, issue, 0)

    @pl.when(i == n - 1)
    def _():
        for s in range(DISP_RING):
            drain(s)


def _dispatch(zero_blocks, dest_flat, n2p, n_rows):
    N = n2p.shape[0] // ROW_SPLIT
    tm = TM_MOVE
    assert N // tm >= DISP_RING
    grid_spec = pltpu.PrefetchScalarGridSpec(
        num_scalar_prefetch=1,
        grid=(N // tm,),
        in_specs=[
            pl.BlockSpec((tm * TOP_K,), lambda i, zb: (i,), memory_space=pltpu.SMEM),
            pl.BlockSpec(memory_space=pl.ANY),
        ],
        out_specs=pl.BlockSpec(memory_space=pl.ANY),
        scratch_shapes=[
            pltpu.VMEM((DISP_RING, tm * ROW_SPLIT, LANES), jnp.uint32),
            pltpu.VMEM((BLOCK_SUBROWS, LANES), jnp.uint32),
            pltpu.SemaphoreType.DMA((DISP_RING,)),
            pltpu.SemaphoreType.DMA((DISP_RING,)),
        ],
    )
    return pl.pallas_call(
        _disp_kernel,
        out_shape=jax.ShapeDtypeStruct((n_rows * ROW_SPLIT, LANES), jnp.uint32),
        grid_spec=grid_spec,
        compiler_params=_params(("arbitrary",)),
        name="disp",
    )(zero_blocks, dest_flat, n2p)


def _experts_kernel(eb_ref, bv_ref, xs_ref, wg_ref, wu_ref, wd_ref, ys_ref,
                    xbuf, ybuf, wgb, wub, wdb, sem_in, sem_out):
    e = pl.program_id(0)
    n_used = eb_ref[N_EXPERTS]
    n_blocks = ys_ref.shape[0] // BLOCK_SUBROWS
    first = eb_ref[e]
    end = eb_ref[e + 1]

    def in_copy(g, slot):
        return _row_copy(xs_ref.at[_block_rows(g)], xbuf.at[slot], sem_in.at[slot])

    def out_copy(g, slot):
        return _row_copy(ybuf.at[slot], ys_ref.at[_block_rows(g)], sem_out.at[slot])

    @pl.when(e == 0)
    def _():
        for g in range(RING - 1):
            @pl.when(g < n_used)
            def _(g=g):
                in_copy(g, g).start()

    @pl.when(end > first)
    def _():
        wgb[...] = wg_ref[...].astype(BF16)
        wub[...] = wu_ref[...].astype(BF16)
        wdb[...] = wd_ref[...].astype(BF16)

        def block(g, carry):
            slot = g & (RING - 1)
            in_copy(g, slot).wait()

            @pl.when(g + RING - 1 < n_used)
            def _():
                in_copy(g + RING - 1, (g + RING - 1) & (RING - 1)).start()

            @pl.when(g >= RING)
            def _():
                out_copy(g - RING, slot).wait()

            lo, hi = _unpack_halves(_load_rows(xbuf.at[slot], ROW_BLOCK))
            live = lax.broadcasted_iota(jnp.int32, lo.shape, 0) < bv_ref[g]
            lo = jnp.where(live, lo, 0.0).astype(BF16)
            hi = jnp.where(live, hi, 0.0).astype(BF16)
            gate = _dot(lo, wgb[0:HALF, :]) + _dot(hi, wgb[HALF:, :])
            up = _dot(lo, wub[0:HALF, :]) + _dot(hi, wub[HALF:, :])
            act = (gate * jax.nn.sigmoid(gate)) * up
            _store_rows(ybuf.at[slot], _pack_halves(_dot(act.astype(BF16), wdb[...])))
            out_copy(g, slot).start()
            return carry

        lax.fori_loop(first, end, block, 0)

    @pl.when(e == N_EXPERTS - 1)
    def _():
        for back in range(1, RING + 1):
            @pl.when(n_used >= back)
            def _(back=back):
                out_copy(n_used - back, (n_used - back) & (RING - 1)).wait()

        xbuf[0] = jnp.zeros((BLOCK_SUBROWS, LANES), jnp.uint32)

        def zero(g, carry):
            _row_copy(xbuf.at[0], ys_ref.at[_block_rows(g)], sem_out.at[0]).start()
            return carry

        def done(g, carry):
            _row_copy(xbuf.at[0], ys_ref.at[_block_rows(g)], sem_out.at[0]).wait()
            return carry

        lax.fori_loop(n_used, n_blocks, zero, 0)
        lax.fori_loop(n_used, n_blocks, done, 0)


def _experts(expert_block, block_valid, xs, w_gate, w_up, w_down):
    def w_map(e, eb, bv):
        return (e, 0, 0)

    grid_spec = pltpu.PrefetchScalarGridSpec(
        num_scalar_prefetch=2,
        grid=(N_EXPERTS,),
        in_specs=[
            pl.BlockSpec(memory_space=pl.ANY),
            pl.BlockSpec((None, D_MODEL, EXPERT_DIM), w_map),
            pl.BlockSpec((None, D_MODEL, EXPERT_DIM), w_map),
            pl.BlockSpec((None, EXPERT_DIM, D_MODEL), w_map),
        ],
        out_specs=pl.BlockSpec(memory_space=pl.ANY),
        scratch_shapes=[
            pltpu.VMEM((RING, BLOCK_SUBROWS, LANES), jnp.uint32),
            pltpu.VMEM((RING, BLOCK_SUBROWS, LANES), jnp.uint32),
            pltpu.VMEM((D_MODEL, EXPERT_DIM), BF16),
            pltpu.VMEM((D_MODEL, EXPERT_DIM), BF16),
            pltpu.VMEM((EXPERT_DIM, D_MODEL), BF16),
            pltpu.SemaphoreType.DMA((RING,)),
            pltpu.SemaphoreType.DMA((RING,)),
        ],
    )
    return pl.pallas_call(
        _experts_kernel,
        out_shape=jax.ShapeDtypeStruct(xs.shape, jnp.uint32),
        grid_spec=grid_spec,
        compiler_params=_params(("arbitrary",)),
        name="experts",
    )(expert_block, block_valid, xs, w_gate, w_up, w_down)


def _comb_kernel(dest_ref, next_ref, ys_ref, hs_ref, gate_ref, mod_ref, gf_ref, o_ref, buf, sem):
    i = pl.program_id(0)
    tm = hs_ref.shape[0]
    slot = i & 1

    def gather(rows_ref, into):
        def issue(t, carry):
            base = t * TOP_K
            for k in range(TOP_K):
                _row_copy(ys_ref.at[_token_rows(rows_ref[base + k])],
                          buf.at[into, k, _token_rows(t * ROW_SPLIT)], sem.at[into]).start(priority=k % 2)
            return carry

        lax.fori_loop(0, tm, issue, 0)

    @pl.when(i == 0)
    def _():
        gather(dest_ref, 0)

    for into in range(2):
        @pl.when(jnp.logical_and(i + 1 < pl.num_programs(0), slot == 1 - into))
        def _(into=into):
            gather(next_ref, into)

    for k in range(TOP_K):
        _row_copy(ys_ref.at[pl.ds(0, tm * ROW_SPLIT)], buf.at[slot, k], sem.at[slot]).wait()

    for c in range(tm // COMB_CHUNK):
        tok = slice(c * COMB_CHUNK, (c + 1) * COMB_CHUNK)
        gates = gate_ref[tok, :]
        lo = jnp.zeros((COMB_CHUNK, HALF), F32)
        hi = jnp.zeros((COMB_CHUNK, HALF), F32)
        for k in range(TOP_K):
            piece = buf.at[slot, k, pl.ds(c * COMB_CHUNK * ROW_SPLIT, COMB_CHUNK * ROW_SPLIT)]
            lo_k, hi_k = _unpack_halves(_load_rows(piece, COMB_CHUNK))
            gk = gates[:, k:k + 1]
            lo = lo + gk * lo_k
            hi = hi + gk * hi_k
        routed = jnp.concatenate([lo, hi], axis=1)
        h2 = hs_ref[tok, :] + mod_ref[0, 5:6, :] * routed
        o_ref[tok, :] = _rms(h2, gf_ref[...])


def _combine(dest_flat, ys, hs2, gates_nk, mod3, g_final, seq):
    N = hs2.shape[0]
    tm = TM_MOVE
    per_seq = seq // tm
    n_tiles = N // tm
    return pl.pallas_call(
        _comb_kernel,
        out_shape=jax.ShapeDtypeStruct((N, D_MODEL), F32),
        grid=(n_tiles,),
        in_specs=[
            pl.BlockSpec((tm * TOP_K,), lambda i: (i,), memory_space=pltpu.SMEM),
            pl.BlockSpec((tm * TOP_K,), lambda i: (jnp.minimum(i + 1, n_tiles - 1),), memory_space=pltpu.SMEM),
            pl.BlockSpec(memory_space=pl.ANY),
            pl.BlockSpec((tm, D_MODEL), lambda i: (i, 0)),
            pl.BlockSpec((tm, TOP_K), lambda i: (i, 0)),
            pl.BlockSpec((1, 6, D_MODEL), lambda i: (i // per_seq, 0, 0)),
            pl.BlockSpec((1, D_MODEL), lambda i: (0, 0)),
        ],
        out_specs=pl.BlockSpec((tm, D_MODEL), lambda i: (i, 0)),
        scratch_shapes=[pltpu.VMEM((2, TOP_K, tm * ROW_SPLIT, LANES), jnp.uint32),
                        pltpu.SemaphoreType.DMA((2,))],
        compiler_params=_params(("arbitrary",)),
        name="comb",
    )(dest_flat, dest_flat, ys, hs2, gates_nk, mod3, g_final.reshape(1, D_MODEL))


def _layer(x, mod3, g_norm_mix, w_in, sgu_ln_gain, sgu_ln_bias, w_spatial, b_spatial, g_out_attn,
           g_out_sgu, w_out, g_norm_ffn, w_router, b_router, w_exp_gate, w_exp_up, w_exp_down,
           w_sh_gate, w_sh_up, w_sh_down, g_final):
    B, S, _ = x.shape
    N = B * S

    q, k, v, u, z = _inproj(x, mod3, g_norm_mix, w_in.astype(BF16))
    head_of_lane = jnp.arange(ATTN_WIDTH) // HEAD_DIM
    slopes = jnp.exp2(-8.0 * jnp.arange(1, ATTN_HEADS + 1, dtype=F32) / ATTN_HEADS)
    slopes_lane = slopes[head_of_lane].reshape(ATTN_WIDTH // LANES, 1, LANES)
    attn = _attention(q, k, v, slopes_lane)

    hs, n2p, logits_t = _mix(
        attn, u, z, x, mod3, g_out_attn, g_out_sgu, sgu_ln_gain, sgu_ln_bias, w_spatial, b_spatial.T,
        w_out.astype(BF16), g_norm_ffn, w_router.T.astype(BF16),
        jnp.concatenate([w_sh_gate, w_sh_up], axis=1).astype(BF16), w_sh_down.astype(BF16))

    idx_t, gates_t, rank_t, counts = _route(logits_t, b_router)

    n_blocks = (N * TOP_K + N_EXPERTS * (ROW_BLOCK - 1)) // ROW_BLOCK
    cnt = counts[:, 0].astype(jnp.int32)
    padded = (cnt + ROW_BLOCK - 1) // ROW_BLOCK * ROW_BLOCK
    pends = jnp.cumsum(padded)
    pstart = pends - padded
    block_row = jnp.arange(n_blocks, dtype=jnp.int32) * ROW_BLOCK
    block_e = jnp.sum(pends[None, :] <= block_row[:, None], axis=1, dtype=jnp.int32)
    block_e = jnp.minimum(block_e, N_EXPERTS - 1)
    own = block_e[:, None] == jnp.arange(N_EXPERTS, dtype=jnp.int32)[None, :]
    live_end = jnp.sum(jnp.where(own, (pstart + cnt)[None, :], 0), axis=1)
    block_valid = jnp.clip(live_end - block_row, 0, ROW_BLOCK).astype(jnp.int32)
    n_used = (pends[-1:] // ROW_BLOCK).astype(jnp.int32)
    expert_block = jnp.concatenate([pstart // ROW_BLOCK, n_used]).astype(jnp.int32)

    last_block = jnp.where(cnt > 0, pends // ROW_BLOCK - 1, -1)
    tail_block = n_used[0] + jnp.arange(n_blocks - N * TOP_K // ROW_BLOCK)
    tail_block = jnp.where(tail_block < n_blocks, tail_block, -1)
    zero_blocks = jnp.concatenate([last_block, tail_block]).astype(jnp.int32)

    dest_flat = _dest(idx_t, rank_t, pstart).T.reshape(N * TOP_K)
    xs = _dispatch(zero_blocks, dest_flat, n2p, n_blocks * ROW_BLOCK)
    ys = _experts(expert_block, block_valid, xs, w_exp_gate, w_exp_up, w_exp_down)
    out = _combine(dest_flat, ys, hs.reshape(N, D_MODEL), gates_t.T, mod3, g_final, S)
    return out.reshape(B, S, D_MODEL)


def kernel(x, c, w_ada, b_ada, g_norm_mix, w_in, sgu_ln_gain, sgu_ln_bias, w_spatial, b_spatial, g_out_attn, g_out_sgu, w_out, g_norm_ffn, w_router, b_router, w_exp_gate, w_exp_up, w_exp_down, w_sh_gate, w_sh_up, w_sh_down, g_final):
    assert w_ada.shape[0] == 1, "single-layer stack"
    B = x.shape[0]
    mod3 = _ada(c, w_ada[0], b_ada[0]).reshape(B, 6, D_MODEL)
    return _layer(x, mod3, g_norm_mix[0], w_in[0], sgu_ln_gain[0], sgu_ln_bias[0], w_spatial[0],
                  b_spatial[0], g_out_attn[0], g_out_sgu[0], w_out[0], g_norm_ffn[0], w_router[0],
                  b_router[0], w_exp_gate[0], w_exp_up[0], w_exp_down[0], w_sh_gate[0], w_sh_up[0],
                  w_sh_down[0], g_final)
```

```python
import functools

import jax
import jax.numpy as jnp
from jax import lax
from jax.experimental import pallas as pl
from jax.experimental.pallas import tpu as pltpu

F32 = jnp.float32
BF16 = jnp.bfloat16

D_MODEL = 1024
ATTN_WIDTH = 512
ATTN_HEADS = 8
HEAD_DIM = 64
SGU_WIDTH = 512
SGU_GROUPS = 4
SGU_GROUP_DIM = 128
CHUNK = 128
DILATED_BRANCHES = ((128, 1), (512, 4), (2048, 16))
ATTN_BLOCK = 128
N_EXPERTS = 256
TOP_K = 8
N_EXPERT_GROUPS = 8
GROUP_SIZE = N_EXPERTS // N_EXPERT_GROUPS
TOPK_GROUPS = 4
EXPERT_DIM = 256
ROUTED_SCALE = 2.5
EPS = 1e-6

LANES = 128
HALF = D_MODEL // 2
ROW_BLOCK = 512
NEG_BIG = -1e30
VMEM_LIMIT = 56 * 1024 * 1024

TM_PROJ = 512
TM_ROUTE = 512
TM_MOVE = 512
DISP_RING = 3
COMB_CHUNK = 128
ATTN_UNROLL = 8
RING = 4


def _dot(a, b):
    return jnp.dot(a, b, preferred_element_type=F32)


def _dot_nt(a, b):
    return lax.dot_general(a, b, (((1,), (1,)), ((), ())), preferred_element_type=F32)


def _rms(x, g):
    return x * lax.rsqrt(jnp.mean(x * x, axis=-1, keepdims=True) + EPS) * g


def _pack_halves(x):
    return pltpu.pack_elementwise([x[:, :HALF], x[:, HALF:]], packed_dtype=BF16)


def _unpack_halves(w):
    lo = pltpu.unpack_elementwise(w, index=0, packed_dtype=BF16, unpacked_dtype=F32)
    hi = pltpu.unpack_elementwise(w, index=1, packed_dtype=BF16, unpacked_dtype=F32)
    return lo, hi


ROW_SPLIT = HALF // LANES
BLOCK_SUBROWS = ROW_BLOCK * ROW_SPLIT


def _store_rows(ref, packed):
    rows = packed.shape[0]
    for c in range(ROW_SPLIT):
        ref[pl.ds(c, rows, stride=ROW_SPLIT), :] = packed[:, c * LANES:(c + 1) * LANES]


def _load_rows(ref, rows):
    return jnp.concatenate([ref[pl.ds(c, rows, stride=ROW_SPLIT), :] for c in range(ROW_SPLIT)], axis=1)


def _params(sem=None):
    return pltpu.CompilerParams(dimension_semantics=sem, vmem_limit_bytes=VMEM_LIMIT)


def _ada_kernel(c_ref, w_ref, b_ref, o_ref):
    c = c_ref[...]
    cond = c * jax.nn.sigmoid(c)
    ch = cond.astype(BF16)
    cl = (cond - ch.astype(F32)).astype(BF16)
    w = w_ref[...]
    wh = w.astype(BF16)
    wl = (w - wh.astype(F32)).astype(BF16)
    o_ref[...] = _dot(ch, wh) + _dot(cl, wh) + _dot(ch, wl) + b_ref[...]


def _ada(c, w_ada, b_ada):
    B = c.shape[0]
    n_out = w_ada.shape[1]
    tn = D_MODEL
    return pl.pallas_call(
        _ada_kernel,
        out_shape=jax.ShapeDtypeStruct((B, n_out), F32),
        grid=(n_out // tn,),
        in_specs=[
            pl.BlockSpec((B, D_MODEL), lambda j: (0, 0)),
            pl.BlockSpec((D_MODEL, tn), lambda j: (0, j)),
            pl.BlockSpec((1, tn), lambda j: (0, j)),
        ],
        out_specs=pl.BlockSpec((B, tn), lambda j: (0, j)),
        compiler_params=_params(("arbitrary",)),
        name="ada",
    )(c, w_ada, b_ada.reshape(1, n_out))


def _inproj_kernel(x_ref, mod_ref, g_ref, w_ref, q_ref, k_ref, v_ref, u_ref, z_ref):
    x = x_ref[0]
    shift = mod_ref[0, 0:1, :]
    scale = mod_ref[0, 1:2, :]
    n = _rms(x, g_ref[...]) * (1.0 + scale) + shift
    p = _dot(n.astype(BF16), w_ref[...])
    q_ref[0] = (p[:, 0:ATTN_WIDTH] * (HEAD_DIM ** -0.5)).astype(BF16)
    k_ref[0] = p[:, ATTN_WIDTH:2 * ATTN_WIDTH].astype(BF16)
    v_ref[0] = p[:, 2 * ATTN_WIDTH:3 * ATTN_WIDTH].astype(BF16)
    u_ref[0] = p[:, 3 * ATTN_WIDTH:3 * ATTN_WIDTH + SGU_WIDTH].astype(BF16)
    z_ref[0] = p[:, 3 * ATTN_WIDTH + SGU_WIDTH:].astype(BF16)


def _inproj(x, mod3, g_norm, w_in_bf):
    B, S, _ = x.shape
    tm = TM_PROJ
    n_in = w_in_bf.shape[1]
    out = jax.ShapeDtypeStruct((B, S, ATTN_WIDTH), BF16)
    tile = pl.BlockSpec((1, tm, ATTN_WIDTH), lambda b, i: (b, i, 0))
    return pl.pallas_call(
        _inproj_kernel,
        out_shape=(out,) * 5,
        grid=(B, S // tm),
        in_specs=[
            pl.BlockSpec((1, tm, D_MODEL), lambda b, i: (b, i, 0)),
            pl.BlockSpec((1, 6, D_MODEL), lambda b, i: (b, 0, 0)),
            pl.BlockSpec((1, D_MODEL), lambda b, i: (0, 0)),
            pl.BlockSpec((D_MODEL, n_in), lambda b, i: (0, 0)),
        ],
        out_specs=(tile,) * 5,
        compiler_params=_params(("arbitrary", "arbitrary")),
        name="inproj",
    )(x, mod3, g_norm.reshape(1, D_MODEL), w_in_bf)


def _attn_kernel(slope_ref, q_ref, k_ref, v_ref, o_ref, qf, kf, vf, oacc, lacc, tbl, *, seq):
    qf[...] = q_ref[0].astype(F32)
    kf[...] = k_ref[0].astype(F32)
    vf[...] = v_ref[0].astype(F32)

    lane = lax.broadcasted_iota(jnp.int32, (1, LANES), 1)
    head0 = lane < HEAD_DIM
    slopes = slope_ref[0]
    slope_h = (slopes[:, 0:1], slopes[:, HEAD_DIM:HEAD_DIM + 1])

    whole_class = [seq // dil == 2 * ATTN_BLOCK for _, dil in DILATED_BRANCHES]
    for bi, (window, dil) in enumerate(DILATED_BRANCHES):
        steps = window // dil
        if whole_class[bi]:
            back = (lax.broadcasted_iota(jnp.int32, (2 * ATTN_BLOCK, 2 * ATTN_BLOCK), 0)
                    - lax.broadcasted_iota(jnp.int32, (2 * ATTN_BLOCK, 2 * ATTN_BLOCK), 1))
            valid = (back >= 0) & (back <= steps)
            dist = (back * dil).astype(F32)
            for hh in range(2):
                tbl[bi, hh] = jnp.where(valid, -slope_h[hh] * dist, NEG_BIG)
            continue
        qr = lax.broadcasted_iota(jnp.int32, (ATTN_BLOCK, 2 * ATTN_BLOCK), 0)
        kc = lax.broadcasted_iota(jnp.int32, (ATTN_BLOCK, 2 * ATTN_BLOCK), 1)
        for var in range(2):
            back = qr - kc + var * ATTN_BLOCK
            valid = (back >= 0) & (back <= steps)
            dist = (back * dil).astype(F32)
            for hh in range(2):
                tbl[bi, var, hh * ATTN_BLOCK:(hh + 1) * ATTN_BLOCK, :] = jnp.where(
                    valid, -slope_h[hh] * dist, NEG_BIG)

    for bi, (window, dil) in enumerate(DILATED_BRANCHES):
        cls_len = seq // dil
        nb = cls_len // ATTN_BLOCK
        nb_shift = nb.bit_length() - 1
        whole = whole_class[bi]
        qn = 2 * ATTN_BLOCK if whole else ATTN_BLOCK
        n_units = dil if whole else dil * nb
        per_step = ATTN_UNROLL * ATTN_BLOCK // qn

        def rows(start, size, dil=dil):
            if dil == 1:
                return pl.ds(pl.multiple_of(start, ATTN_BLOCK), size)
            return pl.ds(start, size, stride=dil)

        def block(it, bi=bi, dil=dil, nb=nb, nb_shift=nb_shift, rows=rows, whole=whole, qn=qn):
            if whole:
                q_rows = rows(it, qn)
                k_rows = q_rows
                table = jnp.concatenate([tbl[bi, 0], tbl[bi, 1]], axis=0)
            else:
                r = lax.shift_right_logical(it, nb_shift)
                i = it & (nb - 1)
                var = jnp.minimum(i, 1)
                q_rows = rows(i * ATTN_BLOCK * dil + r, qn)
                k_rows = rows((i - var) * ATTN_BLOCK * dil + r, 2 * ATTN_BLOCK)
                table = tbl[bi, var]
            q2 = qf[q_rows, :]
            kb = kf[k_rows, :].astype(BF16)
            v2 = vf[k_rows, :]
            qs = jnp.concatenate([jnp.where(head0, q2, 0.0), jnp.where(head0, 0.0, q2)], axis=0)
            s = _dot_nt(qs.astype(BF16), kb) + table
            m = jnp.max(s, axis=-1, keepdims=True)
            p = jnp.exp(s - m)
            den = jnp.sum(p, axis=-1, keepdims=True)
            pb = p.astype(BF16)
            vs = jnp.concatenate([jnp.where(head0, v2, 0.0), jnp.where(head0, 0.0, v2)], axis=0)
            o = _dot(jnp.concatenate([pb[:qn], pb[qn:]], axis=1), vs.astype(BF16))
            den2 = jnp.where(head0, den[:qn], den[qn:])
            lse = jnp.where(head0, m[:qn], m[qn:]) + jnp.log(den2)
            return q_rows, o / den2, lse

        def body(step, carry, bi=bi, block=block, per_step=per_step):
            done = [block(step * per_step + j) for j in range(per_step)]
            if bi > 0:
                merged = []
                for q_rows, o, lse in done:
                    l_old = lacc[q_rows, :]
                    m2 = jnp.maximum(l_old, lse)
                    a = jnp.exp(l_old - m2)
                    b = jnp.exp(lse - m2)
                    tot = a + b
                    merged.append((q_rows, (oacc[q_rows, :] * a + o * b) / tot, m2 + jnp.log(tot)))
                done = merged
            for q_rows, o, lse in done:
                oacc[q_rows, :] = o
                lacc[q_rows, :] = lse
            return carry

        assert n_units % per_step == 0
        lax.fori_loop(0, n_units // per_step, body, 0)

    o_ref[0] = oacc[...].astype(BF16)


def _attention(q, k, v, slopes_lane):
    B, S, _ = q.shape
    for window, dil in DILATED_BRANCHES:
        assert window // dil <= ATTN_BLOCK
        cls_len = S // dil
        assert S % dil == 0 and cls_len % ATTN_BLOCK == 0 and cls_len >= 2 * ATTN_BLOCK
        assert (cls_len // ATTN_BLOCK) & (cls_len // ATTN_BLOCK - 1) == 0
    n_pairs = ATTN_WIDTH // LANES
    tile = pl.BlockSpec((1, S, LANES), lambda b, p: (b, 0, p))
    return pl.pallas_call(
        functools.partial(_attn_kernel, seq=S),
        out_shape=jax.ShapeDtypeStruct((B, S, ATTN_WIDTH), BF16),
        grid=(B, n_pairs),
        in_specs=[pl.BlockSpec((1, 1, LANES), lambda b, p: (p, 0, 0)), tile, tile, tile],
        out_specs=tile,
        scratch_shapes=[pltpu.VMEM((S, LANES), F32)] * 5
        + [pltpu.VMEM((len(DILATED_BRANCHES), 2, 2 * ATTN_BLOCK, 2 * ATTN_BLOCK), F32)],
        compiler_params=_params(("arbitrary", "arbitrary")),
        name="attn",
    )(slopes_lane, q, k, v)


def _mix_kernel(attn_ref, u_ref, z_ref, x_ref, mod_ref, ga_ref, gs_ref, lng_ref, lnb_ref, wsp_ref,
                bsp_ref, wout_ref, gffn_ref, wrt_ref, wsgu_ref, wsd_ref,
                hs_ref, n2p_ref, lg_ref):
    tm = x_ref.shape[1]
    nc = tm // CHUNK
    a_n = _rms(attn_ref[0].astype(F32), ga_ref[...])

    ug = jax.nn.gelu(u_ref[0].astype(F32))
    zg = jax.nn.gelu(z_ref[0].astype(F32))
    mu = jnp.mean(zg, axis=-1, keepdims=True)
    zc = zg - mu
    var = jnp.mean(zc * zc, axis=-1, keepdims=True)
    zb = (zc * lax.rsqrt(var + EPS) * lng_ref[...] + lnb_ref[...]).astype(BF16)

    row = lax.broadcasted_iota(jnp.int32, (CHUNK, CHUNK), 0)
    col = lax.broadcasted_iota(jnp.int32, (CHUNK, CHUNK), 1)
    per_group = []
    for g in range(SGU_GROUPS):
        wc = jnp.where(row >= col, wsp_ref[g], 0.0).astype(BF16)
        lanes = slice(g * SGU_GROUP_DIM, (g + 1) * SGU_GROUP_DIM)
        zcat = jnp.concatenate([zb[c * CHUNK:(c + 1) * CHUNK, lanes] for c in range(nc)], axis=1)
        per_group.append(_dot(wc, zcat) + bsp_ref[:, g:g + 1])
    mixed = jnp.concatenate(
        [jnp.concatenate([per_group[g][:, c * CHUNK:(c + 1) * CHUNK] for g in range(SGU_GROUPS)], axis=1)
         for c in range(nc)], axis=0)
    s_n = _rms(ug * mixed, gs_ref[...])

    mix = (_dot(a_n.astype(BF16), wout_ref[0:ATTN_WIDTH, :])
           + _dot(s_n.astype(BF16), wout_ref[ATTN_WIDTH:, :]))
    gate1 = mod_ref[0, 2:3, :]
    shift2 = mod_ref[0, 3:4, :]
    scale2 = mod_ref[0, 4:5, :]
    gate2 = mod_ref[0, 5:6, :]
    h1 = x_ref[0] + gate1 * mix
    n2 = _rms(h1, gffn_ref[...]) * (1.0 + scale2) + shift2
    n2b = n2.astype(BF16)

    lg_ref[...] = _dot_nt(wrt_ref[...], n2b)
    gu = _dot(n2b, wsgu_ref[...])
    gsh = gu[:, :EXPERT_DIM]
    act = (gsh * jax.nn.sigmoid(gsh)) * gu[:, EXPERT_DIM:]
    shared = _dot(act.astype(BF16), wsd_ref[...])
    hs_ref[0] = h1 + gate2 * shared
    _store_rows(n2p_ref, _pack_halves(n2))


def _mix(attn, u, z, x, mod3, g_out_attn, g_out_sgu, ln_g, ln_b, w_spatial, b_spatial_t, w_out_bf,
         g_norm_ffn, w_router_t_bf, w_sh_gu_bf, w_sh_d_bf):
    B, S, _ = x.shape
    tm = TM_PROJ
    nt = S // tm
    N = B * S
    half_tile = pl.BlockSpec((1, tm, ATTN_WIDTH), lambda b, i: (b, i, 0))
    full_tile = pl.BlockSpec((1, tm, D_MODEL), lambda b, i: (b, i, 0))

    def const(shape):
        return pl.BlockSpec(shape, lambda b, i: (0,) * len(shape))

    return pl.pallas_call(
        _mix_kernel,
        out_shape=(
            jax.ShapeDtypeStruct((B, S, D_MODEL), F32),
            jax.ShapeDtypeStruct((N * ROW_SPLIT, LANES), jnp.uint32),
            jax.ShapeDtypeStruct((N_EXPERTS, N), F32),
        ),
        grid=(B, nt),
        in_specs=[
            half_tile, half_tile, half_tile, full_tile,
            pl.BlockSpec((1, 6, D_MODEL), lambda b, i: (b, 0, 0)),
            const((1, ATTN_WIDTH)), const((1, SGU_WIDTH)), const((1, SGU_WIDTH)), const((1, SGU_WIDTH)),
            const((SGU_GROUPS, CHUNK, CHUNK)), const((CHUNK, SGU_GROUPS)),
            const((ATTN_WIDTH + SGU_WIDTH, D_MODEL)), const((1, D_MODEL)),
            const((N_EXPERTS, D_MODEL)), const((D_MODEL, 2 * EXPERT_DIM)), const((EXPERT_DIM, D_MODEL)),
        ],
        out_specs=(
            full_tile,
            pl.BlockSpec((tm * ROW_SPLIT, LANES), lambda b, i: (b * nt + i, 0)),
            pl.BlockSpec((N_EXPERTS, tm), lambda b, i: (0, b * nt + i)),
        ),
        compiler_params=_params(("arbitrary", "arbitrary")),
        name="mix",
    )(attn, u, z, x, mod3, g_out_attn.reshape(1, -1), g_out_sgu.reshape(1, -1), ln_g.reshape(1, -1),
      ln_b.reshape(1, -1), w_spatial, b_spatial_t, w_out_bf, g_norm_ffn.reshape(1, -1), w_router_t_bf,
      w_sh_gu_bf, w_sh_d_bf)


def _first_max(v, iota, size):
    mx = jnp.max(v, axis=0, keepdims=True)
    am = jnp.min(jnp.where(v == mx, iota, size), axis=0, keepdims=True)
    return mx, am


def _route_kernel(lg_ref, br_ref, idx_ref, gate_ref, rank_ref, cnt_ref, carry, tri):
    step = pl.program_id(0)
    tr = lg_ref.shape[1]

    @pl.when(step == 0)
    def _():
        carry[...] = jnp.zeros_like(carry)
        before = (lax.broadcasted_iota(jnp.int32, (tr, tr), 0)
                  < lax.broadcasted_iota(jnp.int32, (tr, tr), 1))
        tri[...] = jnp.where(before, 1.0, 0.0).astype(BF16)

    scores = jax.nn.sigmoid(lg_ref[...])
    choice = scores + br_ref[...]

    iota_g = lax.broadcasted_iota(jnp.int32, (GROUP_SIZE, tr), 0)
    gs = []
    for g in range(N_EXPERT_GROUPS):
        cg = choice[g * GROUP_SIZE:(g + 1) * GROUP_SIZE, :]
        m1, am = _first_max(cg, iota_g, GROUP_SIZE)
        m2 = jnp.max(jnp.where(iota_g == am, -jnp.inf, cg), axis=0, keepdims=True)
        gs.append(m1 + m2)
    gscore = jnp.concatenate(gs, axis=0)

    iota_n = lax.broadcasted_iota(jnp.int32, (N_EXPERT_GROUPS, tr), 0)
    t = gscore
    for _ in range(TOPK_GROUPS - 1):
        _, am = _first_max(t, iota_n, N_EXPERT_GROUPS)
        t = jnp.where(iota_n == am, -jnp.inf, t)
    kth = jnp.max(t, axis=0, keepdims=True)
    keep = gscore >= kth

    v = jnp.concatenate(
        [jnp.where(keep[g:g + 1, :], choice[g * GROUP_SIZE:(g + 1) * GROUP_SIZE, :], -jnp.inf)
         for g in range(N_EXPERT_GROUPS)], axis=0)
    iota_e = lax.broadcasted_iota(jnp.int32, (N_EXPERTS, tr), 0)
    idxs, sels = [], []
    chosen = jnp.zeros((N_EXPERTS, tr), F32)
    for _ in range(TOP_K):
        _, am = _first_max(v, iota_e, N_EXPERTS)
        hit = iota_e == am
        idxs.append(am)
        sels.append(jnp.sum(jnp.where(hit, scores, 0.0), axis=0, keepdims=True))
        chosen = jnp.where(hit, 1.0, chosen)
        v = jnp.where(hit, -jnp.inf, v)
    sel = jnp.concatenate(sels, axis=0)
    idx_ref[...] = jnp.concatenate(idxs, axis=0)
    gate_ref[...] = sel / jnp.sum(sel, axis=0, keepdims=True) * ROUTED_SCALE

    earlier = _dot(chosen.astype(BF16), tri[...]) + carry[...]
    rank_ref[...] = jnp.concatenate(
        [jnp.sum(jnp.where(iota_e == am, earlier, 0.0), axis=0, keepdims=True) for am in idxs],
        axis=0).astype(jnp.int32)
    carry[...] = carry[...] + jnp.sum(chosen, axis=1, keepdims=True)
    cnt_ref[...] = jnp.broadcast_to(carry[...], cnt_ref.shape)


def _route(logits_t, b_router):
    N = logits_t.shape[1]
    tr = TM_ROUTE
    kt = pl.BlockSpec((TOP_K, tr), lambda i: (0, i))
    return pl.pallas_call(
        _route_kernel,
        out_shape=(
            jax.ShapeDtypeStruct((TOP_K, N), jnp.int32),
            jax.ShapeDtypeStruct((TOP_K, N), F32),
            jax.ShapeDtypeStruct((TOP_K, N), jnp.int32),
            jax.ShapeDtypeStruct((N_EXPERTS, LANES), F32),
        ),
        grid=(N // tr,),
        in_specs=[
            pl.BlockSpec((N_EXPERTS, tr), lambda i: (0, i)),
            pl.BlockSpec((N_EXPERTS, 1), lambda i: (0, 0)),
        ],
        out_specs=(kt, kt, kt, pl.BlockSpec((N_EXPERTS, LANES), lambda i: (0, 0))),
        scratch_shapes=[pltpu.VMEM((N_EXPERTS, 1), F32), pltpu.VMEM((tr, tr), BF16)],
        compiler_params=_params(("arbitrary",)),
        name="route",
    )(logits_t, b_router.reshape(N_EXPERTS, 1))


def _dest_kernel(idx_ref, rank_ref, ps_ref, dest_ref):
    tr = idx_ref.shape[1]
    iota_e = lax.broadcasted_iota(jnp.int32, (N_EXPERTS, tr), 0)
    idx = idx_ref[...]
    start = jnp.concatenate(
        [jnp.sum(jnp.where(iota_e == idx[k:k + 1, :], ps_ref[...], 0.0), axis=0, keepdims=True)
         for k in range(TOP_K)], axis=0)
    dest_ref[...] = (start.astype(jnp.int32) + rank_ref[...]) * ROW_SPLIT


def _dest(idx_t, rank_t, pstart):
    N = idx_t.shape[1]
    tr = TM_ROUTE
    kt = pl.BlockSpec((TOP_K, tr), lambda i: (0, i))
    return pl.pallas_call(
        _dest_kernel,
        out_shape=jax.ShapeDtypeStruct((TOP_K, N), jnp.int32),
        grid=(N // tr,),
        in_specs=[kt, kt, pl.BlockSpec((N_EXPERTS, 1), lambda i: (0, 0))],
        out_specs=kt,
        compiler_params=_params(("arbitrary",)),
        name="dest",
    )(idx_t, rank_t, pstart.astype(F32).reshape(N_EXPERTS, 1))


def _row_copy(src, dst, sem):
    return pltpu.make_async_copy(src, dst, sem)


def _token_rows(start):
    return pl.ds(pl.multiple_of(start, ROW_SPLIT), ROW_SPLIT)


def _block_rows(block):
    return pl.ds(pl.multiple_of(block * BLOCK_SUBROWS, BLOCK_SUBROWS), BLOCK_SUBROWS)


def _disp_kernel(zb_ref, dest_ref, x_ref, xs_ref, xbuf, zbuf, sem_in, sem_out):
    i = pl.program_id(0)
    n = pl.num_programs(0)
    tile_rows = xbuf.shape[1]
    tm = tile_rows // ROW_SPLIT
    slot = lax.rem(i, DISP_RING)
    ahead = lax.rem(i + 1, DISP_RING)

    def load(tile, into):
        start = pl.multiple_of(tile * tile_rows, tile_rows)
        return _row_copy(x_ref.at[pl.ds(start, tile_rows)], xbuf.at[into], sem_in.at[into])

    def drain(of):
        for k in range(TOP_K):
            _row_copy(xbuf.at[of], xs_ref.at[pl.ds(0, tile_rows)], sem_out.at[of]).wait()

    @pl.when(i == 0)
    def _():
        zbuf[...] = jnp.zeros_like(zbuf)

        def zero(j, started):
            blk = zb_ref[j]

            @pl.when(blk >= 0)
            def _():
                _row_copy(zbuf, xs_ref.at[_block_rows(blk)], sem_out.at[0]).start()

            return started + jnp.where(blk >= 0, 1, 0)

        def done(j, carry):
            _row_copy(zbuf, xs_ref.at[_block_rows(0)], sem_out.at[0]).wait()
            return carry

        started = lax.fori_loop(0, zb_ref.shape[0], zero, 0)
        lax.fori_loop(0, started, done, 0)
        load(0, 0).start()

    @pl.when(i + 1 < n)
    def _():
        @pl.when(i + 1 >= DISP_RING)
        def _():
            drain(ahead)

        load(i + 1, ahead).start()

    load(i, slot).wait()

    def issue(t, carry):
        src = xbuf.at[slot, _token_rows(t * ROW_SPLIT)]
        base = t * TOP_K
        for k in range(TOP_K):
            _row_copy(src, xs_ref.at[_token_rows(dest_ref[base + k])], sem_out.at[slot]).start(priority=k % 2)
        return carry

    lax.fori_loop(0, tm, issue, 0)

    @pl.when(i == n - 1)
    def _():
        for s in range(DISP_RING):
            drain(s)


def _dispatch(zero_blocks, dest_flat, n2p, n_rows):
    N = n2p.shape[0] // ROW_SPLIT
    tm = TM_MOVE
    assert N // tm >= DISP_RING
    grid_spec = pltpu.PrefetchScalarGridSpec(
        num_scalar_prefetch=1,
        grid=(N // tm,),
        in_specs=[
            pl.BlockSpec((tm * TOP_K,), lambda i, zb: (i,), memory_space=pltpu.SMEM),
            pl.BlockSpec(memory_space=pl.ANY),
        ],
        out_specs=pl.BlockSpec(memory_space=pl.ANY),
        scratch_shapes=[
            pltpu.VMEM((DISP_RING, tm * ROW_SPLIT, LANES), jnp.uint32),
            pltpu.VMEM((BLOCK_SUBROWS, LANES), jnp.uint32),
            pltpu.SemaphoreType.DMA((DISP_RING,)),
            pltpu.SemaphoreType.DMA((DISP_RING,)),
        ],
    )
    return pl.pallas_call(
        _disp_kernel,
        out_shape=jax.ShapeDtypeStruct((n_rows * ROW_SPLIT, LANES), jnp.uint32),
        grid_spec=grid_spec,
        compiler_params=_params(("arbitrary",)),
        name="disp",
    )(zero_blocks, dest_flat, n2p)


def _experts_kernel(eb_ref, bv_ref, xs_ref, wg_ref, wu_ref, wd_ref, ys_ref,
                    xbuf, ybuf, wgb, wub, wdb, sem_in, sem_out):
    e = pl.program_id(0)
    n_used = eb_ref[N_EXPERTS]
    n_blocks = ys_ref.shape[0] // BLOCK_SUBROWS
    first = eb_ref[e]
    end = eb_ref[e + 1]

    def in_copy(g, slot):
        return _row_copy(xs_ref.at[_block_rows(g)], xbuf.at[slot], sem_in.at[slot])

    def out_copy(g, slot):
        return _row_copy(ybuf.at[slot], ys_ref.at[_block_rows(g)], sem_out.at[slot])

    @pl.when(e == 0)
    def _():
        for g in range(RING - 1):
            @pl.when(g < n_used)
            def _(g=g):
                in_copy(g, g).start()

    @pl.when(end > first)
    def _():
        wgb[...] = wg_ref[...].astype(BF16)
        wub[...] = wu_ref[...].astype(BF16)
        wdb[...] = wd_ref[...].astype(BF16)

        def block(g, carry):
            slot = g & (RING - 1)
            in_copy(g, slot).wait()

            @pl.when(g + RING - 1 < n_used)
            def _():
                in_copy(g + RING - 1, (g + RING - 1) & (RING - 1)).start()

            @pl.when(g >= RING)
            def _():
                out_copy(g - RING, slot).wait()

            lo, hi = _unpack_halves(_load_rows(xbuf.at[slot], ROW_BLOCK))
            live = lax.broadcasted_iota(jnp.int32, lo.shape, 0) < bv_ref[g]
            lo = jnp.where(live, lo, 0.0).astype(BF16)
            hi = jnp.where(live, hi, 0.0).astype(BF16)
            gate = _dot(lo, wgb[0:HALF, :]) + _dot(hi, wgb[HALF:, :])
            up = _dot(lo, wub[0:HALF, :]) + _dot(hi, wub[HALF:, :])
            act = (gate * jax.nn.sigmoid(gate)) * up
            _store_rows(ybuf.at[slot], _pack_halves(_dot(act.astype(BF16), wdb[...])))
            out_copy(g, slot).start()
            return carry

        lax.fori_loop(first, end, block, 0)

    @pl.when(e == N_EXPERTS - 1)
    def _():
        for back in range(1, RING + 1):
            @pl.when(n_used >= back)
            def _(back=back):
                out_copy(n_used - back, (n_used - back) & (RING - 1)).wait()

        xbuf[0] = jnp.zeros((BLOCK_SUBROWS, LANES), jnp.uint32)

        def zero(g, carry):
            _row_copy(xbuf.at[0], ys_ref.at[_block_rows(g)], sem_out.at[0]).start()
            return carry

        def done(g, carry):
            _row_copy(xbuf.at[0], ys_ref.at[_block_rows(g)], sem_out.at[0]).wait()
            return carry

        lax.fori_loop(n_used, n_blocks, zero, 0)
        lax.fori_loop(n_used, n_blocks, done, 0)


def _experts(expert_block, block_valid, xs, w_gate, w_up, w_down):
    def w_map(e, eb, bv):
        return (e, 0, 0)

    grid_spec = pltpu.PrefetchScalarGridSpec(
        num_scalar_prefetch=2,
        grid=(N_EXPERTS,),
        in_specs=[
            pl.BlockSpec(memory_space=pl.ANY),
            pl.BlockSpec((None, D_MODEL, EXPERT_DIM), w_map),
            pl.BlockSpec((None, D_MODEL, EXPERT_DIM), w_map),
            pl.BlockSpec((None, EXPERT_DIM, D_MODEL), w_map),
        ],
        out_specs=pl.BlockSpec(memory_space=pl.ANY),
        scratch_shapes=[
            pltpu.VMEM((RING, BLOCK_SUBROWS, LANES), jnp.uint32),
            pltpu.VMEM((RING, BLOCK_SUBROWS, LANES), jnp.uint32),
            pltpu.VMEM((D_MODEL, EXPERT_DIM), BF16),
            pltpu.VMEM((D_MODEL, EXPERT_DIM), BF16),
            pltpu.VMEM((EXPERT_DIM, D_MODEL), BF16),
            pltpu.SemaphoreType.DMA((RING,)),
            pltpu.SemaphoreType.DMA((RING,)),
        ],
    )
    return pl.pallas_call(
        _experts_kernel,
        out_shape=jax.ShapeDtypeStruct(xs.shape, jnp.uint32),
        grid_spec=grid_spec,
        compiler_params=_params(("arbitrary",)),
        name="experts",
    )(expert_block, block_valid, xs, w_gate, w_up, w_down)


def _comb_kernel(dest_ref, next_ref, ys_ref, hs_ref, gate_ref, mod_ref, gf_ref, o_ref, buf, sem):
    i = pl.program_id(0)
    tm = hs_ref.shape[0]
    slot = i & 1

    def gather(rows_ref, into):
        def issue(t, carry):
            base = t * TOP_K
            for k in range(TOP_K):
                _row_copy(ys_ref.at[_token_rows(rows_ref[base + k])],
                          buf.at[into, k, _token_rows(t * ROW_SPLIT)], sem.at[into]).start(priority=k % 2)
            return carry

        lax.fori_loop(0, tm, issue, 0)

    @pl.when(i == 0)
    def _():
        gather(dest_ref, 0)

    for into in range(2):
        @pl.when(jnp.logical_and(i + 1 < pl.num_programs(0), slot == 1 - into))
        def _(into=into):
            gather(next_ref, into)

    for k in range(TOP_K):
        _row_copy(ys_ref.at[pl.ds(0, tm * ROW_SPLIT)], buf.at[slot, k], sem.at[slot]).wait()

    for c in range(tm // COMB_CHUNK):
        tok = slice(c * COMB_CHUNK, (c + 1) * COMB_CHUNK)
        gates = gate_ref[tok, :]
        lo = jnp.zeros((COMB_CHUNK, HALF), F32)
        hi = jnp.zeros((COMB_CHUNK, HALF), F32)
        for k in range(TOP_K):
            piece = buf.at[slot, k, pl.ds(c * COMB_CHUNK * ROW_SPLIT, COMB_CHUNK * ROW_SPLIT)]
            lo_k, hi_k = _unpack_halves(_load_rows(piece, COMB_CHUNK))
            gk = gates[:, k:k + 1]
            lo = lo + gk * lo_k
            hi = hi + gk * hi_k
        routed = jnp.concatenate([lo, hi], axis=1)
        h2 = hs_ref[tok, :] + mod_ref[0, 5:6, :] * routed
        o_ref[tok, :] = _rms(h2, gf_ref[...])


def _combine(dest_flat, ys, hs2, gates_nk, mod3, g_final, seq):
    N = hs2.shape[0]
    tm = TM_MOVE
    per_seq = seq // tm
    n_tiles = N // tm
    return pl.pallas_call(
        _comb_kernel,
        out_shape=jax.ShapeDtypeStruct((N, D_MODEL), F32),
        grid=(n_tiles,),
        in_specs=[
            pl.BlockSpec((tm * TOP_K,), lambda i: (i,), memory_space=pltpu.SMEM),
            pl.BlockSpec((tm * TOP_K,), lambda i: (jnp.minimum(i + 1, n_tiles - 1),), memory_space=pltpu.SMEM),
            pl.BlockSpec(memory_space=pl.ANY),
            pl.BlockSpec((tm, D_MODEL), lambda i: (i, 0)),
            pl.BlockSpec((tm, TOP_K), lambda i: (i, 0)),
            pl.BlockSpec((1, 6, D_MODEL), lambda i: (i // per_seq, 0, 0)),
            pl.BlockSpec((1, D_MODEL), lambda i: (0, 0)),
        ],
        out_specs=pl.BlockSpec((tm, D_MODEL), lambda i: (i, 0)),
        scratch_shapes=[pltpu.VMEM((2, TOP_K, tm * ROW_SPLIT, LANES), jnp.uint32),
                        pltpu.SemaphoreType.DMA((2,))],
        compiler_params=_params(("arbitrary",)),
        name="comb",
    )(dest_flat, dest_flat, ys, hs2, gates_nk, mod3, g_final.reshape(1, D_MODEL))


def _layer(x, mod3, g_norm_mix, w_in, sgu_ln_gain, sgu_ln_bias, w_spatial, b_spatial, g_out_attn,
           g_out_sgu, w_out, g_norm_ffn, w_router, b_router, w_exp_gate, w_exp_up, w_exp_down,
           w_sh_gate, w_sh_up, w_sh_down, g_final):
    B, S, _ = x.shape
    N = B * S

    q, k, v, u, z = _inproj(x, mod3, g_norm_mix, w_in.astype(BF16))
    head_of_lane = jnp.arange(ATTN_WIDTH) // HEAD_DIM
    slopes = jnp.exp2(-8.0 * jnp.arange(1, ATTN_HEADS + 1, dtype=F32) / ATTN_HEADS)
    slopes_lane = slopes[head_of_lane].reshape(ATTN_WIDTH // LANES, 1, LANES)
    attn = _attention(q, k, v, slopes_lane)

    hs, n2p, logits_t = _mix(
        attn, u, z, x, mod3, g_out_attn, g_out_sgu, sgu_ln_gain, sgu_ln_bias, w_spatial, b_spatial.T,
        w_out.astype(BF16), g_norm_ffn, w_router.T.astype(BF16),
        jnp.concatenate([w_sh_gate, w_sh_up], axis=1).astype(BF16), w_sh_down.astype(BF16))

    idx_t, gates_t, rank_t, counts = _route(logits_t, b_router)

    n_blocks = (N * TOP_K + N_EXPERTS * (ROW_BLOCK - 1)) // ROW_BLOCK
    cnt = counts[:, 0].astype(jnp.int32)
    padded = (cnt + ROW_BLOCK - 1) // ROW_BLOCK * ROW_BLOCK
    pends = jnp.cumsum(padded)
    pstart = pends - padded
    block_row = jnp.arange(n_blocks, dtype=jnp.int32) * ROW_BLOCK
    block_e = jnp.sum(pends[None, :] <= block_row[:, None], axis=1, dtype=jnp.int32)
    block_e = jnp.minimum(block_e, N_EXPERTS - 1)
    own = block_e[:, None] == jnp.arange(N_EXPERTS, dtype=jnp.int32)[None, :]
    live_end = jnp.sum(jnp.where(own, (pstart + cnt)[None, :], 0), axis=1)
    block_valid = jnp.clip(live_end - block_row, 0, ROW_BLOCK).astype(jnp.int32)
    n_used = (pends[-1:] // ROW_BLOCK).astype(jnp.int32)
    expert_block = jnp.concatenate([pstart // ROW_BLOCK, n_used]).astype(jnp.int32)

    last_block = jnp.where(cnt > 0, pends // ROW_BLOCK - 1, -1)
    tail_block = n_used[0] + jnp.arange(n_blocks - N * TOP_K // ROW_BLOCK)
    tail_block = jnp.where(tail_block < n_blocks, tail_block, -1)
    zero_blocks = jnp.concatenate([last_block, tail_block]).astype(jnp.int32)

    dest_flat = _dest(idx_t, rank_t, pstart).T.reshape(N * TOP_K)
    xs = _dispatch(zero_blocks, dest_flat, n2p, n_blocks * ROW_BLOCK)
    ys = _experts(expert_block, block_valid, xs, w_exp_gate, w_exp_up, w_exp_down)
    out = _combine(dest_flat, ys, hs.reshape(N, D_MODEL), gates_t.T, mod3, g_final, S)
    return out.reshape(B, S, D_MODEL)


def kernel(x, c, w_ada, b_ada, g_norm_mix, w_in, sgu_ln_gain, sgu_ln_bias, w_spatial, b_spatial, g_out_attn, g_out_sgu, w_out, g_norm_ffn, w_router, b_router, w_exp_gate, w_exp_up, w_exp_down, w_sh_gate, w_sh_up, w_sh_down, g_final):
    assert w_ada.shape[0] == 1, "single-layer stack"
    B = x.shape[0]
    mod3 = _ada(c, w_ada[0], b_ada[0]).reshape(B, 6, D_MODEL)
    return _layer(x, mod3, g_norm_mix[0], w_in[0], sgu_ln_gain[0], sgu_ln_bias[0], w_spatial[0],
                  b_spatial[0], g_out_attn[0], g_out_sgu[0], w_out[0], g_norm_ffn[0], w_router[0],
                  b_router[0], w_exp_gate[0], w_exp_up[0], w_exp_down[0], w_sh_gate[0], w_sh_up[0],
                  w_sh_down[0], g_final)
```

```python
import functools

import jax
import jax.numpy as jnp
from jax import lax
from jax.experimental import pallas as pl
from jax.experimental.pallas import tpu as pltpu

F32 = jnp.float32
BF16 = jnp.bfloat16

D_MODEL = 1024
ATTN_WIDTH = 512
ATTN_HEADS = 8
HEAD_DIM = 64
SGU_WIDTH = 512
SGU_GROUPS = 4
SGU_GROUP_DIM = 128
CHUNK = 128
DILATED_BRANCHES = ((128, 1), (512, 4), (2048, 16))
ATTN_BLOCK = 128
N_EXPERTS = 256
TOP_K = 8
N_EXPERT_GROUPS = 8
GROUP_SIZE = N_EXPERTS // N_EXPERT_GROUPS
TOPK_GROUPS = 4
EXPERT_DIM = 256
ROUTED_SCALE = 2.5
EPS = 1e-6

LANES = 128
HALF = D_MODEL // 2
ROW_BLOCK = 512
NEG_BIG = -1e30
VMEM_LIMIT = 56 * 1024 * 1024

TM_PROJ = 512
TM_ROUTE = 512
TM_MOVE = 512
DISP_RING = 3
COMB_CHUNK = 128
ATTN_UNROLL = 16
RING = 4


def _dot(a, b):
    return jnp.dot(a, b, preferred_element_type=F32)


def _dot_nt(a, b):
    return lax.dot_general(a, b, (((1,), (1,)), ((), ())), preferred_element_type=F32)


def _rms(x, g):
    return x * lax.rsqrt(jnp.mean(x * x, axis=-1, keepdims=True) + EPS) * g


def _pack_halves(x):
    return pltpu.pack_elementwise([x[:, :HALF], x[:, HALF:]], packed_dtype=BF16)


def _unpack_halves(w):
    lo = pltpu.unpack_elementwise(w, index=0, packed_dtype=BF16, unpacked_dtype=F32)
    hi = pltpu.unpack_elementwise(w, index=1, packed_dtype=BF16, unpacked_dtype=F32)
    return lo, hi


ROW_SPLIT = HALF // LANES
BLOCK_SUBROWS = ROW_BLOCK * ROW_SPLIT


def _store_rows(ref, packed):
    rows = packed.shape[0]
    for c in range(ROW_SPLIT):
        ref[pl.ds(c, rows, stride=ROW_SPLIT), :] = packed[:, c * LANES:(c + 1) * LANES]


def _load_rows(ref, rows):
    return jnp.concatenate([ref[pl.ds(c, rows, stride=ROW_SPLIT), :] for c in range(ROW_SPLIT)], axis=1)


def _params(sem=None):
    return pltpu.CompilerParams(dimension_semantics=sem, vmem_limit_bytes=VMEM_LIMIT)


def _ada_kernel(c_ref, w_ref, b_ref, o_ref):
    c = c_ref[...]
    cond = c * jax.nn.sigmoid(c)
    ch = cond.astype(BF16)
    cl = (cond - ch.astype(F32)).astype(BF16)
    w = w_ref[...]
    wh = w.astype(BF16)
    wl = (w - wh.astype(F32)).astype(BF16)
    o_ref[...] = _dot(ch, wh) + _dot(cl, wh) + _dot(ch, wl) + b_ref[...]


def _ada(c, w_ada, b_ada):
    B = c.shape[0]
    n_out = w_ada.shape[1]
    tn = D_MODEL
    return pl.pallas_call(
        _ada_kernel,
        out_shape=jax.ShapeDtypeStruct((B, n_out), F32),
        grid=(n_out // tn,),
        in_specs=[
            pl.BlockSpec((B, D_MODEL), lambda j: (0, 0)),
            pl.BlockSpec((D_MODEL, tn), lambda j: (0, j)),
            pl.BlockSpec((1, tn), lambda j: (0, j)),
        ],
        out_specs=pl.BlockSpec((B, tn), lambda j: (0, j)),
        compiler_params=_params(("arbitrary",)),
        name="ada",
    )(c, w_ada, b_ada.reshape(1, n_out))


def _inproj_kernel(x_ref, mod_ref, g_ref, w_ref, q_ref, k_ref, v_ref, u_ref, z_ref):
    x = x_ref[0]
    shift = mod_ref[0, 0:1, :]
    scale = mod_ref[0, 1:2, :]
    n = _rms(x, g_ref[...]) * (1.0 + scale) + shift
    p = _dot(n.astype(BF16), w_ref[...])
    q_ref[0] = (p[:, 0:ATTN_WIDTH] * (HEAD_DIM ** -0.5)).astype(BF16)
    k_ref[0] = p[:, ATTN_WIDTH:2 * ATTN_WIDTH].astype(BF16)
    v_ref[0] = p[:, 2 * ATTN_WIDTH:3 * ATTN_WIDTH].astype(BF16)
    u_ref[0] = p[:, 3 * ATTN_WIDTH:3 * ATTN_WIDTH + SGU_WIDTH].astype(BF16)
    z_ref[0] = p[:, 3 * ATTN_WIDTH + SGU_WIDTH:].astype(BF16)


def _inproj(x, mod3, g_norm, w_in_bf):
    B, S, _ = x.shape
    tm = TM_PROJ
    n_in = w_in_bf.shape[1]
    out = jax.ShapeDtypeStruct((B, S, ATTN_WIDTH), BF16)
    tile = pl.BlockSpec((1, tm, ATTN_WIDTH), lambda b, i: (b, i, 0))
    return pl.pallas_call(
        _inproj_kernel,
        out_shape=(out,) * 5,
        grid=(B, S // tm),
        in_specs=[
            pl.BlockSpec((1, tm, D_MODEL), lambda b, i: (b, i, 0)),
            pl.BlockSpec((1, 6, D_MODEL), lambda b, i: (b, 0, 0)),
            pl.BlockSpec((1, D_MODEL), lambda b, i: (0, 0)),
            pl.BlockSpec((D_MODEL, n_in), lambda b, i: (0, 0)),
        ],
        out_specs=(tile,) * 5,
        compiler_params=_params(("arbitrary", "arbitrary")),
        name="inproj",
    )(x, mod3, g_norm.reshape(1, D_MODEL), w_in_bf)


def _attn_kernel(slope_ref, q_ref, k_ref, v_ref, o_ref, qf, kf, vf, oacc, lacc, tbl, *, seq):
    qf[...] = q_ref[0].astype(F32)
    kf[...] = k_ref[0].astype(F32)
    vf[...] = v_ref[0].astype(F32)

    lane = lax.broadcasted_iota(jnp.int32, (1, LANES), 1)
    head0 = lane < HEAD_DIM
    slopes = slope_ref[0]
    slope_h = (slopes[:, 0:1], slopes[:, HEAD_DIM:HEAD_DIM + 1])

    whole_class = [seq // dil == 2 * ATTN_BLOCK for _, dil in DILATED_BRANCHES]
    for bi, (window, dil) in enumerate(DILATED_BRANCHES):
        steps = window // dil
        if whole_class[bi]:
            back = (lax.broadcasted_iota(jnp.int32, (2 * ATTN_BLOCK, 2 * ATTN_BLOCK), 0)
                    - lax.broadcasted_iota(jnp.int32, (2 * ATTN_BLOCK, 2 * ATTN_BLOCK), 1))
            valid = (back >= 0) & (back <= steps)
            dist = (back * dil).astype(F32)
            for hh in range(2):
                tbl[bi, hh] = jnp.where(valid, -slope_h[hh] * dist, NEG_BIG)
            continue
        qr = lax.broadcasted_iota(jnp.int32, (ATTN_BLOCK, 2 * ATTN_BLOCK), 0)
        kc = lax.broadcasted_iota(jnp.int32, (ATTN_BLOCK, 2 * ATTN_BLOCK), 1)
        for var in range(2):
            back = qr - kc + var * ATTN_BLOCK
            valid = (back >= 0) & (back <= steps)
            dist = (back * dil).astype(F32)
            for hh in range(2):
                tbl[bi, var, hh * ATTN_BLOCK:(hh + 1) * ATTN_BLOCK, :] = jnp.where(
                    valid, -slope_h[hh] * dist, NEG_BIG)

    for bi, (window, dil) in enumerate(DILATED_BRANCHES):
        cls_len = seq // dil
        nb = cls_len // ATTN_BLOCK
        nb_shift = nb.bit_length() - 1
        whole = whole_class[bi]
        qn = 2 * ATTN_BLOCK if whole else ATTN_BLOCK
        n_units = dil if whole else dil * nb
        per_step = ATTN_UNROLL * ATTN_BLOCK // qn

        def rows(start, size, dil=dil):
            if dil == 1:
                return pl.ds(pl.multiple_of(start, ATTN_BLOCK), size)
            return pl.ds(start, size, stride=dil)

        def block(it, bi=bi, dil=dil, nb=nb, nb_shift=nb_shift, rows=rows, whole=whole, qn=qn):
            if whole:
                q_rows = rows(it, qn)
                k_rows = q_rows
                table = jnp.concatenate([tbl[bi, 0], tbl[bi, 1]], axis=0)
            else:
                r = lax.shift_right_logical(it, nb_shift)
                i = it & (nb - 1)
                var = jnp.minimum(i, 1)
                q_rows = rows(i * ATTN_BLOCK * dil + r, qn)
                k_rows = rows((i - var) * ATTN_BLOCK * dil + r, 2 * ATTN_BLOCK)
                table = tbl[bi, var]
            q2 = qf[q_rows, :]
            kb = kf[k_rows, :].astype(BF16)
            v2 = vf[k_rows, :]
            qs = jnp.concatenate([jnp.where(head0, q2, 0.0), jnp.where(head0, 0.0, q2)], axis=0)
            s = _dot_nt(qs.astype(BF16), kb) + table
            m = jnp.max(s, axis=-1, keepdims=True)
            p = jnp.exp(s - m)
            den = jnp.sum(p, axis=-1, keepdims=True)
            pb = p.astype(BF16)
            vs = jnp.concatenate([jnp.where(head0, v2, 0.0), jnp.where(head0, 0.0, v2)], axis=0)
            o = _dot(jnp.concatenate([pb[:qn], pb[qn:]], axis=1), vs.astype(BF16))
            den2 = jnp.where(head0, den[:qn], den[qn:])
            lse = jnp.where(head0, m[:qn], m[qn:]) + jnp.log(den2)
            return q_rows, o / den2, lse

        def body(step, carry, bi=bi, block=block, per_step=per_step):
            done = [block(step * per_step + j) for j in range(per_step)]
            if bi > 0:
                merged = []
                for q_rows, o, lse in done:
                    l_old = lacc[q_rows, :]
                    m2 = jnp.maximum(l_old, lse)
                    a = jnp.exp(l_old - m2)
                    b = jnp.exp(lse - m2)
                    tot = a + b
                    merged.append((q_rows, (oacc[q_rows, :] * a + o * b) / tot, m2 + jnp.log(tot)))
                done = merged
            for q_rows, o, lse in done:
                oacc[q_rows, :] = o
                lacc[q_rows, :] = lse
            return carry

        assert n_units % per_step == 0
        lax.fori_loop(0, n_units // per_step, body, 0)

    o_ref[0] = oacc[...].astype(BF16)


def _attention(q, k, v, slopes_lane):
    B, S, _ = q.shape
    for window, dil in DILATED_BRANCHES:
        assert window // dil <= ATTN_BLOCK
        cls_len = S // dil
        assert S % dil == 0 and cls_len % ATTN_BLOCK == 0 and cls_len >= 2 * ATTN_BLOCK
        assert (cls_len // ATTN_BLOCK) & (cls_len // ATTN_BLOCK - 1) == 0
    n_pairs = ATTN_WIDTH // LANES
    tile = pl.BlockSpec((1, S, LANES), lambda b, p: (b, 0, p))
    return pl.pallas_call(
        functools.partial(_attn_kernel, seq=S),
        out_shape=jax.ShapeDtypeStruct((B, S, ATTN_WIDTH), BF16),
        grid=(B, n_pairs),
        in_specs=[pl.BlockSpec((1, 1, LANES), lambda b, p: (p, 0, 0)), tile, tile, tile],
        out_specs=tile,
        scratch_shapes=[pltpu.VMEM((S, LANES), F32)] * 5
        + [pltpu.VMEM((len(DILATED_BRANCHES), 2, 2 * ATTN_BLOCK, 2 * ATTN_BLOCK), F32)],
        compiler_params=_params(("arbitrary", "arbitrary")),
        name="attn",
    )(slopes_lane, q, k, v)


def _mix_kernel(attn_ref, u_ref, z_ref, x_ref, mod_ref, ga_ref, gs_ref, lng_ref, lnb_ref, wsp_ref,
                bsp_ref, wout_ref, gffn_ref, wrt_ref, wsgu_ref, wsd_ref,
                hs_ref, n2p_ref, lg_ref):
    tm = x_ref.shape[1]
    nc = tm // CHUNK
    a_n = _rms(attn_ref[0].astype(F32), ga_ref[...])

    ug = jax.nn.gelu(u_ref[0].astype(F32))
    zg = jax.nn.gelu(z_ref[0].astype(F32))
    mu = jnp.mean(zg, axis=-1, keepdims=True)
    zc = zg - mu
    var = jnp.mean(zc * zc, axis=-1, keepdims=True)
    zb = (zc * lax.rsqrt(var + EPS) * lng_ref[...] + lnb_ref[...]).astype(BF16)

    row = lax.broadcasted_iota(jnp.int32, (CHUNK, CHUNK), 0)
    col = lax.broadcasted_iota(jnp.int32, (CHUNK, CHUNK), 1)
    per_group = []
    for g in range(SGU_GROUPS):
        wc = jnp.where(row >= col, wsp_ref[g], 0.0).astype(BF16)
        lanes = slice(g * SGU_GROUP_DIM, (g + 1) * SGU_GROUP_DIM)
        zcat = jnp.concatenate([zb[c * CHUNK:(c + 1) * CHUNK, lanes] for c in range(nc)], axis=1)
        per_group.append(_dot(wc, zcat) + bsp_ref[:, g:g + 1])
    mixed = jnp.concatenate(
        [jnp.concatenate([per_group[g][:, c * CHUNK:(c + 1) * CHUNK] for g in range(SGU_GROUPS)], axis=1)
         for c in range(nc)], axis=0)
    s_n = _rms(ug * mixed, gs_ref[...])

    mix = (_dot(a_n.astype(BF16), wout_ref[0:ATTN_WIDTH, :])
           + _dot(s_n.astype(BF16), wout_ref[ATTN_WIDTH:, :]))
    gate1 = mod_ref[0, 2:3, :]
    shift2 = mod_ref[0, 3:4, :]
    scale2 = mod_ref[0, 4:5, :]
    gate2 = mod_ref[0, 5:6, :]
    h1 = x_ref[0] + gate1 * mix
    n2 = _rms(h1, gffn_ref[...]) * (1.0 + scale2) + shift2
    n2b = n2.astype(BF16)

    lg_ref[...] = _dot_nt(wrt_ref[...], n2b)
    gu = _dot(n2b, wsgu_ref[...])
    gsh = gu[:, :EXPERT_DIM]
    act = (gsh * jax.nn.sigmoid(gsh)) * gu[:, EXPERT_DIM:]
    shared = _dot(act.astype(BF16), wsd_ref[...])
    hs_ref[0] = h1 + gate2 * shared
    _store_rows(n2p_ref, _pack_halves(n2))


def _mix(attn, u, z, x, mod3, g_out_attn, g_out_sgu, ln_g, ln_b, w_spatial, b_spatial_t, w_out_bf,
         g_norm_ffn, w_router_t_bf, w_sh_gu_bf, w_sh_d_bf):
    B, S, _ = x.shape
    tm = TM_PROJ
    nt = S // tm
    N = B * S
    half_tile = pl.BlockSpec((1, tm, ATTN_WIDTH), lambda b, i: (b, i, 0))
    full_tile = pl.BlockSpec((1, tm, D_MODEL), lambda b, i: (b, i, 0))

    def const(shape):
        return pl.BlockSpec(shape, lambda b, i: (0,) * len(shape))

    return pl.pallas_call(
        _mix_kernel,
        out_shape=(
            jax.ShapeDtypeStruct((B, S, D_MODEL), F32),
            jax.ShapeDtypeStruct((N * ROW_SPLIT, LANES), jnp.uint32),
            jax.ShapeDtypeStruct((N_EXPERTS, N), F32),
        ),
        grid=(B, nt),
        in_specs=[
            half_tile, half_tile, half_tile, full_tile,
            pl.BlockSpec((1, 6, D_MODEL), lambda b, i: (b, 0, 0)),
            const((1, ATTN_WIDTH)), const((1, SGU_WIDTH)), const((1, SGU_WIDTH)), const((1, SGU_WIDTH)),
            const((SGU_GROUPS, CHUNK, CHUNK)), const((CHUNK, SGU_GROUPS)),
            const((ATTN_WIDTH + SGU_WIDTH, D_MODEL)), const((1, D_MODEL)),
            const((N_EXPERTS, D_MODEL)), const((D_MODEL, 2 * EXPERT_DIM)), const((EXPERT_DIM, D_MODEL)),
        ],
        out_specs=(
            full_tile,
            pl.BlockSpec((tm * ROW_SPLIT, LANES), lambda b, i: (b * nt + i, 0)),
            pl.BlockSpec((N_EXPERTS, tm), lambda b, i: (0, b * nt + i)),
        ),
        compiler_params=_params(("arbitrary", "arbitrary")),
        name="mix",
    )(attn, u, z, x, mod3, g_out_attn.reshape(1, -1), g_out_sgu.reshape(1, -1), ln_g.reshape(1, -1),
      ln_b.reshape(1, -1), w_spatial, b_spatial_t, w_out_bf, g_norm_ffn.reshape(1, -1), w_router_t_bf,
      w_sh_gu_bf, w_sh_d_bf)


def _first_max(v, iota, size):
    mx = jnp.max(v, axis=0, keepdims=True)
    am = jnp.min(jnp.where(v == mx, iota, size), axis=0, keepdims=True)
    return mx, am


def _route_kernel(lg_ref, br_ref, idx_ref, gate_ref, rank_ref, cnt_ref, carry, tri):
    step = pl.program_id(0)
    tr = lg_ref.shape[1]

    @pl.when(step == 0)
    def _():
        carry[...] = jnp.zeros_like(carry)
        before = (lax.broadcasted_iota(jnp.int32, (tr, tr), 0)
                  < lax.broadcasted_iota(jnp.int32, (tr, tr), 1))
        tri[...] = jnp.where(before, 1.0, 0.0).astype(BF16)

    scores = jax.nn.sigmoid(lg_ref[...])
    choice = scores + br_ref[...]

    iota_g = lax.broadcasted_iota(jnp.int32, (GROUP_SIZE, tr), 0)
    gs = []
    for g in range(N_EXPERT_GROUPS):
        cg = choice[g * GROUP_SIZE:(g + 1) * GROUP_SIZE, :]
        m1, am = _first_max(cg, iota_g, GROUP_SIZE)
        m2 = jnp.max(jnp.where(iota_g == am, -jnp.inf, cg), axis=0, keepdims=True)
        gs.append(m1 + m2)
    gscore = jnp.concatenate(gs, axis=0)

    iota_n = lax.broadcasted_iota(jnp.int32, (N_EXPERT_GROUPS, tr), 0)
    t = gscore
    for _ in range(TOPK_GROUPS - 1):
        _, am = _first_max(t, iota_n, N_EXPERT_GROUPS)
        t = jnp.where(iota_n == am, -jnp.inf, t)
    kth = jnp.max(t, axis=0, keepdims=True)
    keep = gscore >= kth

    v = jnp.concatenate(
        [jnp.where(keep[g:g + 1, :], choice[g * GROUP_SIZE:(g + 1) * GROUP_SIZE, :], -jnp.inf)
         for g in range(N_EXPERT_GROUPS)], axis=0)
    iota_e = lax.broadcasted_iota(jnp.int32, (N_EXPERTS, tr), 0)
    idxs, sels = [], []
    chosen = jnp.zeros((N_EXPERTS, tr), F32)
    for _ in range(TOP_K):
        _, am = _first_max(v, iota_e, N_EXPERTS)
        hit = iota_e == am
        idxs.append(am)
        sels.append(jnp.sum(jnp.where(hit, scores, 0.0), axis=0, keepdims=True))
        chosen = jnp.where(hit, 1.0, chosen)
        v = jnp.where(hit, -jnp.inf, v)
    sel = jnp.concatenate(sels, axis=0)
    idx_ref[...] = jnp.concatenate(idxs, axis=0)
    gate_ref[...] = sel / jnp.sum(sel, axis=0, keepdims=True) * ROUTED_SCALE

    earlier = _dot(chosen.astype(BF16), tri[...]) + carry[...]
    rank_ref[...] = jnp.concatenate(
        [jnp.sum(jnp.where(iota_e == am, earlier, 0.0), axis=0, keepdims=True) for am in idxs],
        axis=0).astype(jnp.int32)
    carry[...] = carry[...] + jnp.sum(chosen, axis=1, keepdims=True)
    cnt_ref[...] = jnp.broadcast_to(carry[...], cnt_ref.shape)


def _route(logits_t, b_router):
    N = logits_t.shape[1]
    tr = TM_ROUTE
    kt = pl.BlockSpec((TOP_K, tr), lambda i: (0, i))
    return pl.pallas_call(
        _route_kernel,
        out_shape=(
            jax.ShapeDtypeStruct((TOP_K, N), jnp.int32),
            jax.ShapeDtypeStruct((TOP_K, N), F32),
            jax.ShapeDtypeStruct((TOP_K, N), jnp.int32),
            jax.ShapeDtypeStruct((N_EXPERTS, LANES), F32),
        ),
        grid=(N // tr,),
        in_specs=[
            pl.BlockSpec((N_EXPERTS, tr), lambda i: (0, i)),
            pl.BlockSpec((N_EXPERTS, 1), lambda i: (0, 0)),
        ],
        out_specs=(kt, kt, kt, pl.BlockSpec((N_EXPERTS, LANES), lambda i: (0, 0))),
        scratch_shapes=[pltpu.VMEM((N_EXPERTS, 1), F32), pltpu.VMEM((tr, tr), BF16)],
        compiler_params=_params(("arbitrary",)),
        name="route",
    )(logits_t, b_router.reshape(N_EXPERTS, 1))


def _dest_kernel(idx_ref, rank_ref, ps_ref, dest_ref):
    tr = idx_ref.shape[1]
    iota_e = lax.broadcasted_iota(jnp.int32, (N_EXPERTS, tr), 0)
    idx = idx_ref[...]
    start = jnp.concatenate(
        [jnp.sum(jnp.where(iota_e == idx[k:k + 1, :], ps_ref[...], 0.0), axis=0, keepdims=True)
         for k in range(TOP_K)], axis=0)
    dest_ref[...] = (start.astype(jnp.int32) + rank_ref[...]) * ROW_SPLIT


def _dest(idx_t, rank_t, pstart):
    N = idx_t.shape[1]
    tr = TM_ROUTE
    kt = pl.BlockSpec((TOP_K, tr), lambda i: (0, i))
    return pl.pallas_call(
        _dest_kernel,
        out_shape=jax.ShapeDtypeStruct((TOP_K, N), jnp.int32),
        grid=(N // tr,),
        in_specs=[kt, kt, pl.BlockSpec((N_EXPERTS, 1), lambda i: (0, 0))],
        out_specs=kt,
        compiler_params=_params(("arbitrary",)),
        name="dest",
    )(idx_t, rank_t, pstart.astype(F32).reshape(N_EXPERTS, 1))


def _row_copy(src, dst, sem):
    return pltpu.make_async_copy(src, dst, sem)


def _token_rows(start):
    return pl.ds(pl.multiple_of(start, ROW_SPLIT), ROW_SPLIT)


def _block_rows(block):
    return pl.ds(pl.multiple_of(block * BLOCK_SUBROWS, BLOCK_SUBROWS), BLOCK_SUBROWS)


def _disp_kernel(zb_ref, dest_ref, x_ref, xs_ref, xbuf, zbuf, sem_in, sem_out):
    i = pl.program_id(0)
    n = pl.num_programs(0)
    tile_rows = xbuf.shape[1]
    tm = tile_rows // ROW_SPLIT
    slot = lax.rem(i, DISP_RING)
    ahead = lax.rem(i + 1, DISP_RING)

    def load(tile, into):
        start = pl.multiple_of(tile * tile_rows, tile_rows)
        return _row_copy(x_ref.at[pl.ds(start, tile_rows)], xbuf.at[into], sem_in.at[into])

    def drain(of):
        for k in range(TOP_K):
            _row_copy(xbuf.at[of], xs_ref.at[pl.ds(0, tile_rows)], sem_out.at[of]).wait()

    @pl.when(i == 0)
    def _():
        zbuf[...] = jnp.zeros_like(zbuf)

        def zero(j, started):
            blk = zb_ref[j]

            @pl.when(blk >= 0)
            def _():
                _row_copy(zbuf, xs_ref.at[_block_rows(blk)], sem_out.at[0]).start()

            return started + jnp.where(blk >= 0, 1, 0)

        def done(j, carry):
            _row_copy(zbuf, xs_ref.at[_block_rows(0)], sem_out.at[0]).wait()
            return carry

        started = lax.fori_loop(0, zb_ref.shape[0], zero, 0)
        lax.fori_loop(0, started, done, 0)
        load(0, 0).start()

    @pl.when(i + 1 < n)
    def _():
        @pl.when(i + 1 >= DISP_RING)
        def _():
            drain(ahead)

        load(i + 1, ahead).start()

    load(i, slot).wait()

    def issue(t, carry):
        src = x_ref.at[_token_rows((i * tm + t) * ROW_SPLIT)]
        base = t * TOP_K
        for k in range(TOP_K):
            _row_copy(src, xs_ref.at[_token_rows(dest_ref[base + k])], sem_out.at[slot]).start(priority=k % 2)
        return carry

    lax.fori_loop(0, tm, issue, 0)

    @pl.when(i == n - 1)
    def _():
        for s in range(DISP_RING):
            drain(s)


def _dispatch(zero_blocks, dest_flat, n2p, n_rows):
    N = n2p.shape[0] // ROW_SPLIT
    tm = TM_MOVE
    assert N // tm >= DISP_RING
    grid_spec = pltpu.PrefetchScalarGridSpec(
        num_scalar_prefetch=1,
        grid=(N // tm,),
        in_specs=[
            pl.BlockSpec((tm * TOP_K,), lambda i, zb: (i,), memory_space=pltpu.SMEM),
            pl.BlockSpec(memory_space=pl.ANY),
        ],
        out_specs=pl.BlockSpec(memory_space=pl.ANY),
        scratch_shapes=[
            pltpu.VMEM((DISP_RING, tm * ROW_SPLIT, LANES), jnp.uint32),
            pltpu.VMEM((BLOCK_SUBROWS, LANES), jnp.uint32),
            pltpu.SemaphoreType.DMA((DISP_RING,)),
            pltpu.SemaphoreType.DMA((DISP_RING,)),
        ],
    )
    return pl.pallas_call(
        _disp_kernel,
        out_shape=jax.ShapeDtypeStruct((n_rows * ROW_SPLIT, LANES), jnp.uint32),
        grid_spec=grid_spec,
        compiler_params=_params(("arbitrary",)),
        name="disp",
    )(zero_blocks, dest_flat, n2p)


def _experts_kernel(eb_ref, bv_ref, xs_ref, wg_ref, wu_ref, wd_ref, ys_ref,
                    xbuf, ybuf, wgb, wub, wdb, sem_in, sem_out):
    e = pl.program_id(0)
    n_used = eb_ref[N_EXPERTS]
    n_blocks = ys_ref.shape[0] // BLOCK_SUBROWS
    first = eb_ref[e]
    end = eb_ref[e + 1]

    def in_copy(g, slot):
        return _row_copy(xs_ref.at[_block_rows(g)], xbuf.at[slot], sem_in.at[slot])

    def out_copy(g, slot):
        return _row_copy(ybuf.at[slot], ys_ref.at[_block_rows(g)], sem_out.at[slot])

    @pl.when(e == 0)
    def _():
        for g in range(RING - 1):
            @pl.when(g < n_used)
            def _(g=g):
                in_copy(g, g).start()

    @pl.when(end > first)
    def _():
        wgb[...] = wg_ref[...].astype(BF16)
        wub[...] = wu_ref[...].astype(BF16)
        wdb[...] = wd_ref[...].astype(BF16)

        def block(g, carry):
            slot = g & (RING - 1)
            in_copy(g, slot).wait()

            @pl.when(g + RING - 1 < n_used)
            def _():
                in_copy(g + RING - 1, (g + RING - 1) & (RING - 1)).start()

            @pl.when(g >= RING)
            def _():
                out_copy(g - RING, slot).wait()

            lo, hi = _unpack_halves(_load_rows(xbuf.at[slot], ROW_BLOCK))
            live = lax.broadcasted_iota(jnp.int32, lo.shape, 0) < bv_ref[g]
            lo = jnp.where(live, lo, 0.0).astype(BF16)
            hi = jnp.where(live, hi, 0.0).astype(BF16)
            gate = _dot(lo, wgb[0:HALF, :]) + _dot(hi, wgb[HALF:, :])
            up = _dot(lo, wub[0:HALF, :]) + _dot(hi, wub[HALF:, :])
            act = (gate * jax.nn.sigmoid(gate)) * up
            _store_rows(ybuf.at[slot], _pack_halves(_dot(act.astype(BF16), wdb[...])))
            out_copy(g, slot).start()
            return carry

        lax.fori_loop(first, end, block, 0)

    @pl.when(e == N_EXPERTS - 1)
    def _():
        for back in range(1, RING + 1):
            @pl.when(n_used >= back)
            def _(back=back):
                out_copy(n_used - back, (n_used - back) & (RING - 1)).wait()

        xbuf[0] = jnp.zeros((BLOCK_SUBROWS, LANES), jnp.uint32)

        def zero(g, carry):
            _row_copy(xbuf.at[0], ys_ref.at[_block_rows(g)], sem_out.at[0]).start()
            return carry

        def done(g, carry):
            _row_copy(xbuf.at[0], ys_ref.at[_block_rows(g)], sem_out.at[0]).wait()
            return carry

        lax.fori_loop(n_used, n_blocks, zero, 0)
        lax.fori_loop(n_used, n_blocks, done, 0)


def _experts(expert_block, block_valid, xs, w_gate, w_up, w_down):
    def w_map(e, eb, bv):
        return (e, 0, 0)

    grid_spec = pltpu.PrefetchScalarGridSpec(
        num_scalar_prefetch=2,
        grid=(N_EXPERTS,),
        in_specs=[
            pl.BlockSpec(memory_space=pl.ANY),
            pl.BlockSpec((None, D_MODEL, EXPERT_DIM), w_map),
            pl.BlockSpec((None, D_MODEL, EXPERT_DIM), w_map),
            pl.BlockSpec((None, EXPERT_DIM, D_MODEL), w_map),
        ],
        out_specs=pl.BlockSpec(memory_space=pl.ANY),
        scratch_shapes=[
            pltpu.VMEM((RING, BLOCK_SUBROWS, LANES), jnp.uint32),
            pltpu.VMEM((RING, BLOCK_SUBROWS, LANES), jnp.uint32),
            pltpu.VMEM((D_MODEL, EXPERT_DIM), BF16),
            pltpu.VMEM((D_MODEL, EXPERT_DIM), BF16),
            pltpu.VMEM((EXPERT_DIM, D_MODEL), BF16),
            pltpu.SemaphoreType.DMA((RING,)),
            pltpu.SemaphoreType.DMA((RING,)),
        ],
    )
    return pl.pallas_call(
        _experts_kernel,
        out_shape=jax.ShapeDtypeStruct(xs.shape, jnp.uint32),
        grid_spec=grid_spec,
        compiler_params=_params(("arbitrary",)),
        name="experts",
    )(expert_block, block_valid, xs, w_gate, w_up, w_down)


def _comb_kernel(dest_ref, next_ref, ys_ref, hs_ref, gate_ref, mod_ref, gf_ref, o_ref, buf, sem):
    i = pl.program_id(0)
    tm = hs_ref.shape[0]
    slot = i & 1

    def gather(rows_ref, into):
        def issue(t, carry):
            base = t * TOP_K
            for k in range(TOP_K):
                _row_copy(ys_ref.at[_token_rows(rows_ref[base + k])],
                          buf.at[into, k, _token_rows(t * ROW_SPLIT)], sem.at[into]).start(priority=k % 2)
            return carry

        lax.fori_loop(0, tm, issue, 0)

    @pl.when(i == 0)
    def _():
        gather(dest_ref, 0)

    for into in range(2):
        @pl.when(jnp.logical_and(i + 1 < pl.num_programs(0), slot == 1 - into))
        def _(into=into):
            gather(next_ref, into)

    for k in range(TOP_K):
        _row_copy(ys_ref.at[pl.ds(0, tm * ROW_SPLIT)], buf.at[slot, k], sem.at[slot]).wait()

    for c in range(tm // COMB_CHUNK):
        tok = slice(c * COMB_CHUNK, (c + 1) * COMB_CHUNK)
        gates = gate_ref[tok, :]
        lo = jnp.zeros((COMB_CHUNK, HALF), F32)
        hi = jnp.zeros((COMB_CHUNK, HALF), F32)
        for k in range(TOP_K):
            piece = buf.at[slot, k, pl.ds(c * COMB_CHUNK * ROW_SPLIT, COMB_CHUNK * ROW_SPLIT)]
            lo_k, hi_k = _unpack_halves(_load_rows(piece, COMB_CHUNK))
            gk = gates[:, k:k + 1]
            lo = lo + gk * lo_k
            hi = hi + gk * hi_k
        routed = jnp.concatenate([lo, hi], axis=1)
        h2 = hs_ref[tok, :] + mod_ref[0, 5:6, :] * routed
        o_ref[tok, :] = _rms(h2, gf_ref[...])


def _combine(dest_flat, ys, hs2, gates_nk, mod3, g_final, seq):
    N = hs2.shape[0]
    tm = TM_MOVE
    per_seq = seq // tm
    n_tiles = N // tm
    return pl.pallas_call(
        _comb_kernel,
        out_shape=jax.ShapeDtypeStruct((N, D_MODEL), F32),
        grid=(n_tiles,),
        in_specs=[
            pl.BlockSpec((tm * TOP_K,), lambda i: (i,), memory_space=pltpu.SMEM),
            pl.BlockSpec((tm * TOP_K,), lambda i: (jnp.minimum(i + 1, n_tiles - 1),), memory_space=pltpu.SMEM),
            pl.BlockSpec(memory_space=pl.ANY),
            pl.BlockSpec((tm, D_MODEL), lambda i: (i, 0)),
            pl.BlockSpec((tm, TOP_K), lambda i: (i, 0)),
            pl.BlockSpec((1, 6, D_MODEL), lambda i: (i // per_seq, 0, 0)),
            pl.BlockSpec((1, D_MODEL), lambda i: (0, 0)),
        ],
        out_specs=pl.BlockSpec((tm, D_MODEL), lambda i: (i, 0)),
        scratch_shapes=[pltpu.VMEM((2, TOP_K, tm * ROW_SPLIT, LANES), jnp.uint32),
                        pltpu.SemaphoreType.DMA((2,))],
        compiler_params=_params(("arbitrary",)),
        name="comb",
    )(dest_flat, dest_flat, ys, hs2, gates_nk, mod3, g_final.reshape(1, D_MODEL))


def _layer(x, mod3, g_norm_mix, w_in, sgu_ln_gain, sgu_ln_bias, w_spatial, b_spatial, g_out_attn,
           g_out_sgu, w_out, g_norm_ffn, w_router, b_router, w_exp_gate, w_exp_up, w_exp_down,
           w_sh_gate, w_sh_up, w_sh_down, g_final):
    B, S, _ = x.shape
    N = B * S

    q, k, v, u, z = _inproj(x, mod3, g_norm_mix, w_in.astype(BF16))
    head_of_lane = jnp.arange(ATTN_WIDTH) // HEAD_DIM
    slopes = jnp.exp2(-8.0 * jnp.arange(1, ATTN_HEADS + 1, dtype=F32) / ATTN_HEADS)
    slopes_lane = slopes[head_of_lane].reshape(ATTN_WIDTH // LANES, 1, LANES)
    attn = _attention(q, k, v, slopes_lane)

    hs, n2p, logits_t = _mix(
        attn, u, z, x, mod3, g_out_attn, g_out_sgu, sgu_ln_gain, sgu_ln_bias, w_spatial, b_spatial.T,
        w_out.astype(BF16), g_norm_ffn, w_router.T.astype(BF16),
        jnp.concatenate([w_sh_gate, w_sh_up], axis=1).astype(BF16), w_sh_down.astype(BF16))

    idx_t, gates_t, rank_t, counts = _route(logits_t, b_router)

    n_blocks = (N * TOP_K + N_EXPERTS * (ROW_BLOCK - 1)) // ROW_BLOCK
    cnt = counts[:, 0].astype(jnp.int32)
    padded = (cnt + ROW_BLOCK - 1) // ROW_BLOCK * ROW_BLOCK
    pends = jnp.cumsum(padded)
    pstart = pends - padded
    block_row = jnp.arange(n_blocks, dtype=jnp.int32) * ROW_BLOCK
    block_e = jnp.sum(pends[None, :] <= block_row[:, None], axis=1, dtype=jnp.int32)
    block_e = jnp.minimum(block_e, N_EXPERTS - 1)
    own = block_e[:, None] == jnp.arange(N_EXPERTS, dtype=jnp.int32)[None, :]
    live_end = jnp.sum(jnp.where(own, (pstart + cnt)[None, :], 0), axis=1)
    block_valid = jnp.clip(live_end - block_row, 0, ROW_BLOCK).astype(jnp.int32)
    n_used = (pends[-1:] // ROW_BLOCK).astype(jnp.int32)
    expert_block = jnp.concatenate([pstart // ROW_BLOCK, n_used]).astype(jnp.int32)

    last_block = jnp.where(cnt > 0, pends // ROW_BLOCK - 1, -1)
    tail_block = n_used[0] + jnp.arange(n_blocks - N * TOP_K // ROW_BLOCK)
    tail_block = jnp.where(tail_block < n_blocks, tail_block, -1)
    zero_blocks = jnp.concatenate([last_block, tail_block]).astype(jnp.int32)

    dest_flat = _dest(idx_t, rank_t, pstart).T.reshape(N * TOP_K)
    xs = _dispatch(zero_blocks, dest_flat, n2p, n_blocks * ROW_BLOCK)
    ys = _experts(expert_block, block_valid, xs, w_exp_gate, w_exp_up, w_exp_down)
    out = _combine(dest_flat, ys, hs.reshape(N, D_MODEL), gates_t.T, mod3, g_final, S)
    return out.reshape(B, S, D_MODEL)


def kernel(x, c, w_ada, b_ada, g_norm_mix, w_in, sgu_ln_gain, sgu_ln_bias, w_spatial, b_spatial, g_out_attn, g_out_sgu, w_out, g_norm_ffn, w_router, b_router, w_exp_gate, w_exp_up, w_exp_down, w_sh_gate, w_sh_up, w_sh_down, g_final):
    assert w_ada.shape[0] == 1, "single-layer stack"
    B = x.shape[0]
    mod3 = _ada(c, w_ada[0], b_ada[0]).reshape(B, 6, D_MODEL)
    return _layer(x, mod3, g_norm_mix[0], w_in[0], sgu_ln_gain[0], sgu_ln_bias[0], w_spatial[0],
                  b_spatial[0], g_out_attn[0], g_out_sgu[0], w_out[0], g_norm_ffn[0], w_router[0],
                  b_router[0], w_exp_gate[0], w_exp_up[0], w_exp_down[0], w_sh_gate[0], w_sh_up[0],
                  w_sh_down[0], g_final)
```

```python
import functools

import jax
import jax.numpy as jnp
from jax import lax
from jax.experimental import pallas as pl
from jax.experimental.pallas import tpu as pltpu

F32 = jnp.float32
BF16 = jnp.bfloat16

D_MODEL = 1024
ATTN_WIDTH = 512
ATTN_HEADS = 8
HEAD_DIM = 64
SGU_WIDTH = 512
SGU_GROUPS = 4
SGU_GROUP_DIM = 128
CHUNK = 128
DILATED_BRANCHES = ((128, 1), (512, 4), (2048, 16))
ATTN_BLOCK = 128
N_EXPERTS = 256
TOP_K = 8
N_EXPERT_GROUPS = 8
GROUP_SIZE = N_EXPERTS // N_EXPERT_GROUPS
TOPK_GROUPS = 4
EXPERT_DIM = 256
ROUTED_SCALE = 2.5
EPS = 1e-6

LANES = 128
HALF = D_MODEL // 2
ROW_BLOCK = 512
NEG_BIG = -1e30
VMEM_LIMIT = 56 * 1024 * 1024

TM_PROJ = 512
TM_ROUTE = 512
TM_MOVE = 512
DISP_RING = 3
COMB_CHUNK = 128
ATTN_UNROLL = 16
RING = 4


def _dot(a, b):
    return jnp.dot(a, b, preferred_element_type=F32)


def _dot_nt(a, b):
    return lax.dot_general(a, b, (((1,), (1,)), ((), ())), preferred_element_type=F32)


def _rms(x, g):
    return x * lax.rsqrt(jnp.mean(x * x, axis=-1, keepdims=True) + EPS) * g


def _pack_halves(x):
    return pltpu.pack_elementwise([x[:, :HALF], x[:, HALF:]], packed_dtype=BF16)


def _unpack_halves(w):
    lo = pltpu.unpack_elementwise(w, index=0, packed_dtype=BF16, unpacked_dtype=F32)
    hi = pltpu.unpack_elementwise(w, index=1, packed_dtype=BF16, unpacked_dtype=F32)
    return lo, hi


ROW_SPLIT = HALF // LANES
BLOCK_SUBROWS = ROW_BLOCK * ROW_SPLIT


def _store_rows(ref, packed):
    rows = packed.shape[0]
    for c in range(ROW_SPLIT):
        ref[pl.ds(c, rows, stride=ROW_SPLIT), :] = packed[:, c * LANES:(c + 1) * LANES]


def _load_rows(ref, rows):
    return jnp.concatenate([ref[pl.ds(c, rows, stride=ROW_SPLIT), :] for c in range(ROW_SPLIT)], axis=1)


def _params(sem=None):
    return pltpu.CompilerParams(dimension_semantics=sem, vmem_limit_bytes=VMEM_LIMIT)


def _ada_kernel(c_ref, w_ref, b_ref, o_ref):
    c = c_ref[...]
    cond = c * jax.nn.sigmoid(c)
    ch = cond.astype(BF16)
    cl = (cond - ch.astype(F32)).astype(BF16)
    w = w_ref[...]
    wh = w.astype(BF16)
    wl = (w - wh.astype(F32)).astype(BF16)
    o_ref[...] = _dot(ch, wh) + _dot(cl, wh) + _dot(ch, wl) + b_ref[...]


def _ada(c, w_ada, b_ada):
    B = c.shape[0]
    n_out = w_ada.shape[1]
    tn = D_MODEL
    return pl.pallas_call(
        _ada_kernel,
        out_shape=jax.ShapeDtypeStruct((B, n_out), F32),
        grid=(n_out // tn,),
        in_specs=[
            pl.BlockSpec((B, D_MODEL), lambda j: (0, 0)),
            pl.BlockSpec((D_MODEL, tn), lambda j: (0, j)),
            pl.BlockSpec((1, tn), lambda j: (0, j)),
        ],
        out_specs=pl.BlockSpec((B, tn), lambda j: (0, j)),
        compiler_params=_params(("arbitrary",)),
        name="ada",
    )(c, w_ada, b_ada.reshape(1, n_out))


def _inproj_kernel(x_ref, mod_ref, g_ref, w_ref, q_ref, k_ref, v_ref, u_ref, z_ref):
    x = x_ref[0]
    shift = mod_ref[0, 0:1, :]
    scale = mod_ref[0, 1:2, :]
    n = _rms(x, g_ref[...]) * (1.0 + scale) + shift
    p = _dot(n.astype(BF16), w_ref[...])
    q_ref[0] = (p[:, 0:ATTN_WIDTH] * (HEAD_DIM ** -0.5)).astype(BF16)
    k_ref[0] = p[:, ATTN_WIDTH:2 * ATTN_WIDTH].astype(BF16)
    v_ref[0] = p[:, 2 * ATTN_WIDTH:3 * ATTN_WIDTH].astype(BF16)
    u_ref[0] = p[:, 3 * ATTN_WIDTH:3 * ATTN_WIDTH + SGU_WIDTH].astype(BF16)
    z_ref[0] = p[:, 3 * ATTN_WIDTH + SGU_WIDTH:].astype(BF16)


def _inproj(x, mod3, g_norm, w_in_bf):
    B, S, _ = x.shape
    tm = TM_PROJ
    n_in = w_in_bf.shape[1]
    out = jax.ShapeDtypeStruct((B, S, ATTN_WIDTH), BF16)
    tile = pl.BlockSpec((1, tm, ATTN_WIDTH), lambda b, i: (b, i, 0))
    return pl.pallas_call(
        _inproj_kernel,
        out_shape=(out,) * 5,
        grid=(B, S // tm),
        in_specs=[
            pl.BlockSpec((1, tm, D_MODEL), lambda b, i: (b, i, 0)),
            pl.BlockSpec((1, 6, D_MODEL), lambda b, i: (b, 0, 0)),
            pl.BlockSpec((1, D_MODEL), lambda b, i: (0, 0)),
            pl.BlockSpec((D_MODEL, n_in), lambda b, i: (0, 0)),
        ],
        out_specs=(tile,) * 5,
        compiler_params=_params(("arbitrary", "arbitrary")),
        name="inproj",
    )(x, mod3, g_norm.reshape(1, D_MODEL), w_in_bf)


def _attn_kernel(slope_ref, q_ref, k_ref, v_ref, o_ref, qf, kf, vf, oacc, lacc, tbl, *, seq):
    qf[...] = q_ref[0].astype(F32)
    kf[...] = k_ref[0].astype(F32)
    vf[...] = v_ref[0].astype(F32)

    lane = lax.broadcasted_iota(jnp.int32, (1, LANES), 1)
    head0 = lane < HEAD_DIM
    slopes = slope_ref[0]
    slope_h = (slopes[:, 0:1], slopes[:, HEAD_DIM:HEAD_DIM + 1])

    whole_class = [seq // dil == 2 * ATTN_BLOCK for _, dil in DILATED_BRANCHES]
    for bi, (window, dil) in enumerate(DILATED_BRANCHES):
        steps = window // dil
        if whole_class[bi]:
            back = (lax.broadcasted_iota(jnp.int32, (2 * ATTN_BLOCK, 2 * ATTN_BLOCK), 0)
                    - lax.broadcasted_iota(jnp.int32, (2 * ATTN_BLOCK, 2 * ATTN_BLOCK), 1))
            valid = (back >= 0) & (back <= steps)
            dist = (back * dil).astype(F32)
            for hh in range(2):
                tbl[bi, hh] = jnp.where(valid, -slope_h[hh] * dist, NEG_BIG)
            continue
        qr = lax.broadcasted_iota(jnp.int32, (ATTN_BLOCK, 2 * ATTN_BLOCK), 0)
        kc = lax.broadcasted_iota(jnp.int32, (ATTN_BLOCK, 2 * ATTN_BLOCK), 1)
        for var in range(2):
            back = qr - kc + var * ATTN_BLOCK
            valid = (back >= 0) & (back <= steps)
            dist = (back * dil).astype(F32)
            for hh in range(2):
                tbl[bi, var, hh * ATTN_BLOCK:(hh + 1) * ATTN_BLOCK, :] = jnp.where(
                    valid, -slope_h[hh] * dist, NEG_BIG)

    for bi, (window, dil) in enumerate(DILATED_BRANCHES):
        cls_len = seq // dil
        nb = cls_len // ATTN_BLOCK
        nb_shift = nb.bit_length() - 1
        whole = whole_class[bi]
        qn = 2 * ATTN_BLOCK if whole else ATTN_BLOCK
        n_units = dil if whole else dil * nb
        per_step = ATTN_UNROLL * ATTN_BLOCK // qn

        def rows(start, size, dil=dil):
            if dil == 1:
                return pl.ds(pl.multiple_of(start, ATTN_BLOCK), size)
            return pl.ds(start, size, stride=dil)

        def block(it, bi=bi, dil=dil, nb=nb, nb_shift=nb_shift, rows=rows, whole=whole, qn=qn):
            if whole:
                q_rows = rows(it, qn)
                k_rows = q_rows
                table = jnp.concatenate([tbl[bi, 0], tbl[bi, 1]], axis=0)
            else:
                r = lax.shift_right_logical(it, nb_shift)
                i = it & (nb - 1)
                var = jnp.minimum(i, 1)
                q_rows = rows(i * ATTN_BLOCK * dil + r, qn)
                k_rows = rows((i - var) * ATTN_BLOCK * dil + r, 2 * ATTN_BLOCK)
                table = tbl[bi, var]
            q2 = qf[q_rows, :]
            kb = kf[k_rows, :].astype(BF16)
            v2 = vf[k_rows, :]
            qs = jnp.concatenate([jnp.where(head0, q2, 0.0), jnp.where(head0, 0.0, q2)], axis=0)
            s = _dot_nt(qs.astype(BF16), kb) + table
            m = jnp.max(s, axis=-1, keepdims=True)
            p = jnp.exp(s - m)
            den = jnp.sum(p, axis=-1, keepdims=True)
            pb = p.astype(BF16)
            vs = jnp.concatenate([jnp.where(head0, v2, 0.0), jnp.where(head0, 0.0, v2)], axis=0)
            o = _dot(jnp.concatenate([pb[:qn], pb[qn:]], axis=1), vs.astype(BF16))
            den2 = jnp.where(head0, den[:qn], den[qn:])
            lse = jnp.where(head0, m[:qn], m[qn:]) + jnp.log(den2)
            return q_rows, o / den2, lse

        def body(step, carry, bi=bi, block=block, per_step=per_step):
            done = [block(step * per_step + j) for j in range(per_step)]
            if bi > 0:
                merged = []
                for q_rows, o, lse in done:
                    l_old = lacc[q_rows, :]
                    m2 = jnp.maximum(l_old, lse)
                    a = jnp.exp(l_old - m2)
                    b = jnp.exp(lse - m2)
                    tot = a + b
                    merged.append((q_rows, (oacc[q_rows, :] * a + o * b) / tot, m2 + jnp.log(tot)))
                done = merged
            for q_rows, o, lse in done:
                oacc[q_rows, :] = o
                lacc[q_rows, :] = lse
            return carry

        assert n_units % per_step == 0
        lax.fori_loop(0, n_units // per_step, body, 0)

    o_ref[0] = oacc[...].astype(BF16)


def _attention(q, k, v, slopes_lane):
    B, S, _ = q.shape
    for window, dil in DILATED_BRANCHES:
        assert window // dil <= ATTN_BLOCK
        cls_len = S // dil
        assert S % dil == 0 and cls_len % ATTN_BLOCK == 0 and cls_len >= 2 * ATTN_BLOCK
        assert (cls_len // ATTN_BLOCK) & (cls_len // ATTN_BLOCK - 1) == 0
    n_pairs = ATTN_WIDTH // LANES
    tile = pl.BlockSpec((1, S, LANES), lambda b, p: (b, 0, p))
    return pl.pallas_call(
        functools.partial(_attn_kernel, seq=S),
        out_shape=jax.ShapeDtypeStruct((B, S, ATTN_WIDTH), BF16),
        grid=(B, n_pairs),
        in_specs=[pl.BlockSpec((1, 1, LANES), lambda b, p: (p, 0, 0)), tile, tile, tile],
        out_specs=tile,
        scratch_shapes=[pltpu.VMEM((S, LANES), F32)] * 5
        + [pltpu.VMEM((len(DILATED_BRANCHES), 2, 2 * ATTN_BLOCK, 2 * ATTN_BLOCK), F32)],
        compiler_params=_params(("arbitrary", "arbitrary")),
        name="attn",
    )(slopes_lane, q, k, v)


def _mix_kernel(attn_ref, u_ref, z_ref, x_ref, mod_ref, ga_ref, gs_ref, lng_ref, lnb_ref, wsp_ref,
                bsp_ref, wout_ref, gffn_ref, wrt_ref, wsgu_ref, wsd_ref,
                hs_ref, n2p_ref, lg_ref):
    tm = x_ref.shape[1]
    nc = tm // CHUNK
    a_n = _rms(attn_ref[0].astype(F32), ga_ref[...])

    ug = jax.nn.gelu(u_ref[0].astype(F32))
    zg = jax.nn.gelu(z_ref[0].astype(F32))
    mu = jnp.mean(zg, axis=-1, keepdims=True)
    zc = zg - mu
    var = jnp.mean(zc * zc, axis=-1, keepdims=True)
    zb = (zc * lax.rsqrt(var + EPS) * lng_ref[...] + lnb_ref[...]).astype(BF16)

    row = lax.broadcasted_iota(jnp.int32, (CHUNK, CHUNK), 0)
    col = lax.broadcasted_iota(jnp.int32, (CHUNK, CHUNK), 1)
    per_group = []
    for g in range(SGU_GROUPS):
        wc = jnp.where(row >= col, wsp_ref[g], 0.0).astype(BF16)
        lanes = slice(g * SGU_GROUP_DIM, (g + 1) * SGU_GROUP_DIM)
        zcat = jnp.concatenate([zb[c * CHUNK:(c + 1) * CHUNK, lanes] for c in range(nc)], axis=1)
        per_group.append(_dot(wc, zcat) + bsp_ref[:, g:g + 1])
    mixed = jnp.concatenate(
        [jnp.concatenate([per_group[g][:, c * CHUNK:(c + 1) * CHUNK] for g in range(SGU_GROUPS)], axis=1)
         for c in range(nc)], axis=0)
    s_n = _rms(ug * mixed, gs_ref[...])

    mix = (_dot(a_n.astype(BF16), wout_ref[0:ATTN_WIDTH, :])
           + _dot(s_n.astype(BF16), wout_ref[ATTN_WIDTH:, :]))
    gate1 = mod_ref[0, 2:3, :]
    shift2 = mod_ref[0, 3:4, :]
    scale2 = mod_ref[0, 4:5, :]
    gate2 = mod_ref[0, 5:6, :]
    h1 = x_ref[0] + gate1 * mix
    n2 = _rms(h1, gffn_ref[...]) * (1.0 + scale2) + shift2
    n2b = n2.astype(BF16)

    lg_ref[...] = _dot_nt(wrt_ref[...], n2b)
    gu = _dot(n2b, wsgu_ref[...])
    gsh = gu[:, :EXPERT_DIM]
    act = (gsh * jax.nn.sigmoid(gsh)) * gu[:, EXPERT_DIM:]
    shared = _dot(act.astype(BF16), wsd_ref[...])
    hs_ref[0] = h1 + gate2 * shared
    _store_rows(n2p_ref, _pack_halves(n2))


def _mix(attn, u, z, x, mod3, g_out_attn, g_out_sgu, ln_g, ln_b, w_spatial, b_spatial_t, w_out_bf,
         g_norm_ffn, w_router_t_bf, w_sh_gu_bf, w_sh_d_bf):
    B, S, _ = x.shape
    tm = TM_PROJ
    nt = S // tm
    N = B * S
    half_tile = pl.BlockSpec((1, tm, ATTN_WIDTH), lambda b, i: (b, i, 0))
    full_tile = pl.BlockSpec((1, tm, D_MODEL), lambda b, i: (b, i, 0))

    def const(shape):
        return pl.BlockSpec(shape, lambda b, i: (0,) * len(shape))

    return pl.pallas_call(
        _mix_kernel,
        out_shape=(
            jax.ShapeDtypeStruct((B, S, D_MODEL), F32),
            jax.ShapeDtypeStruct((N * ROW_SPLIT, LANES), jnp.uint32),
            jax.ShapeDtypeStruct((N_EXPERTS, N), F32),
        ),
        grid=(B, nt),
        in_specs=[
            half_tile, half_tile, half_tile, full_tile,
            pl.BlockSpec((1, 6, D_MODEL), lambda b, i: (b, 0, 0)),
            const((1, ATTN_WIDTH)), const((1, SGU_WIDTH)), const((1, SGU_WIDTH)), const((1, SGU_WIDTH)),
            const((SGU_GROUPS, CHUNK, CHUNK)), const((CHUNK, SGU_GROUPS)),
            const((ATTN_WIDTH + SGU_WIDTH, D_MODEL)), const((1, D_MODEL)),
            const((N_EXPERTS, D_MODEL)), const((D_MODEL, 2 * EXPERT_DIM)), const((EXPERT_DIM, D_MODEL)),
        ],
        out_specs=(
            full_tile,
            pl.BlockSpec((tm * ROW_SPLIT, LANES), lambda b, i: (b * nt + i, 0)),
            pl.BlockSpec((N_EXPERTS, tm), lambda b, i: (0, b * nt + i)),
        ),
        compiler_params=_params(("arbitrary", "arbitrary")),
        name="mix",
    )(attn, u, z, x, mod3, g_out_attn.reshape(1, -1), g_out_sgu.reshape(1, -1), ln_g.reshape(1, -1),
      ln_b.reshape(1, -1), w_spatial, b_spatial_t, w_out_bf, g_norm_ffn.reshape(1, -1), w_router_t_bf,
      w_sh_gu_bf, w_sh_d_bf)


def _first_max(v, iota, size):
    mx = jnp.max(v, axis=0, keepdims=True)
    am = jnp.min(jnp.where(v == mx, iota, size), axis=0, keepdims=True)
    return mx, am


def _route_kernel(lg_ref, br_ref, idx_ref, gate_ref, rank_ref, cnt_ref, carry, tri):
    step = pl.program_id(0)
    tr = lg_ref.shape[1]

    @pl.when(step == 0)
    def _():
        carry[...] = jnp.zeros_like(carry)
        before = (lax.broadcasted_iota(jnp.int32, (tr, tr), 0)
                  < lax.broadcasted_iota(jnp.int32, (tr, tr), 1))
        tri[...] = jnp.where(before, 1.0, 0.0).astype(BF16)

    scores = jax.nn.sigmoid(lg_ref[...])
    choice = scores + br_ref[...]

    iota_g = lax.broadcasted_iota(jnp.int32, (GROUP_SIZE, tr), 0)
    gs = []
    for g in range(N_EXPERT_GROUPS):
        cg = choice[g * GROUP_SIZE:(g + 1) * GROUP_SIZE, :]
        m1, am = _first_max(cg, iota_g, GROUP_SIZE)
        m2 = jnp.max(jnp.where(iota_g == am, -jnp.inf, cg), axis=0, keepdims=True)
        gs.append(m1 + m2)
    gscore = jnp.concatenate(gs, axis=0)

    iota_n = lax.broadcasted_iota(jnp.int32, (N_EXPERT_GROUPS, tr), 0)
    t = gscore
    for _ in range(TOPK_GROUPS - 1):
        _, am = _first_max(t, iota_n, N_EXPERT_GROUPS)
        t = jnp.where(iota_n == am, -jnp.inf, t)
    kth = jnp.max(t, axis=0, keepdims=True)
    keep = gscore >= kth

    v = jnp.concatenate(
        [jnp.where(keep[g:g + 1, :], choice[g * GROUP_SIZE:(g + 1) * GROUP_SIZE, :], -jnp.inf)
         for g in range(N_EXPERT_GROUPS)], axis=0)
    iota_e = lax.broadcasted_iota(jnp.int32, (N_EXPERTS, tr), 0)
    idxs, sels = [], []
    chosen = jnp.zeros((N_EXPERTS, tr), F32)
    for _ in range(TOP_K):
        _, am = _first_max(v, iota_e, N_EXPERTS)
        hit = iota_e == am
        idxs.append(am)
        sels.append(jnp.sum(jnp.where(hit, scores, 0.0), axis=0, keepdims=True))
        chosen = jnp.where(hit, 1.0, chosen)
        v = jnp.where(hit, -jnp.inf, v)
    sel = jnp.concatenate(sels, axis=0)
    idx_ref[...] = jnp.concatenate(idxs, axis=0)
    gate_ref[...] = sel / jnp.sum(sel, axis=0, keepdims=True) * ROUTED_SCALE

    earlier = _dot(chosen.astype(BF16), tri[...]) + carry[...]
    rank_ref[...] = jnp.concatenate(
        [jnp.sum(jnp.where(iota_e == am, earlier, 0.0), axis=0, keepdims=True) for am in idxs],
        axis=0).astype(jnp.int32)
    carry[...] = carry[...] + jnp.sum(chosen, axis=1, keepdims=True)
    cnt_ref[...] = jnp.broadcast_to(carry[...], cnt_ref.shape)


def _route(logits_t, b_router):
    N = logits_t.shape[1]
    tr = TM_ROUTE
    kt = pl.BlockSpec((TOP_K, tr), lambda i: (0, i))
    return pl.pallas_call(
        _route_kernel,
        out_shape=(
            jax.ShapeDtypeStruct((TOP_K, N), jnp.int32),
            jax.ShapeDtypeStruct((TOP_K, N), F32),
            jax.ShapeDtypeStruct((TOP_K, N), jnp.int32),
            jax.ShapeDtypeStruct((N_EXPERTS, LANES), F32),
        ),
        grid=(N // tr,),
        in_specs=[
            pl.BlockSpec((N_EXPERTS, tr), lambda i: (0, i)),
            pl.BlockSpec((N_EXPERTS, 1), lambda i: (0, 0)),
        ],
        out_specs=(kt, kt, kt, pl.BlockSpec((N_EXPERTS, LANES), lambda i: (0, 0))),
        scratch_shapes=[pltpu.VMEM((N_EXPERTS, 1), F32), pltpu.VMEM((tr, tr), BF16)],
        compiler_params=_params(("arbitrary",)),
        name="route",
    )(logits_t, b_router.reshape(N_EXPERTS, 1))


def _dest_kernel(idx_ref, rank_ref, ps_ref, dest_ref):
    tr = idx_ref.shape[1]
    iota_e = lax.broadcasted_iota(jnp.int32, (N_EXPERTS, tr), 0)
    idx = idx_ref[...]
    start = jnp.concatenate(
        [jnp.sum(jnp.where(iota_e == idx[k:k + 1, :], ps_ref[...], 0.0), axis=0, keepdims=True)
         for k in range(TOP_K)], axis=0)
    dest_ref[...] = (start.astype(jnp.int32) + rank_ref[...]) * ROW_SPLIT


def _dest(idx_t, rank_t, pstart):
    N = idx_t.shape[1]
    tr = TM_ROUTE
    kt = pl.BlockSpec((TOP_K, tr), lambda i: (0, i))
    return pl.pallas_call(
        _dest_kernel,
        out_shape=jax.ShapeDtypeStruct((TOP_K, N), jnp.int32),
        grid=(N // tr,),
        in_specs=[kt, kt, pl.BlockSpec((N_EXPERTS, 1), lambda i: (0, 0))],
        out_specs=kt,
        compiler_params=_params(("arbitrary",)),
        name="dest",
    )(idx_t, rank_t, pstart.astype(F32).reshape(N_EXPERTS, 1))


def _row_copy(src, dst, sem):
    return pltpu.make_async_copy(src, dst, sem)


def _token_rows(start):
    return pl.ds(pl.multiple_of(start, ROW_SPLIT), ROW_SPLIT)


def _block_rows(block):
    return pl.ds(pl.multiple_of(block * BLOCK_SUBROWS, BLOCK_SUBROWS), BLOCK_SUBROWS)


def _disp_kernel(zb_ref, dest_ref, x_ref, xs_ref, xbuf, zbuf, sem_in, sem_out):
    i = pl.program_id(0)
    n = pl.num_programs(0)
    tile_rows = xbuf.shape[1]
    tm = tile_rows // ROW_SPLIT
    slot = lax.rem(i, DISP_RING)
    ahead = lax.rem(i + 1, DISP_RING)

    def load(tile, into):
        start = pl.multiple_of(tile * tile_rows, tile_rows)
        return _row_copy(x_ref.at[pl.ds(start, tile_rows)], xbuf.at[into], sem_in.at[into])

    def drain(of):
        for k in range(TOP_K):
            _row_copy(xbuf.at[of], xs_ref.at[pl.ds(0, tile_rows)], sem_out.at[of]).wait()

    @pl.when(i == 0)
    def _():
        zbuf[...] = jnp.zeros_like(zbuf)

        def zero(j, started):
            blk = zb_ref[j]

            @pl.when(blk >= 0)
            def _():
                _row_copy(zbuf, xs_ref.at[_block_rows(blk)], sem_out.at[0]).start()

            return started + jnp.where(blk >= 0, 1, 0)

        def done(j, carry):
            _row_copy(zbuf, xs_ref.at[_block_rows(0)], sem_out.at[0]).wait()
            return carry

        started = lax.fori_loop(0, zb_ref.shape[0], zero, 0)
        lax.fori_loop(0, started, done, 0)
        load(0, 0).start()

    @pl.when(i + 1 < n)
    def _():
        @pl.when(i + 1 >= DISP_RING)
        def _():
            drain(ahead)

        load(i + 1, ahead).start()

    load(i, slot).wait()

    def issue(t, carry):
        src = xbuf.at[slot, _token_rows(t * ROW_SPLIT)]
        base = t * TOP_K
        for k in range(TOP_K):
            _row_copy(src, xs_ref.at[_token_rows(dest_ref[base + k])], sem_out.at[slot]).start(priority=k % 2)
        return carry

    lax.fori_loop(0, tm, issue, 0)

    @pl.when(i == n - 1)
    def _():
        for s in range(DISP_RING):
            drain(s)


def _dispatch(zero_blocks, dest_flat, n2p, n_rows):
    N = n2p.shape[0] // ROW_SPLIT
    tm = TM_MOVE
    assert N // tm >= DISP_RING
    grid_spec = pltpu.PrefetchScalarGridSpec(
        num_scalar_prefetch=1,
        grid=(N // tm,),
        in_specs=[
            pl.BlockSpec((tm * TOP_K,), lambda i, zb: (i,), memory_space=pltpu.SMEM),
            pl.BlockSpec(memory_space=pl.ANY),
        ],
        out_specs=pl.BlockSpec(memory_space=pl.ANY),
        scratch_shapes=[
            pltpu.VMEM((DISP_RING, tm * ROW_SPLIT, LANES), jnp.uint32),
            pltpu.VMEM((BLOCK_SUBROWS, LANES), jnp.uint32),
            pltpu.SemaphoreType.DMA((DISP_RING,)),
            pltpu.SemaphoreType.DMA((DISP_RING,)),
        ],
    )
    return pl.pallas_call(
        _disp_kernel,
        out_shape=jax.ShapeDtypeStruct((n_rows * ROW_SPLIT, LANES), jnp.uint32),
        grid_spec=grid_spec,
        compiler_params=_params(("arbitrary",)),
        name="disp",
    )(zero_blocks, dest_flat, n2p)


def _experts_kernel(eb_ref, bv_ref, xs_ref, wg_ref, wu_ref, wd_ref, ys_ref,
                    xbuf, ybuf, wgb, wub, wdb, sem_in, sem_out):
    e = pl.program_id(0)
    n_used = eb_ref[N_EXPERTS]
    n_blocks = ys_ref.shape[0] // BLOCK_SUBROWS
    first = eb_ref[e]
    end = eb_ref[e + 1]

    def in_copy(g, slot):
        return _row_copy(xs_ref.at[_block_rows(g)], xbuf.at[slot], sem_in.at[slot])

    def out_copy(g, slot):
        return _row_copy(ybuf.at[slot], ys_ref.at[_block_rows(g)], sem_out.at[slot])

    @pl.when(e == 0)
    def _():
        for g in range(RING - 1):
            @pl.when(g < n_used)
            def _(g=g):
                in_copy(g, g).start()

    @pl.when(end > first)
    def _():
        wgb[...] = wg_ref[...].astype(BF16)
        wub[...] = wu_ref[...].astype(BF16)
        wdb[...] = wd_ref[...].astype(BF16)

        def block(g, carry):
            slot = g & (RING - 1)
            in_copy(g, slot).wait()

            @pl.when(g + RING - 1 < n_used)
            def _():
                in_copy(g + RING - 1, (g + RING - 1) & (RING - 1)).start()

            @pl.when(g >= RING)
            def _():
                out_copy(g - RING, slot).wait()

            lo, hi = _unpack_halves(_load_rows(xbuf.at[slot], ROW_BLOCK))
            live = lax.broadcasted_iota(jnp.int32, lo.shape, 0) < bv_ref[g]
            lo = jnp.where(live, lo, 0.0).astype(BF16)
            hi = jnp.where(live, hi, 0.0).astype(BF16)
            gate = _dot(lo, wgb[0:HALF, :]) + _dot(hi, wgb[HALF:, :])
            up = _dot(lo, wub[0:HALF, :]) + _dot(hi, wub[HALF:, :])
            act = (gate * jax.nn.sigmoid(gate)) * up
            _store_rows(ybuf.at[slot], _pack_halves(_dot(act.astype(BF16), wdb[...])))
            out_copy(g, slot).start()
            return carry

        lax.fori_loop(first, end, block, 0)

    @pl.when(e == N_EXPERTS - 1)
    def _():
        for back in range(1, RING + 1):
            @pl.when(n_used >= back)
            def _(back=back):
                out_copy(n_used - back, (n_used - back) & (RING - 1)).wait()

        xbuf[0] = jnp.zeros((BLOCK_SUBROWS, LANES), jnp.uint32)

        def zero(g, carry):
            _row_copy(xbuf.at[0], ys_ref.at[_block_rows(g)], sem_out.at[0]).start()
            return carry

        def done(g, carry):
            _row_copy(xbuf.at[0], ys_ref.at[_block_rows(g)], sem_out.at[0]).wait()
            return carry

        lax.fori_loop(n_used, n_blocks, zero, 0)
        lax.fori_loop(n_used, n_blocks, done, 0)


def _experts(expert_block, block_valid, xs, w_gate, w_up, w_down):
    def w_map(e, eb, bv):
        return (e, 0, 0)

    grid_spec = pltpu.PrefetchScalarGridSpec(
        num_scalar_prefetch=2,
        grid=(N_EXPERTS,),
        in_specs=[
            pl.BlockSpec(memory_space=pl.ANY),
            pl.BlockSpec((None, D_MODEL, EXPERT_DIM), w_map),
            pl.BlockSpec((None, D_MODEL, EXPERT_DIM), w_map),
            pl.BlockSpec((None, EXPERT_DIM, D_MODEL), w_map),
        ],
        out_specs=pl.BlockSpec(memory_space=pl.ANY),
        scratch_shapes=[
            pltpu.VMEM((RING, BLOCK_SUBROWS, LANES), jnp.uint32),
            pltpu.VMEM((RING, BLOCK_SUBROWS, LANES), jnp.uint32),
            pltpu.VMEM((D_MODEL, EXPERT_DIM), BF16),
            pltpu.VMEM((D_MODEL, EXPERT_DIM), BF16),
            pltpu.VMEM((EXPERT_DIM, D_MODEL), BF16),
            pltpu.SemaphoreType.DMA((RING,)),
            pltpu.SemaphoreType.DMA((RING,)),
        ],
    )
    return pl.pallas_call(
        _experts_kernel,
        out_shape=jax.ShapeDtypeStruct(xs.shape, jnp.uint32),
        grid_spec=grid_spec,
        compiler_params=_params(("arbitrary",)),
        name="experts",
    )(expert_block, block_valid, xs, w_gate, w_up, w_down)


def _comb_kernel(dest_ref, next_ref, ys_ref, hs_ref, gate_ref, mod_ref, gf_ref, o_ref, buf, sem):
    i = pl.program_id(0)
    tm = hs_ref.shape[0]
    slot = i & 1

    def gather(rows_ref, into):
        def issue(t, carry):
            base = t * TOP_K
            for k in range(TOP_K):
                _row_copy(ys_ref.at[_token_rows(rows_ref[base + k])],
                          buf.at[into, k, _token_rows(t * ROW_SPLIT)], sem.at[into]).start(priority=k % 2)
            return carry

        lax.fori_loop(0, tm, issue, 0)

    @pl.when(i == 0)
    def _():
        gather(dest_ref, 0)

    for into in range(2):
        @pl.when(jnp.logical_and(i + 1 < pl.num_programs(0), slot == 1 - into))
        def _(into=into):
            gather(next_ref, into)

    for k in range(TOP_K):
        _row_copy(ys_ref.at[pl.ds(0, tm * ROW_SPLIT)], buf.at[slot, k], sem.at[slot]).wait()

    for c in range(tm // COMB_CHUNK):
        tok = slice(c * COMB_CHUNK, (c + 1) * COMB_CHUNK)
        gates = gate_ref[tok, :]
        lo = jnp.zeros((COMB_CHUNK, HALF), F32)
        hi = jnp.zeros((COMB_CHUNK, HALF), F32)
        for k in range(TOP_K):
            piece = buf.at[slot, k, pl.ds(c * COMB_CHUNK * ROW_SPLIT, COMB_CHUNK * ROW_SPLIT)]
            lo_k, hi_k = _unpack_halves(_load_rows(piece, COMB_CHUNK))
            gk = gates[:, k:k + 1]
            lo = lo + gk * lo_k
            hi = hi + gk * hi_k
        routed = jnp.concatenate([lo, hi], axis=1)
        h2 = hs_ref[tok, :] + mod_ref[0, 5:6, :] * routed
        o_ref[tok, :] = _rms(h2, gf_ref[...])


def _combine(dest_flat, ys, hs2, gates_nk, mod3, g_final, seq):
    N = hs2.shape[0]
    tm = TM_MOVE
    per_seq = seq // tm
    n_tiles = N // tm
    return pl.pallas_call(
        _comb_kernel,
        out_shape=jax.ShapeDtypeStruct((N, D_MODEL), F32),
        grid=(n_tiles,),
        in_specs=[
            pl.BlockSpec((tm * TOP_K,), lambda i: (i,), memory_space=pltpu.SMEM),
            pl.BlockSpec((tm * TOP_K,), lambda i: (jnp.minimum(i + 1, n_tiles - 1),), memory_space=pltpu.SMEM),
            pl.BlockSpec(memory_space=pl.ANY),
            pl.BlockSpec((tm, D_MODEL), lambda i: (i, 0)),
            pl.BlockSpec((tm, TOP_K), lambda i: (i, 0)),
            pl.BlockSpec((1, 6, D_MODEL), lambda i: (i // per_seq, 0, 0)),
            pl.BlockSpec((1, D_MODEL), lambda i: (0, 0)),
        ],
        out_specs=pl.BlockSpec((tm, D_MODEL), lambda i: (i, 0)),
        scratch_shapes=[pltpu.VMEM((2, TOP_K, tm * ROW_SPLIT, LANES), jnp.uint32),
                        pltpu.SemaphoreType.DMA((2,))],
        compiler_params=_params(("arbitrary",)),
        name="comb",
    )(dest_flat, dest_flat, ys, hs2, gates_nk, mod3, g_final.reshape(1, D_MODEL))


def _layer(x, mod3, g_norm_mix, w_in, sgu_ln_gain, sgu_ln_bias, w_spatial, b_spatial, g_out_attn,
           g_out_sgu, w_out, g_norm_ffn, w_router, b_router, w_exp_gate, w_exp_up, w_exp_down,
           w_sh_gate, w_sh_up, w_sh_down, g_final):
    B, S, _ = x.shape
    N = B * S

    q, k, v, u, z = _inproj(x, mod3, g_norm_mix, w_in.astype(BF16))
    head_of_lane = jnp.arange(ATTN_WIDTH) // HEAD_DIM
    slopes = jnp.exp2(-8.0 * jnp.arange(1, ATTN_HEADS + 1, dtype=F32) / ATTN_HEADS)
    slopes_lane = slopes[head_of_lane].reshape(ATTN_WIDTH // LANES, 1, LANES)
    attn = _attention(q, k, v, slopes_lane)

    hs, n2p, logits_t = _mix(
        attn, u, z, x, mod3, g_out_attn, g_out_sgu, sgu_ln_gain, sgu_ln_bias, w_spatial, b_spatial.T,
        w_out.astype(BF16), g_norm_ffn, w_router.T.astype(BF16),
        jnp.concatenate([w_sh_gate, w_sh_up], axis=1).astype(BF16), w_sh_down.astype(BF16))

    idx_t, gates_t, rank_t, counts = _route(logits_t, b_router)

    n_blocks = (N * TOP_K + N_EXPERTS * (ROW_BLOCK - 1)) // ROW_BLOCK
    cnt = counts[:, 0].astype(jnp.int32)
    padded = (cnt + ROW_BLOCK - 1) // ROW_BLOCK * ROW_BLOCK
    pends = jnp.cumsum(padded)
    pstart = pends - padded
    block_row = jnp.arange(n_blocks, dtype=jnp.int32) * ROW_BLOCK
    block_e = jnp.sum(pends[None, :] <= block_row[:, None], axis=1, dtype=jnp.int32)
    block_e = jnp.minimum(block_e, N_EXPERTS - 1)
    own = block_e[:, None] == jnp.arange(N_EXPERTS, dtype=jnp.int32)[None, :]
    live_end = jnp.sum(jnp.where(own, (pstart + cnt)[None, :], 0), axis=1)
    block_valid = jnp.clip(live_end - block_row, 0, ROW_BLOCK).astype(jnp.int32)
    n_used = (pends[-1:] // ROW_BLOCK).astype(jnp.int32)
    expert_block = jnp.concatenate([pstart // ROW_BLOCK, n_used]).astype(jnp.int32)

    last_block = jnp.where(cnt > 0, pends // ROW_BLOCK - 1, -1)
    tail_block = n_used[0] + jnp.arange(n_blocks - N * TOP_K // ROW_BLOCK)
    tail_block = jnp.where(tail_block < n_blocks, tail_block, -1)
    zero_blocks = jnp.concatenate([last_block, tail_block]).astype(jnp.int32)

    dest_flat = _dest(idx_t, rank_t, pstart).T.reshape(N * TOP_K)
    xs = _dispatch(zero_blocks, dest_flat, n2p, n_blocks * ROW_BLOCK)
    ys = _experts(expert_block, block_valid, xs, w_exp_gate, w_exp_up, w_exp_down)
    out = _combine(dest_flat, ys, hs.reshape(N, D_MODEL), gates_t.T, mod3, g_final, S)
    return out.reshape(B, S, D_MODEL)


def kernel(x, c, w_ada, b_ada, g_norm_mix, w_in, sgu_ln_gain, sgu_ln_bias, w_spatial, b_spatial, g_out_attn, g_out_sgu, w_out, g_norm_ffn, w_router, b_router, w_exp_gate, w_exp_up, w_exp_down, w_sh_gate, w_sh_up, w_sh_down, g_final):
    assert w_ada.shape[0] == 1, "single-layer stack"
    B = x.shape[0]
    mod3 = _ada(c, w_ada[0], b_ada[0]).reshape(B, 6, D_MODEL)
    return _layer(x, mod3, g_norm_mix[0], w_in[0], sgu_ln_gain[0], sgu_ln_bias[0], w_spatial[0],
                  b_spatial[0], g_out_attn[0], g_out_sgu[0], w_out[0], g_norm_ffn[0], w_router[0],
                  b_router[0], w_exp_gate[0], w_exp_up[0], w_exp_down[0], w_sh_gate[0], w_sh_up[0],
                  w_sh_down[0], g_final)
```

```python
import functools

import jax
import jax.numpy as jnp
from jax import lax
from jax.experimental import pallas as pl
from jax.experimental.pallas import tpu as pltpu

F32 = jnp.float32
BF16 = jnp.bfloat16

D_MODEL = 1024
ATTN_WIDTH = 512
ATTN_HEADS = 8
HEAD_DIM = 64
SGU_WIDTH = 512
SGU_GROUPS = 4
SGU_GROUP_DIM = 128
CHUNK = 128
DILATED_BRANCHES = ((128, 1), (512, 4), (2048, 16))
ATTN_BLOCK = 128
N_EXPERTS = 256
TOP_K = 8
N_EXPERT_GROUPS = 8
GROUP_SIZE = N_EXPERTS // N_EXPERT_GROUPS
TOPK_GROUPS = 4
EXPERT_DIM = 256
ROUTED_SCALE = 2.5
EPS = 1e-6

LANES = 128
HALF = D_MODEL // 2
ROW_BLOCK = 512
NEG_BIG = -1e30

TM_PROJ = 1024
TM_ROUTE = 512
TM_MOVE = 512
DISP_RING = 3
COMB_CHUNK = 128
ATTN_UNROLL = 16
RING = 4


def _dot(a, b):
    return jnp.dot(a, b, preferred_element_type=F32)


def _dot_nt(a, b):
    return lax.dot_general(a, b, (((1,), (1,)), ((), ())), preferred_element_type=F32)


def _rms(x, g):
    return x * lax.rsqrt(jnp.mean(x * x, axis=-1, keepdims=True) + EPS) * g


def _pack_halves(x):
    return pltpu.pack_elementwise([x[:, :HALF], x[:, HALF:]], packed_dtype=BF16)


def _unpack_halves(w):
    lo = pltpu.unpack_elementwise(w, index=0, packed_dtype=BF16, unpacked_dtype=F32)
    hi = pltpu.unpack_elementwise(w, index=1, packed_dtype=BF16, unpacked_dtype=F32)
    return lo, hi


ROW_SPLIT = HALF // LANES
BLOCK_SUBROWS = ROW_BLOCK * ROW_SPLIT


def _store_rows(ref, packed):
    rows = packed.shape[0]
    for c in range(ROW_SPLIT):
        ref[pl.ds(c, rows, stride=ROW_SPLIT), :] = packed[:, c * LANES:(c + 1) * LANES]


def _load_rows(ref, rows):
    return jnp.concatenate([ref[pl.ds(c, rows, stride=ROW_SPLIT), :] for c in range(ROW_SPLIT)], axis=1)


def _params(n_axes, vmem_mib):
    return pltpu.CompilerParams(dimension_semantics=("arbitrary",) * n_axes,
                                vmem_limit_bytes=vmem_mib * 1024 * 1024)


def _ada_kernel(c_ref, w_ref, b_ref, o_ref):
    c = c_ref[...]
    cond = c * jax.nn.sigmoid(c)
    ch = cond.astype(BF16)
    cl = (cond - ch.astype(F32)).astype(BF16)
    w = w_ref[...]
    wh = w.astype(BF16)
    wl = (w - wh.astype(F32)).astype(BF16)
    o_ref[...] = _dot(ch, wh) + _dot(cl, wh) + _dot(ch, wl) + b_ref[...]


def _ada(c, w_ada, b_ada):
    B = c.shape[0]
    n_out = w_ada.shape[1]
    tn = D_MODEL
    return pl.pallas_call(
        _ada_kernel,
        out_shape=jax.ShapeDtypeStruct((B, n_out), F32),
        grid=(n_out // tn,),
        in_specs=[
            pl.BlockSpec((B, D_MODEL), lambda j: (0, 0)),
            pl.BlockSpec((D_MODEL, tn), lambda j: (0, j)),
            pl.BlockSpec((1, tn), lambda j: (0, j)),
        ],
        out_specs=pl.BlockSpec((B, tn), lambda j: (0, j)),
        compiler_params=_params(1, 16),
        name="ada",
    )(c, w_ada, b_ada.reshape(1, n_out))


def _inproj_kernel(x_ref, mod_ref, g_ref, w_ref, q_ref, k_ref, v_ref, u_ref, z_ref):
    x = x_ref[0]
    shift = mod_ref[0, 0:1, :]
    scale = mod_ref[0, 1:2, :]
    n = _rms(x, g_ref[...]) * (1.0 + scale) + shift
    p = _dot(n.astype(BF16), w_ref[...])
    q_ref[0] = (p[:, 0:ATTN_WIDTH] * (HEAD_DIM ** -0.5)).astype(BF16)
    k_ref[0] = p[:, ATTN_WIDTH:2 * ATTN_WIDTH].astype(BF16)
    v_ref[0] = p[:, 2 * ATTN_WIDTH:3 * ATTN_WIDTH].astype(BF16)
    u_ref[0] = p[:, 3 * ATTN_WIDTH:3 * ATTN_WIDTH + SGU_WIDTH].astype(BF16)
    z_ref[0] = p[:, 3 * ATTN_WIDTH + SGU_WIDTH:].astype(BF16)


def _inproj(x, mod3, g_norm, w_in_bf):
    B, S, _ = x.shape
    tm = TM_PROJ
    n_in = w_in_bf.shape[1]
    out = jax.ShapeDtypeStruct((B, S, ATTN_WIDTH), BF16)
    tile = pl.BlockSpec((1, tm, ATTN_WIDTH), lambda b, i: (b, i, 0))
    return pl.pallas_call(
        _inproj_kernel,
        out_shape=(out,) * 5,
        grid=(B, S // tm),
        in_specs=[
            pl.BlockSpec((1, tm, D_MODEL), lambda b, i: (b, i, 0)),
            pl.BlockSpec((1, 6, D_MODEL), lambda b, i: (b, 0, 0)),
            pl.BlockSpec((1, D_MODEL), lambda b, i: (0, 0)),
            pl.BlockSpec((D_MODEL, n_in), lambda b, i: (0, 0)),
        ],
        out_specs=(tile,) * 5,
        compiler_params=_params(2, 32),
        name="inproj",
    )(x, mod3, g_norm.reshape(1, D_MODEL), w_in_bf)


def _attn_kernel(slope_ref, q_ref, k_ref, v_ref, o_ref, qf, kf, vf, oacc, lacc, tbl, *, seq):
    qf[...] = q_ref[0].astype(F32)
    kf[...] = k_ref[0].astype(F32)
    vf[...] = v_ref[0].astype(F32)

    lane = lax.broadcasted_iota(jnp.int32, (1, LANES), 1)
    head0 = lane < HEAD_DIM
    slopes = slope_ref[0]
    slope_h = (slopes[:, 0:1], slopes[:, HEAD_DIM:HEAD_DIM + 1])

    whole_class = [seq // dil == 2 * ATTN_BLOCK for _, dil in DILATED_BRANCHES]
    for bi, (window, dil) in enumerate(DILATED_BRANCHES):
        steps = window // dil
        if whole_class[bi]:
            back = (lax.broadcasted_iota(jnp.int32, (2 * ATTN_BLOCK, 2 * ATTN_BLOCK), 0)
                    - lax.broadcasted_iota(jnp.int32, (2 * ATTN_BLOCK, 2 * ATTN_BLOCK), 1))
            valid = (back >= 0) & (back <= steps)
            dist = (back * dil).astype(F32)
            for hh in range(2):
                tbl[bi, hh] = jnp.where(valid, -slope_h[hh] * dist, NEG_BIG)
            continue
        qr = lax.broadcasted_iota(jnp.int32, (ATTN_BLOCK, 2 * ATTN_BLOCK), 0)
        kc = lax.broadcasted_iota(jnp.int32, (ATTN_BLOCK, 2 * ATTN_BLOCK), 1)
        for var in range(2):
            back = qr - kc + var * ATTN_BLOCK
            valid = (back >= 0) & (back <= steps)
            dist = (back * dil).astype(F32)
            for hh in range(2):
                tbl[bi, var, hh * ATTN_BLOCK:(hh + 1) * ATTN_BLOCK, :] = jnp.where(
                    valid, -slope_h[hh] * dist, NEG_BIG)

    for bi, (window, dil) in enumerate(DILATED_BRANCHES):
        cls_len = seq // dil
        nb = cls_len // ATTN_BLOCK
        nb_shift = nb.bit_length() - 1
        whole = whole_class[bi]
        qn = 2 * ATTN_BLOCK if whole else ATTN_BLOCK
        n_units = dil if whole else dil * nb
        per_step = ATTN_UNROLL * ATTN_BLOCK // qn

        def rows(start, size, dil=dil):
            if dil == 1:
                return pl.ds(pl.multiple_of(start, ATTN_BLOCK), size)
            return pl.ds(start, size, stride=dil)

        def block(it, bi=bi, dil=dil, nb=nb, nb_shift=nb_shift, rows=rows, whole=whole, qn=qn):
            if whole:
                q_rows = rows(it, qn)
                k_rows = q_rows
                table = jnp.concatenate([tbl[bi, 0], tbl[bi, 1]], axis=0)
            else:
                r = lax.shift_right_logical(it, nb_shift)
                i = it & (nb - 1)
                var = jnp.minimum(i, 1)
                q_rows = rows(i * ATTN_BLOCK * dil + r, qn)
                k_rows = rows((i - var) * ATTN_BLOCK * dil + r, 2 * ATTN_BLOCK)
                table = tbl[bi, var]
            q2 = qf[q_rows, :]
            kb = kf[k_rows, :].astype(BF16)
            v2 = vf[k_rows, :]
            qs = jnp.concatenate([jnp.where(head0, q2, 0.0), jnp.where(head0, 0.0, q2)], axis=0)
            s = _dot_nt(qs.astype(BF16), kb) + table
            m = jnp.max(s, axis=-1, keepdims=True)
            p = jnp.exp(s - m)
            den = jnp.sum(p, axis=-1, keepdims=True)
            pb = p.astype(BF16)
            vs = jnp.concatenate([jnp.where(head0, v2, 0.0), jnp.where(head0, 0.0, v2)], axis=0)
            o = _dot(jnp.concatenate([pb[:qn], pb[qn:]], axis=1), vs.astype(BF16))
            den2 = jnp.where(head0, den[:qn], den[qn:])
            lse = jnp.where(head0, m[:qn], m[qn:]) + jnp.log(den2)
            return q_rows, o / den2, lse

        def body(step, carry, bi=bi, block=block, per_step=per_step):
            done = [block(step * per_step + j) for j in range(per_step)]
            if bi > 0:
                merged = []
                for q_rows, o, lse in done:
                    l_old = lacc[q_rows, :]
                    m2 = jnp.maximum(l_old, lse)
                    a = jnp.exp(l_old - m2)
                    b = jnp.exp(lse - m2)
                    tot = a + b
                    merged.append((q_rows, (oacc[q_rows, :] * a + o * b) / tot, m2 + jnp.log(tot)))
                done = merged
            for q_rows, o, lse in done:
                oacc[q_rows, :] = o
                lacc[q_rows, :] = lse
            return carry

        assert n_units % per_step == 0
        lax.fori_loop(0, n_units // per_step, body, 0)

    o_ref[0] = oacc[...].astype(BF16)


def _attention(q, k, v, slopes_lane):
    B, S, _ = q.shape
    for window, dil in DILATED_BRANCHES:
        assert window // dil <= ATTN_BLOCK
        cls_len = S // dil
        assert S % dil == 0 and cls_len % ATTN_BLOCK == 0 and cls_len >= 2 * ATTN_BLOCK
        assert (cls_len // ATTN_BLOCK) & (cls_len // ATTN_BLOCK - 1) == 0
    n_pairs = ATTN_WIDTH // LANES
    tile = pl.BlockSpec((1, S, LANES), lambda b, p: (b, 0, p))
    return pl.pallas_call(
        functools.partial(_attn_kernel, seq=S),
        out_shape=jax.ShapeDtypeStruct((B, S, ATTN_WIDTH), BF16),
        grid=(B, n_pairs),
        in_specs=[pl.BlockSpec((1, 1, LANES), lambda b, p: (p, 0, 0)), tile, tile, tile],
        out_specs=tile,
        scratch_shapes=[pltpu.VMEM((S, LANES), F32)] * 5
        + [pltpu.VMEM((len(DILATED_BRANCHES), 2, 2 * ATTN_BLOCK, 2 * ATTN_BLOCK), F32)],
        compiler_params=_params(2, 32),
        name="attn",
    )(slopes_lane, q, k, v)


def _mix_kernel(attn_ref, u_ref, z_ref, x_ref, mod_ref, ga_ref, gs_ref, lng_ref, lnb_ref, wsp_ref,
                bsp_ref, wout_ref, gffn_ref, wrt_ref, wsgu_ref, wsd_ref,
                hs_ref, n2p_ref, lg_ref):
    tm = x_ref.shape[1]
    nc = tm // CHUNK
    a_n = _rms(attn_ref[0].astype(F32), ga_ref[...])

    ug = jax.nn.gelu(u_ref[0].astype(F32))
    zg = jax.nn.gelu(z_ref[0].astype(F32))
    mu = jnp.mean(zg, axis=-1, keepdims=True)
    zc = zg - mu
    var = jnp.mean(zc * zc, axis=-1, keepdims=True)
    zb = (zc * lax.rsqrt(var + EPS) * lng_ref[...] + lnb_ref[...]).astype(BF16)

    row = lax.broadcasted_iota(jnp.int32, (CHUNK, CHUNK), 0)
    col = lax.broadcasted_iota(jnp.int32, (CHUNK, CHUNK), 1)
    per_group = []
    for g in range(SGU_GROUPS):
        wc = jnp.where(row >= col, wsp_ref[g], 0.0).astype(BF16)
        lanes = slice(g * SGU_GROUP_DIM, (g + 1) * SGU_GROUP_DIM)
        zcat = jnp.concatenate([zb[c * CHUNK:(c + 1) * CHUNK, lanes] for c in range(nc)], axis=1)
        per_group.append(_dot(wc, zcat) + bsp_ref[:, g:g + 1])
    mixed = jnp.concatenate(
        [jnp.concatenate([per_group[g][:, c * CHUNK:(c + 1) * CHUNK] for g in range(SGU_GROUPS)], axis=1)
         for c in range(nc)], axis=0)
    s_n = _rms(ug * mixed, gs_ref[...])

    mix = (_dot(a_n.astype(BF16), wout_ref[0:ATTN_WIDTH, :])
           + _dot(s_n.astype(BF16), wout_ref[ATTN_WIDTH:, :]))
    gate1 = mod_ref[0, 2:3, :]
    shift2 = mod_ref[0, 3:4, :]
    scale2 = mod_ref[0, 4:5, :]
    gate2 = mod_ref[0, 5:6, :]
    h1 = x_ref[0] + gate1 * mix
    n2 = _rms(h1, gffn_ref[...]) * (1.0 + scale2) + shift2
    n2b = n2.astype(BF16)

    lg_ref[...] = _dot_nt(wrt_ref[...], n2b)
    gu = _dot(n2b, wsgu_ref[...])
    gsh = gu[:, :EXPERT_DIM]
    act = (gsh * jax.nn.sigmoid(gsh)) * gu[:, EXPERT_DIM:]
    shared = _dot(act.astype(BF16), wsd_ref[...])
    hs_ref[0] = h1 + gate2 * shared
    _store_rows(n2p_ref, _pack_halves(n2))


def _mix(attn, u, z, x, mod3, g_out_attn, g_out_sgu, ln_g, ln_b, w_spatial, b_spatial_t, w_out_bf,
         g_norm_ffn, w_router_t_bf, w_sh_gu_bf, w_sh_d_bf):
    B, S, _ = x.shape
    tm = TM_PROJ
    nt = S // tm
    N = B * S
    half_tile = pl.BlockSpec((1, tm, ATTN_WIDTH), lambda b, i: (b, i, 0))
    full_tile = pl.BlockSpec((1, tm, D_MODEL), lambda b, i: (b, i, 0))

    def const(shape):
        return pl.BlockSpec(shape, lambda b, i: (0,) * len(shape))

    return pl.pallas_call(
        _mix_kernel,
        out_shape=(
            jax.ShapeDtypeStruct((B, S, D_MODEL), F32),
            jax.ShapeDtypeStruct((N * ROW_SPLIT, LANES), jnp.uint32),
            jax.ShapeDtypeStruct((N_EXPERTS, N), F32),
        ),
        grid=(B, nt),
        in_specs=[
            half_tile, half_tile, half_tile, full_tile,
            pl.BlockSpec((1, 6, D_MODEL), lambda b, i: (b, 0, 0)),
            const((1, ATTN_WIDTH)), const((1, SGU_WIDTH)), const((1, SGU_WIDTH)), const((1, SGU_WIDTH)),
            const((SGU_GROUPS, CHUNK, CHUNK)), const((CHUNK, SGU_GROUPS)),
            const((ATTN_WIDTH + SGU_WIDTH, D_MODEL)), const((1, D_MODEL)),
            const((N_EXPERTS, D_MODEL)), const((D_MODEL, 2 * EXPERT_DIM)), const((EXPERT_DIM, D_MODEL)),
        ],
        out_specs=(
            full_tile,
            pl.BlockSpec((tm * ROW_SPLIT, LANES), lambda b, i: (b * nt + i, 0)),
            pl.BlockSpec((N_EXPERTS, tm), lambda b, i: (0, b * nt + i)),
        ),
        compiler_params=_params(2, 48),
        name="mix",
    )(attn, u, z, x, mod3, g_out_attn.reshape(1, -1), g_out_sgu.reshape(1, -1), ln_g.reshape(1, -1),
      ln_b.reshape(1, -1), w_spatial, b_spatial_t, w_out_bf, g_norm_ffn.reshape(1, -1), w_router_t_bf,
      w_sh_gu_bf, w_sh_d_bf)


def _first_max(v, iota, size):
    mx = jnp.max(v, axis=0, keepdims=True)
    am = jnp.min(jnp.where(v == mx, iota, size), axis=0, keepdims=True)
    return mx, am


def _route_kernel(lg_ref, br_ref, idx_ref, gate_ref, rank_ref, cnt_ref, carry, tri):
    step = pl.program_id(0)
    tr = lg_ref.shape[1]

    @pl.when(step == 0)
    def _():
        carry[...] = jnp.zeros_like(carry)
        before = (lax.broadcasted_iota(jnp.int32, (tr, tr), 0)
                  < lax.broadcasted_iota(jnp.int32, (tr, tr), 1))
        tri[...] = jnp.where(before, 1.0, 0.0).astype(BF16)

    scores = jax.nn.sigmoid(lg_ref[...])
    choice = scores + br_ref[...]

    iota_g = lax.broadcasted_iota(jnp.int32, (GROUP_SIZE, tr), 0)
    gs = []
    for g in range(N_EXPERT_GROUPS):
        cg = choice[g * GROUP_SIZE:(g + 1) * GROUP_SIZE, :]
        m1, am = _first_max(cg, iota_g, GROUP_SIZE)
        m2 = jnp.max(jnp.where(iota_g == am, -jnp.inf, cg), axis=0, keepdims=True)
        gs.append(m1 + m2)
    gscore = jnp.concatenate(gs, axis=0)

    iota_n = lax.broadcasted_iota(jnp.int32, (N_EXPERT_GROUPS, tr), 0)
    t = gscore
    for _ in range(TOPK_GROUPS - 1):
        _, am = _first_max(t, iota_n, N_EXPERT_GROUPS)
        t = jnp.where(iota_n == am, -jnp.inf, t)
    kth = jnp.max(t, axis=0, keepdims=True)
    keep = gscore >= kth

    v = jnp.concatenate(
        [jnp.where(keep[g:g + 1, :], choice[g * GROUP_SIZE:(g + 1) * GROUP_SIZE, :], -jnp.inf)
         for g in range(N_EXPERT_GROUPS)], axis=0)
    iota_e = lax.broadcasted_iota(jnp.int32, (N_EXPERTS, tr), 0)
    idxs, sels = [], []
    chosen = jnp.zeros((N_EXPERTS, tr), F32)
    for _ in range(TOP_K):
        _, am = _first_max(v, iota_e, N_EXPERTS)
        hit = iota_e == am
        idxs.append(am)
        sels.append(jnp.sum(jnp.where(hit, scores, 0.0), axis=0, keepdims=True))
        chosen = jnp.where(hit, 1.0, chosen)
        v = jnp.where(hit, -jnp.inf, v)
    sel = jnp.concatenate(sels, axis=0)
    idx_ref[...] = jnp.concatenate(idxs, axis=0)
    gate_ref[...] = sel / jnp.sum(sel, axis=0, keepdims=True) * ROUTED_SCALE

    earlier = _dot(chosen.astype(BF16), tri[...]) + carry[...]
    rank_ref[...] = jnp.concatenate(
        [jnp.sum(jnp.where(iota_e == am, earlier, 0.0), axis=0, keepdims=True) for am in idxs],
        axis=0).astype(jnp.int32)
    carry[...] = carry[...] + jnp.sum(chosen, axis=1, keepdims=True)
    cnt_ref[...] = jnp.broadcast_to(carry[...], cnt_ref.shape)


def _route(logits_t, b_router):
    N = logits_t.shape[1]
    tr = TM_ROUTE
    kt = pl.BlockSpec((TOP_K, tr), lambda i: (0, i))
    return pl.pallas_call(
        _route_kernel,
        out_shape=(
            jax.ShapeDtypeStruct((TOP_K, N), jnp.int32),
            jax.ShapeDtypeStruct((TOP_K, N), F32),
            jax.ShapeDtypeStruct((TOP_K, N), jnp.int32),
            jax.ShapeDtypeStruct((N_EXPERTS, LANES), F32),
        ),
        grid=(N // tr,),
        in_specs=[
            pl.BlockSpec((N_EXPERTS, tr), lambda i: (0, i)),
            pl.BlockSpec((N_EXPERTS, 1), lambda i: (0, 0)),
        ],
        out_specs=(kt, kt, kt, pl.BlockSpec((N_EXPERTS, LANES), lambda i: (0, 0))),
        scratch_shapes=[pltpu.VMEM((N_EXPERTS, 1), F32), pltpu.VMEM((tr, tr), BF16)],
        compiler_params=_params(1, 16),
        name="route",
    )(logits_t, b_router.reshape(N_EXPERTS, 1))


def _dest_kernel(idx_ref, rank_ref, ps_ref, dest_ref):
    tr = idx_ref.shape[1]
    iota_e = lax.broadcasted_iota(jnp.int32, (N_EXPERTS, tr), 0)
    idx = idx_ref[...]
    start = jnp.concatenate(
        [jnp.sum(jnp.where(iota_e == idx[k:k + 1, :], ps_ref[...], 0.0), axis=0, keepdims=True)
         for k in range(TOP_K)], axis=0)
    dest_ref[...] = (start.astype(jnp.int32) + rank_ref[...]) * ROW_SPLIT


def _dest(idx_t, rank_t, pstart):
    N = idx_t.shape[1]
    tr = TM_ROUTE
    kt = pl.BlockSpec((TOP_K, tr), lambda i: (0, i))
    return pl.pallas_call(
        _dest_kernel,
        out_shape=jax.ShapeDtypeStruct((TOP_K, N), jnp.int32),
        grid=(N // tr,),
        in_specs=[kt, kt, pl.BlockSpec((N_EXPERTS, 1), lambda i: (0, 0))],
        out_specs=kt,
        compiler_params=_params(1, 8),
        name="dest",
    )(idx_t, rank_t, pstart.astype(F32).reshape(N_EXPERTS, 1))


def _row_copy(src, dst, sem):
    return pltpu.make_async_copy(src, dst, sem)


def _token_rows(start):
    return pl.ds(pl.multiple_of(start, ROW_SPLIT), ROW_SPLIT)


def _block_rows(block):
    return pl.ds(pl.multiple_of(block * BLOCK_SUBROWS, BLOCK_SUBROWS), BLOCK_SUBROWS)


def _disp_kernel(zb_ref, dest_ref, x_ref, xs_ref, xbuf, zbuf, sem_in, sem_out):
    i = pl.program_id(0)
    n = pl.num_programs(0)
    tile_rows = xbuf.shape[1]
    tm = tile_rows // ROW_SPLIT
    slot = lax.rem(i, DISP_RING)
    ahead = lax.rem(i + 1, DISP_RING)

    def load(tile, into):
        start = pl.multiple_of(tile * tile_rows, tile_rows)
        return _row_copy(x_ref.at[pl.ds(start, tile_rows)], xbuf.at[into], sem_in.at[into])

    def drain(of):
        for k in range(TOP_K):
            _row_copy(xbuf.at[of], xs_ref.at[pl.ds(0, tile_rows)], sem_out.at[of]).wait()

    @pl.when(i == 0)
    def _():
        zbuf[...] = jnp.zeros_like(zbuf)

        def zero(j, started):
            blk = zb_ref[j]

            @pl.when(blk >= 0)
            def _():
                _row_copy(zbuf, xs_ref.at[_block_rows(blk)], sem_out.at[0]).start()

            return started + jnp.where(blk >= 0, 1, 0)

        def done(j, carry):
            _row_copy(zbuf, xs_ref.at[_block_rows(0)], sem_out.at[0]).wait()
            return carry

        started = lax.fori_loop(0, zb_ref.shape[0], zero, 0)
        lax.fori_loop(0, started, done, 0)
        load(0, 0).start()

    @pl.when(i + 1 < n)
    def _():
        @pl.when(i + 1 >= DISP_RING)
        def _():
            drain(ahead)

        load(i + 1, ahead).start()

    load(i, slot).wait()

    def issue(t, carry):
        src = xbuf.at[slot, _token_rows(t * ROW_SPLIT)]
        base = t * TOP_K
        for k in range(TOP_K):
            _row_copy(src, xs_ref.at[_token_rows(dest_ref[base + k])], sem_out.at[slot]).start(priority=k % 2)
        return carry

    lax.fori_loop(0, tm, issue, 0)

    @pl.when(i == n - 1)
    def _():
        for s in range(DISP_RING):
            drain(s)


def _dispatch(zero_blocks, dest_flat, n2p, n_rows):
    N = n2p.shape[0] // ROW_SPLIT
    tm = TM_MOVE
    assert N // tm >= DISP_RING
    grid_spec = pltpu.PrefetchScalarGridSpec(
        num_scalar_prefetch=1,
        grid=(N // tm,),
        in_specs=[
            pl.BlockSpec((tm * TOP_K,), lambda i, zb: (i,), memory_space=pltpu.SMEM),
            pl.BlockSpec(memory_space=pl.ANY),
        ],
        out_specs=pl.BlockSpec(memory_space=pl.ANY),
        scratch_shapes=[
            pltpu.VMEM((DISP_RING, tm * ROW_SPLIT, LANES), jnp.uint32),
            pltpu.VMEM((BLOCK_SUBROWS, LANES), jnp.uint32),
            pltpu.SemaphoreType.DMA((DISP_RING,)),
            pltpu.SemaphoreType.DMA((DISP_RING,)),
        ],
    )
    return pl.pallas_call(
        _disp_kernel,
        out_shape=jax.ShapeDtypeStruct((n_rows * ROW_SPLIT, LANES), jnp.uint32),
        grid_spec=grid_spec,
        compiler_params=_params(1, 8),
        name="disp",
    )(zero_blocks, dest_flat, n2p)


def _experts_kernel(eb_ref, bv_ref, xs_ref, wg_ref, wu_ref, wd_ref, ys_ref,
                    xbuf, ybuf, wgb, wub, wdb, sem_in, sem_out):
    e = pl.program_id(0)
    n_used = eb_ref[N_EXPERTS]
    n_blocks = ys_ref.shape[0] // BLOCK_SUBROWS
    first = eb_ref[e]
    end = eb_ref[e + 1]

    def in_copy(g, slot):
        return _row_copy(xs_ref.at[_block_rows(g)], xbuf.at[slot], sem_in.at[slot])

    def out_copy(g, slot):
        return _row_copy(ybuf.at[slot], ys_ref.at[_block_rows(g)], sem_out.at[slot])

    @pl.when(e == 0)
    def _():
        for g in range(RING - 1):
            @pl.when(g < n_used)
            def _(g=g):
                in_copy(g, g).start()

    @pl.when(end > first)
    def _():
        wgb[...] = wg_ref[...].astype(BF16)
        wub[...] = wu_ref[...].astype(BF16)
        wdb[...] = wd_ref[...].astype(BF16)

        def block(g, carry):
            slot = g & (RING - 1)
            in_copy(g, slot).wait()

            @pl.when(g + RING - 1 < n_used)
            def _():
                in_copy(g + RING - 1, (g + RING - 1) & (RING - 1)).start()

            @pl.when(g >= RING)
            def _():
                out_copy(g - RING, slot).wait()

            lo, hi = _unpack_halves(_load_rows(xbuf.at[slot], ROW_BLOCK))
            live = lax.broadcasted_iota(jnp.int32, lo.shape, 0) < bv_ref[g]
            lo = jnp.where(live, lo, 0.0).astype(BF16)
            hi = jnp.where(live, hi, 0.0).astype(BF16)
            gate = _dot(lo, wgb[0:HALF, :]) + _dot(hi, wgb[HALF:, :])
            up = _dot(lo, wub[0:HALF, :]) + _dot(hi, wub[HALF:, :])
            act = (gate * jax.nn.sigmoid(gate)) * up
            _store_rows(ybuf.at[slot], _pack_halves(_dot(act.astype(BF16), wdb[...])))
            out_copy(g, slot).start()
            return carry

        lax.fori_loop(first, end, block, 0)

    @pl.when(e == N_EXPERTS - 1)
    def _():
        for back in range(1, RING + 1):
            @pl.when(n_used >= back)
            def _(back=back):
                out_copy(n_used - back, (n_used - back) & (RING - 1)).wait()

        xbuf[0] = jnp.zeros((BLOCK_SUBROWS, LANES), jnp.uint32)

        def zero(g, carry):
            _row_copy(xbuf.at[0], ys_ref.at[_block_rows(g)], sem_out.at[0]).start()
            return carry

        def done(g, carry):
            _row_copy(xbuf.at[0], ys_ref.at[_block_rows(g)], sem_out.at[0]).wait()
            return carry

        lax.fori_loop(n_used, n_blocks, zero, 0)
        lax.fori_loop(n_used, n_blocks, done, 0)


def _experts(expert_block, block_valid, xs, w_gate, w_up, w_down):
    def w_map(e, eb, bv):
        return (e, 0, 0)

    grid_spec = pltpu.PrefetchScalarGridSpec(
        num_scalar_prefetch=2,
        grid=(N_EXPERTS,),
        in_specs=[
            pl.BlockSpec(memory_space=pl.ANY),
            pl.BlockSpec((None, D_MODEL, EXPERT_DIM), w_map),
            pl.BlockSpec((None, D_MODEL, EXPERT_DIM), w_map),
            pl.BlockSpec((None, EXPERT_DIM, D_MODEL), w_map),
        ],
        out_specs=pl.BlockSpec(memory_space=pl.ANY),
        scratch_shapes=[
            pltpu.VMEM((RING, BLOCK_SUBROWS, LANES), jnp.uint32),
            pltpu.VMEM((RING, BLOCK_SUBROWS, LANES), jnp.uint32),
            pltpu.VMEM((D_MODEL, EXPERT_DIM), BF16),
            pltpu.VMEM((D_MODEL, EXPERT_DIM), BF16),
            pltpu.VMEM((EXPERT_DIM, D_MODEL), BF16),
            pltpu.SemaphoreType.DMA((RING,)),
            pltpu.SemaphoreType.DMA((RING,)),
        ],
    )
    return pl.pallas_call(
        _experts_kernel,
        out_shape=jax.ShapeDtypeStruct(xs.shape, jnp.uint32),
        grid_spec=grid_spec,
        compiler_params=_params(1, 24),
        name="experts",
    )(expert_block, block_valid, xs, w_gate, w_up, w_down)


def _comb_kernel(dest_ref, next_ref, ys_ref, hs_ref, gate_ref, mod_ref, gf_ref, o_ref, buf, sem):
    i = pl.program_id(0)
    tm = hs_ref.shape[0]
    slot = i & 1

    def gather(rows_ref, into):
        def issue(t, carry):
            base = t * TOP_K
            for k in range(TOP_K):
                _row_copy(ys_ref.at[_token_rows(rows_ref[base + k])],
                          buf.at[into, k, _token_rows(t * ROW_SPLIT)], sem.at[into]).start(priority=k % 2)
            return carry

        lax.fori_loop(0, tm, issue, 0)

    @pl.when(i == 0)
    def _():
        gather(dest_ref, 0)

    for into in range(2):
        @pl.when(jnp.logical_and(i + 1 < pl.num_programs(0), slot == 1 - into))
        def _(into=into):
            gather(next_ref, into)

    for k in range(TOP_K):
        _row_copy(ys_ref.at[pl.ds(0, tm * ROW_SPLIT)], buf.at[slot, k], sem.at[slot]).wait()

    for c in range(tm // COMB_CHUNK):
        tok = slice(c * COMB_CHUNK, (c + 1) * COMB_CHUNK)
        gates = gate_ref[tok, :]
        lo = jnp.zeros((COMB_CHUNK, HALF), F32)
        hi = jnp.zeros((COMB_CHUNK, HALF), F32)
        for k in range(TOP_K):
            piece = buf.at[slot, k, pl.ds(c * COMB_CHUNK * ROW_SPLIT, COMB_CHUNK * ROW_SPLIT)]
            lo_k, hi_k = _unpack_halves(_load_rows(piece, COMB_CHUNK))
            gk = gates[:, k:k + 1]
            lo = lo + gk * lo_k
            hi = hi + gk * hi_k
        routed = jnp.concatenate([lo, hi], axis=1)
        h2 = hs_ref[tok, :] + mod_ref[0, 5:6, :] * routed
        o_ref[tok, :] = _rms(h2, gf_ref[...])


def _combine(dest_flat, ys, hs2, gates_nk, mod3, g_final, seq):
    N = hs2.shape[0]
    tm = TM_MOVE
    per_seq = seq // tm
    n_tiles = N // tm
    return pl.pallas_call(
        _comb_kernel,
        out_shape=jax.ShapeDtypeStruct((N, D_MODEL), F32),
        grid=(n_tiles,),
        in_specs=[
            pl.BlockSpec((tm * TOP_K,), lambda i: (i,), memory_space=pltpu.SMEM),
            pl.BlockSpec((tm * TOP_K,), lambda i: (jnp.minimum(i + 1, n_tiles - 1),), memory_space=pltpu.SMEM),
            pl.BlockSpec(memory_space=pl.ANY),
            pl.BlockSpec((tm, D_MODEL), lambda i: (i, 0)),
            pl.BlockSpec((tm, TOP_K), lambda i: (i, 0)),
            pl.BlockSpec((1, 6, D_MODEL), lambda i: (i // per_seq, 0, 0)),
            pl.BlockSpec((1, D_MODEL), lambda i: (0, 0)),
        ],
        out_specs=pl.BlockSpec((tm, D_MODEL), lambda i: (i, 0)),
        scratch_shapes=[pltpu.VMEM((2, TOP_K, tm * ROW_SPLIT, LANES), jnp.uint32),
                        pltpu.SemaphoreType.DMA((2,))],
        compiler_params=_params(1, 32),
        name="comb",
    )(dest_flat, dest_flat, ys, hs2, gates_nk, mod3, g_final.reshape(1, D_MODEL))


def _layer(x, mod3, g_norm_mix, w_in, sgu_ln_gain, sgu_ln_bias, w_spatial, b_spatial, g_out_attn,
           g_out_sgu, w_out, g_norm_ffn, w_router, b_router, w_exp_gate, w_exp_up, w_exp_down,
           w_sh_gate, w_sh_up, w_sh_down, g_final):
    B, S, _ = x.shape
    N = B * S

    q, k, v, u, z = _inproj(x, mod3, g_norm_mix, w_in.astype(BF16))
    head_of_lane = jnp.arange(ATTN_WIDTH) // HEAD_DIM
    slopes = jnp.exp2(-8.0 * jnp.arange(1, ATTN_HEADS + 1, dtype=F32) / ATTN_HEADS)
    slopes_lane = slopes[head_of_lane].reshape(ATTN_WIDTH // LANES, 1, LANES)
    attn = _attention(q, k, v, slopes_lane)

    hs, n2p, logits_t = _mix(
        attn, u, z, x, mod3, g_out_attn, g_out_sgu, sgu_ln_gain, sgu_ln_bias, w_spatial, b_spatial.T,
        w_out.astype(BF16), g_norm_ffn, w_router.T.astype(BF16),
        jnp.concatenate([w_sh_gate, w_sh_up], axis=1).astype(BF16), w_sh_down.astype(BF16))

    idx_t, gates_t, rank_t, counts = _route(logits_t, b_router)

    n_blocks = (N * TOP_K + N_EXPERTS * (ROW_BLOCK - 1)) // ROW_BLOCK
    cnt = counts[:, 0].astype(jnp.int32)
    padded = (cnt + ROW_BLOCK - 1) // ROW_BLOCK * ROW_BLOCK
    pends = jnp.cumsum(padded)
    pstart = pends - padded
    block_row = jnp.arange(n_blocks, dtype=jnp.int32) * ROW_BLOCK
    block_e = jnp.sum(pends[None, :] <= block_row[:, None], axis=1, dtype=jnp.int32)
    block_e = jnp.minimum(block_e, N_EXPERTS - 1)
    own = block_e[:, None] == jnp.arange(N_EXPERTS, dtype=jnp.int32)[None, :]
    live_end = jnp.sum(jnp.where(own, (pstart + cnt)[None, :], 0), axis=1)
    block_valid = jnp.clip(live_end - block_row, 0, ROW_BLOCK).astype(jnp.int32)
    n_used = (pends[-1:] // ROW_BLOCK).astype(jnp.int32)
    expert_block = jnp.concatenate([pstart // ROW_BLOCK, n_used]).astype(jnp.int32)

    last_block = jnp.where(cnt > 0, pends // ROW_BLOCK - 1, -1)
    tail_block = n_used[0] + jnp.arange(n_blocks - N * TOP_K // ROW_BLOCK)
    tail_block = jnp.where(tail_block < n_blocks, tail_block, -1)
    zero_blocks = jnp.concatenate([last_block, tail_block]).astype(jnp.int32)

    dest_flat = _dest(idx_t, rank_t, pstart).T.reshape(N * TOP_K)
    xs = _dispatch(zero_blocks, dest_flat, n2p, n_blocks * ROW_BLOCK)
    ys = _experts(expert_block, block_valid, xs, w_exp_gate, w_exp_up, w_exp_down)
    out = _combine(dest_flat, ys, hs.reshape(N, D_MODEL), gates_t.T, mod3, g_final, S)
    return out.reshape(B, S, D_MODEL)


def kernel(x, c, w_ada, b_ada, g_norm_mix, w_in, sgu_ln_gain, sgu_ln_bias, w_spatial, b_spatial, g_out_attn, g_out_sgu, w_out, g_norm_ffn, w_router, b_router, w_exp_gate, w_exp_up, w_exp_down, w_sh_gate, w_sh_up, w_sh_down, g_final):
    assert w_ada.shape[0] == 1, "single-layer stack"
    B = x.shape[0]
    mod3 = _ada(c, w_ada[0], b_ada[0]).reshape(B, 6, D_MODEL)
    return _layer(x, mod3, g_norm_mix[0], w_in[0], sgu_ln_gain[0], sgu_ln_bias[0], w_spatial[0],
                  b_spatial[0], g_out_attn[0], g_out_sgu[0], w_out[0], g_norm_ffn[0], w_router[0],
                  b_router[0], w_exp_gate[0], w_exp_up[0], w_exp_down[0], w_sh_gate[0], w_sh_up[0],
                  w_sh_down[0], g_final)
```

```python
import functools

import jax
import jax.numpy as jnp
from jax import lax
from jax.experimental import pallas as pl
from jax.experimental.pallas import tpu as pltpu

F32 = jnp.float32
BF16 = jnp.bfloat16

D_MODEL = 1024
ATTN_WIDTH = 512
ATTN_HEADS = 8
HEAD_DIM = 64
SGU_WIDTH = 512
SGU_GROUPS = 4
SGU_GROUP_DIM = 128
CHUNK = 128
DILATED_BRANCHES = ((128, 1), (512, 4), (2048, 16))
ATTN_BLOCK = 128
N_EXPERTS = 256
TOP_K = 8
N_EXPERT_GROUPS = 8
GROUP_SIZE = N_EXPERTS // N_EXPERT_GROUPS
TOPK_GROUPS = 4
EXPERT_DIM = 256
ROUTED_SCALE = 2.5
EPS = 1e-6

LANES = 128
HALF = D_MODEL // 2
ROW_BLOCK = 512
NEG_BIG = -1e30

TM_PROJ = 1024
TM_ROUTE = 512
TM_MOVE = 1024
DISP_RING = 3
COMB_CHUNK = 128
ATTN_UNROLL = 16
RING = 4


def _dot(a, b):
    return jnp.dot(a, b, preferred_element_type=F32)


def _dot_nt(a, b):
    return lax.dot_general(a, b, (((1,), (1,)), ((), ())), preferred_element_type=F32)


def _rms(x, g):
    return x * lax.rsqrt(jnp.mean(x * x, axis=-1, keepdims=True) + EPS) * g


def _pack_halves(x):
    return pltpu.pack_elementwise([x[:, :HALF], x[:, HALF:]], packed_dtype=BF16)


def _unpack_halves(w):
    lo = pltpu.unpack_elementwise(w, index=0, packed_dtype=BF16, unpacked_dtype=F32)
    hi = pltpu.unpack_elementwise(w, index=1, packed_dtype=BF16, unpacked_dtype=F32)
    return lo, hi


ROW_SPLIT = HALF // LANES
BLOCK_SUBROWS = ROW_BLOCK * ROW_SPLIT


def _store_rows(ref, packed):
    rows = packed.shape[0]
    for c in range(ROW_SPLIT):
        ref[pl.ds(c, rows, stride=ROW_SPLIT), :] = packed[:, c * LANES:(c + 1) * LANES]


def _load_rows(ref, rows):
    return jnp.concatenate([ref[pl.ds(c, rows, stride=ROW_SPLIT), :] for c in range(ROW_SPLIT)], axis=1)


def _params(n_axes, vmem_mib):
    return pltpu.CompilerParams(dimension_semantics=("arbitrary",) * n_axes,
                                vmem_limit_bytes=vmem_mib * 1024 * 1024)


def _ada_kernel(c_ref, w_ref, b_ref, o_ref):
    c = c_ref[...]
    cond = c * jax.nn.sigmoid(c)
    ch = cond.astype(BF16)
    cl = (cond - ch.astype(F32)).astype(BF16)
    w = w_ref[...]
    wh = w.astype(BF16)
    wl = (w - wh.astype(F32)).astype(BF16)
    o_ref[...] = _dot(ch, wh) + _dot(cl, wh) + _dot(ch, wl) + b_ref[...]


def _ada(c, w_ada, b_ada):
    B = c.shape[0]
    n_out = w_ada.shape[1]
    tn = D_MODEL
    return pl.pallas_call(
        _ada_kernel,
        out_shape=jax.ShapeDtypeStruct((B, n_out), F32),
        grid=(n_out // tn,),
        in_specs=[
            pl.BlockSpec((B, D_MODEL), lambda j: (0, 0)),
            pl.BlockSpec((D_MODEL, tn), lambda j: (0, j)),
            pl.BlockSpec((1, tn), lambda j: (0, j)),
        ],
        out_specs=pl.BlockSpec((B, tn), lambda j: (0, j)),
        compiler_params=_params(1, 16),
        name="ada",
    )(c, w_ada, b_ada.reshape(1, n_out))


def _inproj_kernel(x_ref, mod_ref, g_ref, w_ref, q_ref, k_ref, v_ref, u_ref, z_ref):
    x = x_ref[0]
    shift = mod_ref[0, 0:1, :]
    scale = mod_ref[0, 1:2, :]
    n = _rms(x, g_ref[...]) * (1.0 + scale) + shift
    p = _dot(n.astype(BF16), w_ref[...])
    q_ref[0] = (p[:, 0:ATTN_WIDTH] * (HEAD_DIM ** -0.5)).astype(BF16)
    k_ref[0] = p[:, ATTN_WIDTH:2 * ATTN_WIDTH].astype(BF16)
    v_ref[0] = p[:, 2 * ATTN_WIDTH:3 * ATTN_WIDTH].astype(BF16)
    u_ref[0] = p[:, 3 * ATTN_WIDTH:3 * ATTN_WIDTH + SGU_WIDTH].astype(BF16)
    z_ref[0] = p[:, 3 * ATTN_WIDTH + SGU_WIDTH:].astype(BF16)


def _inproj(x, mod3, g_norm, w_in_bf):
    B, S, _ = x.shape
    tm = TM_PROJ
    n_in = w_in_bf.shape[1]
    out = jax.ShapeDtypeStruct((B, S, ATTN_WIDTH), BF16)
    tile = pl.BlockSpec((1, tm, ATTN_WIDTH), lambda b, i: (b, i, 0))
    return pl.pallas_call(
        _inproj_kernel,
        out_shape=(out,) * 5,
        grid=(B, S // tm),
        in_specs=[
            pl.BlockSpec((1, tm, D_MODEL), lambda b, i: (b, i, 0)),
            pl.BlockSpec((1, 6, D_MODEL), lambda b, i: (b, 0, 0)),
            pl.BlockSpec((1, D_MODEL), lambda b, i: (0, 0)),
            pl.BlockSpec((D_MODEL, n_in), lambda b, i: (0, 0)),
        ],
        out_specs=(tile,) * 5,
        compiler_params=_params(2, 32),
        name="inproj",
    )(x, mod3, g_norm.reshape(1, D_MODEL), w_in_bf)


def _attn_kernel(slope_ref, q_ref, k_ref, v_ref, o_ref, qf, kf, vf, oacc, lacc, tbl, *, seq):
    qf[...] = q_ref[0].astype(F32)
    kf[...] = k_ref[0].astype(F32)
    vf[...] = v_ref[0].astype(F32)

    lane = lax.broadcasted_iota(jnp.int32, (1, LANES), 1)
    head0 = lane < HEAD_DIM
    slopes = slope_ref[0]
    slope_h = (slopes[:, 0:1], slopes[:, HEAD_DIM:HEAD_DIM + 1])

    whole_class = [seq // dil == 2 * ATTN_BLOCK for _, dil in DILATED_BRANCHES]
    for bi, (window, dil) in enumerate(DILATED_BRANCHES):
        steps = window // dil
        if whole_class[bi]:
            back = (lax.broadcasted_iota(jnp.int32, (2 * ATTN_BLOCK, 2 * ATTN_BLOCK), 0)
                    - lax.broadcasted_iota(jnp.int32, (2 * ATTN_BLOCK, 2 * ATTN_BLOCK), 1))
            valid = (back >= 0) & (back <= steps)
            dist = (back * dil).astype(F32)
            for hh in range(2):
                tbl[bi, hh] = jnp.where(valid, -slope_h[hh] * dist, NEG_BIG)
            continue
        qr = lax.broadcasted_iota(jnp.int32, (ATTN_BLOCK, 2 * ATTN_BLOCK), 0)
        kc = lax.broadcasted_iota(jnp.int32, (ATTN_BLOCK, 2 * ATTN_BLOCK), 1)
        for var in range(2):
            back = qr - kc + var * ATTN_BLOCK
            valid = (back >= 0) & (back <= steps)
            dist = (back * dil).astype(F32)
            for hh in range(2):
                tbl[bi, var, hh * ATTN_BLOCK:(hh + 1) * ATTN_BLOCK, :] = jnp.where(
                    valid, -slope_h[hh] * dist, NEG_BIG)

    for bi, (window, dil) in enumerate(DILATED_BRANCHES):
        cls_len = seq // dil
        nb = cls_len // ATTN_BLOCK
        nb_shift = nb.bit_length() - 1
        whole = whole_class[bi]
        qn = 2 * ATTN_BLOCK if whole else ATTN_BLOCK
        n_units = dil if whole else dil * nb
        per_step = ATTN_UNROLL * ATTN_BLOCK // qn

        def rows(start, size, dil=dil):
            if dil == 1:
                return pl.ds(pl.multiple_of(start, ATTN_BLOCK), size)
            return pl.ds(start, size, stride=dil)

        def block(it, bi=bi, dil=dil, nb=nb, nb_shift=nb_shift, rows=rows, whole=whole, qn=qn):
            if whole:
                q_rows = rows(it, qn)
                k_rows = q_rows
                table = jnp.concatenate([tbl[bi, 0], tbl[bi, 1]], axis=0)
            else:
                r = lax.shift_right_logical(it, nb_shift)
                i = it & (nb - 1)
                var = jnp.minimum(i, 1)
                q_rows = rows(i * ATTN_BLOCK * dil + r, qn)
                k_rows = rows((i - var) * ATTN_BLOCK * dil + r, 2 * ATTN_BLOCK)
                table = tbl[bi, var]
            q2 = qf[q_rows, :]
            kb = kf[k_rows, :].astype(BF16)
            v2 = vf[k_rows, :]
            qs = jnp.concatenate([jnp.where(head0, q2, 0.0), jnp.where(head0, 0.0, q2)], axis=0)
            s = _dot_nt(qs.astype(BF16), kb) + table
            m = jnp.max(s, axis=-1, keepdims=True)
            p = jnp.exp(s - m)
            den = jnp.sum(p, axis=-1, keepdims=True)
            pb = p.astype(BF16)
            vs = jnp.concatenate([jnp.where(head0, v2, 0.0), jnp.where(head0, 0.0, v2)], axis=0)
            o = _dot(jnp.concatenate([pb[:qn], pb[qn:]], axis=1), vs.astype(BF16))
            den2 = jnp.where(head0, den[:qn], den[qn:])
            lse = jnp.where(head0, m[:qn], m[qn:]) + jnp.log(den2)
            return q_rows, o / den2, lse

        def body(step, carry, bi=bi, block=block, per_step=per_step):
            done = [block(step * per_step + j) for j in range(per_step)]
            if bi > 0:
                merged = []
                for q_rows, o, lse in done:
                    l_old = lacc[q_rows, :]
                    m2 = jnp.maximum(l_old, lse)
                    a = jnp.exp(l_old - m2)
                    b = jnp.exp(lse - m2)
                    tot = a + b
                    merged.append((q_rows, (oacc[q_rows, :] * a + o * b) / tot, m2 + jnp.log(tot)))
                done = merged
            for q_rows, o, lse in done:
                oacc[q_rows, :] = o
                lacc[q_rows, :] = lse
            return carry

        assert n_units % per_step == 0
        lax.fori_loop(0, n_units // per_step, body, 0)

    o_ref[0] = oacc[...].astype(BF16)


def _attention(q, k, v, slopes_lane):
    B, S, _ = q.shape
    for window, dil in DILATED_BRANCHES:
        assert window // dil <= ATTN_BLOCK
        cls_len = S // dil
        assert S % dil == 0 and cls_len % ATTN_BLOCK == 0 and cls_len >= 2 * ATTN_BLOCK
        assert (cls_len // ATTN_BLOCK) & (cls_len // ATTN_BLOCK - 1) == 0
    n_pairs = ATTN_WIDTH // LANES
    tile = pl.BlockSpec((1, S, LANES), lambda b, p: (b, 0, p))
    return pl.pallas_call(
        functools.partial(_attn_kernel, seq=S),
        out_shape=jax.ShapeDtypeStruct((B, S, ATTN_WIDTH), BF16),
        grid=(B, n_pairs),
        in_specs=[pl.BlockSpec((1, 1, LANES), lambda b, p: (p, 0, 0)), tile, tile, tile],
        out_specs=tile,
        scratch_shapes=[pltpu.VMEM((S, LANES), F32)] * 5
        + [pltpu.VMEM((len(DILATED_BRANCHES), 2, 2 * ATTN_BLOCK, 2 * ATTN_BLOCK), F32)],
        compiler_params=_params(2, 32),
        name="attn",
    )(slopes_lane, q, k, v)


def _mix_kernel(attn_ref, u_ref, z_ref, x_ref, mod_ref, ga_ref, gs_ref, lng_ref, lnb_ref, wsp_ref,
                bsp_ref, wout_ref, gffn_ref, wrt_ref, wsgu_ref, wsd_ref,
                hs_ref, n2p_ref, lg_ref):
    tm = x_ref.shape[1]
    nc = tm // CHUNK
    a_n = _rms(attn_ref[0].astype(F32), ga_ref[...])

    ug = jax.nn.gelu(u_ref[0].astype(F32))
    zg = jax.nn.gelu(z_ref[0].astype(F32))
    mu = jnp.mean(zg, axis=-1, keepdims=True)
    zc = zg - mu
    var = jnp.mean(zc * zc, axis=-1, keepdims=True)
    zb = (zc * lax.rsqrt(var + EPS) * lng_ref[...] + lnb_ref[...]).astype(BF16)

    row = lax.broadcasted_iota(jnp.int32, (CHUNK, CHUNK), 0)
    col = lax.broadcasted_iota(jnp.int32, (CHUNK, CHUNK), 1)
    per_group = []
    for g in range(SGU_GROUPS):
        wc = jnp.where(row >= col, wsp_ref[g], 0.0).astype(BF16)
        lanes = slice(g * SGU_GROUP_DIM, (g + 1) * SGU_GROUP_DIM)
        zcat = jnp.concatenate([zb[c * CHUNK:(c + 1) * CHUNK, lanes] for c in range(nc)], axis=1)
        per_group.append(_dot(wc, zcat) + bsp_ref[:, g:g + 1])
    mixed = jnp.concatenate(
        [jnp.concatenate([per_group[g][:, c * CHUNK:(c + 1) * CHUNK] for g in range(SGU_GROUPS)], axis=1)
         for c in range(nc)], axis=0)
    s_n = _rms(ug * mixed, gs_ref[...])

    mix = (_dot(a_n.astype(BF16), wout_ref[0:ATTN_WIDTH, :])
           + _dot(s_n.astype(BF16), wout_ref[ATTN_WIDTH:, :]))
    gate1 = mod_ref[0, 2:3, :]
    shift2 = mod_ref[0, 3:4, :]
    scale2 = mod_ref[0, 4:5, :]
    gate2 = mod_ref[0, 5:6, :]
    h1 = x_ref[0] + gate1 * mix
    n2 = _rms(h1, gffn_ref[...]) * (1.0 + scale2) + shift2
    n2b = n2.astype(BF16)

    lg_ref[...] = _dot_nt(wrt_ref[...], n2b)
    gu = _dot(n2b, wsgu_ref[...])
    gsh = gu[:, :EXPERT_DIM]
    act = (gsh * jax.nn.sigmoid(gsh)) * gu[:, EXPERT_DIM:]
    shared = _dot(act.astype(BF16), wsd_ref[...])
    hs_ref[0] = h1 + gate2 * shared
    _store_rows(n2p_ref, _pack_halves(n2))


def _mix(attn, u, z, x, mod3, g_out_attn, g_out_sgu, ln_g, ln_b, w_spatial, b_spatial_t, w_out_bf,
         g_norm_ffn, w_router_t_bf, w_sh_gu_bf, w_sh_d_bf):
    B, S, _ = x.shape
    tm = TM_PROJ
    nt = S // tm
    N = B * S
    half_tile = pl.BlockSpec((1, tm, ATTN_WIDTH), lambda b, i: (b, i, 0))
    full_tile = pl.BlockSpec((1, tm, D_MODEL), lambda b, i: (b, i, 0))

    def const(shape):
        return pl.BlockSpec(shape, lambda b, i: (0,) * len(shape))

    return pl.pallas_call(
        _mix_kernel,
        out_shape=(
            jax.ShapeDtypeStruct((B, S, D_MODEL), F32),
            jax.ShapeDtypeStruct((N * ROW_SPLIT, LANES), jnp.uint32),
            jax.ShapeDtypeStruct((N_EXPERTS, N), F32),
        ),
        grid=(B, nt),
        in_specs=[
            half_tile, half_tile, half_tile, full_tile,
            pl.BlockSpec((1, 6, D_MODEL), lambda b, i: (b, 0, 0)),
            const((1, ATTN_WIDTH)), const((1, SGU_WIDTH)), const((1, SGU_WIDTH)), const((1, SGU_WIDTH)),
            const((SGU_GROUPS, CHUNK, CHUNK)), const((CHUNK, SGU_GROUPS)),
            const((ATTN_WIDTH + SGU_WIDTH, D_MODEL)), const((1, D_MODEL)),
            const((N_EXPERTS, D_MODEL)), const((D_MODEL, 2 * EXPERT_DIM)), const((EXPERT_DIM, D_MODEL)),
        ],
        out_specs=(
            full_tile,
            pl.BlockSpec((tm * ROW_SPLIT, LANES), lambda b, i: (b * nt + i, 0)),
            pl.BlockSpec((N_EXPERTS, tm), lambda b, i: (0, b * nt + i)),
        ),
        compiler_params=_params(2, 48),
        name="mix",
    )(attn, u, z, x, mod3, g_out_attn.reshape(1, -1), g_out_sgu.reshape(1, -1), ln_g.reshape(1, -1),
      ln_b.reshape(1, -1), w_spatial, b_spatial_t, w_out_bf, g_norm_ffn.reshape(1, -1), w_router_t_bf,
      w_sh_gu_bf, w_sh_d_bf)


def _first_max(v, iota, size):
    mx = jnp.max(v, axis=0, keepdims=True)
    am = jnp.min(jnp.where(v == mx, iota, size), axis=0, keepdims=True)
    return mx, am


def _route_kernel(lg_ref, br_ref, idx_ref, gate_ref, rank_ref, cnt_ref, carry, tri):
    step = pl.program_id(0)
    tr = lg_ref.shape[1]

    @pl.when(step == 0)
    def _():
        carry[...] = jnp.zeros_like(carry)
        before = (lax.broadcasted_iota(jnp.int32, (tr, tr), 0)
                  < lax.broadcasted_iota(jnp.int32, (tr, tr), 1))
        tri[...] = jnp.where(before, 1.0, 0.0).astype(BF16)

    scores = jax.nn.sigmoid(lg_ref[...])
    choice = scores + br_ref[...]

    iota_g = lax.broadcasted_iota(jnp.int32, (GROUP_SIZE, tr), 0)
    gs = []
    for g in range(N_EXPERT_GROUPS):
        cg = choice[g * GROUP_SIZE:(g + 1) * GROUP_SIZE, :]
        m1, am = _first_max(cg, iota_g, GROUP_SIZE)
        m2 = jnp.max(jnp.where(iota_g == am, -jnp.inf, cg), axis=0, keepdims=True)
        gs.append(m1 + m2)
    gscore = jnp.concatenate(gs, axis=0)

    iota_n = lax.broadcasted_iota(jnp.int32, (N_EXPERT_GROUPS, tr), 0)
    t = gscore
    for _ in range(TOPK_GROUPS - 1):
        _, am = _first_max(t, iota_n, N_EXPERT_GROUPS)
        t = jnp.where(iota_n == am, -jnp.inf, t)
    kth = jnp.max(t, axis=0, keepdims=True)
    keep = gscore >= kth

    v = jnp.concatenate(
        [jnp.where(keep[g:g + 1, :], choice[g * GROUP_SIZE:(g + 1) * GROUP_SIZE, :], -jnp.inf)
         for g in range(N_EXPERT_GROUPS)], axis=0)
    iota_e = lax.broadcasted_iota(jnp.int32, (N_EXPERTS, tr), 0)
    idxs, sels = [], []
    chosen = jnp.zeros((N_EXPERTS, tr), F32)
    for _ in range(TOP_K):
        _, am = _first_max(v, iota_e, N_EXPERTS)
        hit = iota_e == am
        idxs.append(am)
        sels.append(jnp.sum(jnp.where(hit, scores, 0.0), axis=0, keepdims=True))
        chosen = jnp.where(hit, 1.0, chosen)
        v = jnp.where(hit, -jnp.inf, v)
    sel = jnp.concatenate(sels, axis=0)
    idx_ref[...] = jnp.concatenate(idxs, axis=0)
    gate_ref[...] = sel / jnp.sum(sel, axis=0, keepdims=True) * ROUTED_SCALE

    earlier = _dot(chosen.astype(BF16), tri[...]) + carry[...]
    rank_ref[...] = jnp.concatenate(
        [jnp.sum(jnp.where(iota_e == am, earlier, 0.0), axis=0, keepdims=True) for am in idxs],
        axis=0).astype(jnp.int32)
    carry[...] = carry[...] + jnp.sum(chosen, axis=1, keepdims=True)
    cnt_ref[...] = jnp.broadcast_to(carry[...], cnt_ref.shape)


def _route(logits_t, b_router):
    N = logits_t.shape[1]
    tr = TM_ROUTE
    kt = pl.BlockSpec((TOP_K, tr), lambda i: (0, i))
    return pl.pallas_call(
        _route_kernel,
        out_shape=(
            jax.ShapeDtypeStruct((TOP_K, N), jnp.int32),
            jax.ShapeDtypeStruct((TOP_K, N), F32),
            jax.ShapeDtypeStruct((TOP_K, N), jnp.int32),
            jax.ShapeDtypeStruct((N_EXPERTS, LANES), F32),
        ),
        grid=(N // tr,),
        in_specs=[
            pl.BlockSpec((N_EXPERTS, tr), lambda i: (0, i)),
            pl.BlockSpec((N_EXPERTS, 1), lambda i: (0, 0)),
        ],
        out_specs=(kt, kt, kt, pl.BlockSpec((N_EXPERTS, LANES), lambda i: (0, 0))),
        scratch_shapes=[pltpu.VMEM((N_EXPERTS, 1), F32), pltpu.VMEM((tr, tr), BF16)],
        compiler_params=_params(1, 48),
        name="route",
    )(logits_t, b_router.reshape(N_EXPERTS, 1))


def _dest_kernel(idx_ref, rank_ref, ps_ref, dest_ref):
    tr = idx_ref.shape[1]
    iota_e = lax.broadcasted_iota(jnp.int32, (N_EXPERTS, tr), 0)
    idx = idx_ref[...]
    start = jnp.concatenate(
        [jnp.sum(jnp.where(iota_e == idx[k:k + 1, :], ps_ref[...], 0.0), axis=0, keepdims=True)
         for k in range(TOP_K)], axis=0)
    dest_ref[...] = (start.astype(jnp.int32) + rank_ref[...]) * ROW_SPLIT


def _dest(idx_t, rank_t, pstart):
    N = idx_t.shape[1]
    tr = TM_ROUTE
    kt = pl.BlockSpec((TOP_K, tr), lambda i: (0, i))
    return pl.pallas_call(
        _dest_kernel,
        out_shape=jax.ShapeDtypeStruct((TOP_K, N), jnp.int32),
        grid=(N // tr,),
        in_specs=[kt, kt, pl.BlockSpec((N_EXPERTS, 1), lambda i: (0, 0))],
        out_specs=kt,
        compiler_params=_params(1, 8),
        name="dest",
    )(idx_t, rank_t, pstart.astype(F32).reshape(N_EXPERTS, 1))


def _row_copy(src, dst, sem):
    return pltpu.make_async_copy(src, dst, sem)


def _token_rows(start):
    return pl.ds(pl.multiple_of(start, ROW_SPLIT), ROW_SPLIT)


def _block_rows(block):
    return pl.ds(pl.multiple_of(block * BLOCK_SUBROWS, BLOCK_SUBROWS), BLOCK_SUBROWS)


def _disp_kernel(zb_ref, dest_ref, x_ref, xs_ref, xbuf, zbuf, sem_in, sem_out):
    i = pl.program_id(0)
    n = pl.num_programs(0)
    tile_rows = xbuf.shape[1]
    tm = tile_rows // ROW_SPLIT
    slot = lax.rem(i, DISP_RING)
    ahead = lax.rem(i + 1, DISP_RING)

    def load(tile, into):
        start = pl.multiple_of(tile * tile_rows, tile_rows)
        return _row_copy(x_ref.at[pl.ds(start, tile_rows)], xbuf.at[into], sem_in.at[into])

    def drain(of):
        for k in range(TOP_K):
            _row_copy(xbuf.at[of], xs_ref.at[pl.ds(0, tile_rows)], sem_out.at[of]).wait()

    @pl.when(i == 0)
    def _():
        zbuf[...] = jnp.zeros_like(zbuf)

        def zero(j, started):
            blk = zb_ref[j]

            @pl.when(blk >= 0)
            def _():
                _row_copy(zbuf, xs_ref.at[_block_rows(blk)], sem_out.at[0]).start()

            return started + jnp.where(blk >= 0, 1, 0)

        def done(j, carry):
            _row_copy(zbuf, xs_ref.at[_block_rows(0)], sem_out.at[0]).wait()
            return carry

        started = lax.fori_loop(0, zb_ref.shape[0], zero, 0)
        lax.fori_loop(0, started, done, 0)
        load(0, 0).start()

    @pl.when(i + 1 < n)
    def _():
        @pl.when(i + 1 >= DISP_RING)
        def _():
            drain(ahead)

        load(i + 1, ahead).start()

    load(i, slot).wait()

    def issue(t, carry):
        src = xbuf.at[slot, _token_rows(t * ROW_SPLIT)]
        base = t * TOP_K
        for k in range(TOP_K):
            _row_copy(src, xs_ref.at[_token_rows(dest_ref[base + k])], sem_out.at[slot]).start(priority=k % 2)
        return carry

    lax.fori_loop(0, tm, issue, 0)

    @pl.when(i == n - 1)
    def _():
        for s in range(DISP_RING):
            drain(s)


def _dispatch(zero_blocks, dest_flat, n2p, n_rows):
    N = n2p.shape[0] // ROW_SPLIT
    tm = TM_MOVE
    assert N // tm >= DISP_RING
    grid_spec = pltpu.PrefetchScalarGridSpec(
        num_scalar_prefetch=1,
        grid=(N // tm,),
        in_specs=[
            pl.BlockSpec((tm * TOP_K,), lambda i, zb: (i,), memory_space=pltpu.SMEM),
            pl.BlockSpec(memory_space=pl.ANY),
        ],
        out_specs=pl.BlockSpec(memory_space=pl.ANY),
        scratch_shapes=[
            pltpu.VMEM((DISP_RING, tm * ROW_SPLIT, LANES), jnp.uint32),
            pltpu.VMEM((BLOCK_SUBROWS, LANES), jnp.uint32),
            pltpu.SemaphoreType.DMA((DISP_RING,)),
            pltpu.SemaphoreType.DMA((DISP_RING,)),
        ],
    )
    return pl.pallas_call(
        _disp_kernel,
        out_shape=jax.ShapeDtypeStruct((n_rows * ROW_SPLIT, LANES), jnp.uint32),
        grid_spec=grid_spec,
        compiler_params=_params(1, 16),
        name="disp",
    )(zero_blocks, dest_flat, n2p)


def _experts_kernel(eb_ref, bv_ref, xs_ref, wg_ref, wu_ref, wd_ref, ys_ref,
                    xbuf, ybuf, wgb, wub, wdb, sem_in, sem_out):
    e = pl.program_id(0)
    n_used = eb_ref[N_EXPERTS]
    n_blocks = ys_ref.shape[0] // BLOCK_SUBROWS
    first = eb_ref[e]
    end = eb_ref[e + 1]

    def in_copy(g, slot):
        return _row_copy(xs_ref.at[_block_rows(g)], xbuf.at[slot], sem_in.at[slot])

    def out_copy(g, slot):
        return _row_copy(ybuf.at[slot], ys_ref.at[_block_rows(g)], sem_out.at[slot])

    @pl.when(e == 0)
    def _():
        for g in range(RING - 1):
            @pl.when(g < n_used)
            def _(g=g):
                in_copy(g, g).start()

    @pl.when(end > first)
    def _():
        wgb[...] = wg_ref[...].astype(BF16)
        wub[...] = wu_ref[...].astype(BF16)
        wdb[...] = wd_ref[...].astype(BF16)

        def block(g, carry):
            slot = g & (RING - 1)
            in_copy(g, slot).wait()

            @pl.when(g + RING - 1 < n_used)
            def _():
                in_copy(g + RING - 1, (g + RING - 1) & (RING - 1)).start()

            @pl.when(g >= RING)
            def _():
                out_copy(g - RING, slot).wait()

            lo, hi = _unpack_halves(_load_rows(xbuf.at[slot], ROW_BLOCK))
            live = lax.broadcasted_iota(jnp.int32, lo.shape, 0) < bv_ref[g]
            lo = jnp.where(live, lo, 0.0).astype(BF16)
            hi = jnp.where(live, hi, 0.0).astype(BF16)
            gate = _dot(lo, wgb[0:HALF, :]) + _dot(hi, wgb[HALF:, :])
            up = _dot(lo, wub[0:HALF, :]) + _dot(hi, wub[HALF:, :])
            act = (gate * jax.nn.sigmoid(gate)) * up
            _store_rows(ybuf.at[slot], _pack_halves(_dot(act.astype(BF16), wdb[...])))
            out_copy(g, slot).start()
            return carry

        lax.fori_loop(first, end, block, 0)

    @pl.when(e == N_EXPERTS - 1)
    def _():
        for back in range(1, RING + 1):
            @pl.when(n_used >= back)
            def _(back=back):
                out_copy(n_used - back, (n_used - back) & (RING - 1)).wait()

        xbuf[0] = jnp.zeros((BLOCK_SUBROWS, LANES), jnp.uint32)

        def zero(g, carry):
            _row_copy(xbuf.at[0], ys_ref.at[_block_rows(g)], sem_out.at[0]).start()
            return carry

        def done(g, carry):
            _row_copy(xbuf.at[0], ys_ref.at[_block_rows(g)], sem_out.at[0]).wait()
            return carry

        lax.fori_loop(n_used, n_blocks, zero, 0)
        lax.fori_loop(n_used, n_blocks, done, 0)


def _experts(expert_block, block_valid, xs, w_gate, w_up, w_down):
    def w_map(e, eb, bv):
        return (e, 0, 0)

    grid_spec = pltpu.PrefetchScalarGridSpec(
        num_scalar_prefetch=2,
        grid=(N_EXPERTS,),
        in_specs=[
            pl.BlockSpec(memory_space=pl.ANY),
            pl.BlockSpec((None, D_MODEL, EXPERT_DIM), w_map),
            pl.BlockSpec((None, D_MODEL, EXPERT_DIM), w_map),
            pl.BlockSpec((None, EXPERT_DIM, D_MODEL), w_map),
        ],
        out_specs=pl.BlockSpec(memory_space=pl.ANY),
        scratch_shapes=[
            pltpu.VMEM((RING, BLOCK_SUBROWS, LANES), jnp.uint32),
            pltpu.VMEM((RING, BLOCK_SUBROWS, LANES), jnp.uint32),
            pltpu.VMEM((D_MODEL, EXPERT_DIM), BF16),
            pltpu.VMEM((D_MODEL, EXPERT_DIM), BF16),
            pltpu.VMEM((EXPERT_DIM, D_MODEL), BF16),
            pltpu.SemaphoreType.DMA((RING,)),
            pltpu.SemaphoreType.DMA((RING,)),
        ],
    )
    return pl.pallas_call(
        _experts_kernel,
        out_shape=jax.ShapeDtypeStruct(xs.shape, jnp.uint32),
        grid_spec=grid_spec,
        compiler_params=_params(1, 24),
        name="experts",
    )(expert_block, block_valid, xs, w_gate, w_up, w_down)


def _comb_kernel(dest_ref, next_ref, ys_ref, hs_ref, gate_ref, mod_ref, gf_ref, o_ref, buf, sem):
    i = pl.program_id(0)
    tm = hs_ref.shape[0]
    slot = i & 1

    def gather(rows_ref, into):
        def issue(t, carry):
            base = t * TOP_K
            for k in range(TOP_K):
                _row_copy(ys_ref.at[_token_rows(rows_ref[base + k])],
                          buf.at[into, k, _token_rows(t * ROW_SPLIT)], sem.at[into]).start(priority=k % 2)
            return carry

        lax.fori_loop(0, tm, issue, 0)

    @pl.when(i == 0)
    def _():
        gather(dest_ref, 0)

    for into in range(2):
        @pl.when(jnp.logical_and(i + 1 < pl.num_programs(0), slot == 1 - into))
        def _(into=into):
            gather(next_ref, into)

    for k in range(TOP_K):
        _row_copy(ys_ref.at[pl.ds(0, tm * ROW_SPLIT)], buf.at[slot, k], sem.at[slot]).wait()

    for c in range(tm // COMB_CHUNK):
        tok = slice(c * COMB_CHUNK, (c + 1) * COMB_CHUNK)
        gates = gate_ref[tok, :]
        lo = jnp.zeros((COMB_CHUNK, HALF), F32)
        hi = jnp.zeros((COMB_CHUNK, HALF), F32)
        for k in range(TOP_K):
            piece = buf.at[slot, k, pl.ds(c * COMB_CHUNK * ROW_SPLIT, COMB_CHUNK * ROW_SPLIT)]
            lo_k, hi_k = _unpack_halves(_load_rows(piece, COMB_CHUNK))
            gk = gates[:, k:k + 1]
            lo = lo + gk * lo_k
            hi = hi + gk * hi_k
        routed = jnp.concatenate([lo, hi], axis=1)
        h2 = hs_ref[tok, :] + mod_ref[0, 5:6, :] * routed
        o_ref[tok, :] = _rms(h2, gf_ref[...])


def _combine(dest_flat, ys, hs2, gates_nk, mod3, g_final, seq):
    N = hs2.shape[0]
    tm = TM_MOVE
    per_seq = seq // tm
    n_tiles = N // tm
    return pl.pallas_call(
        _comb_kernel,
        out_shape=jax.ShapeDtypeStruct((N, D_MODEL), F32),
        grid=(n_tiles,),
        in_specs=[
            pl.BlockSpec((tm * TOP_K,), lambda i: (i,), memory_space=pltpu.SMEM),
            pl.BlockSpec((tm * TOP_K,), lambda i: (jnp.minimum(i + 1, n_tiles - 1),), memory_space=pltpu.SMEM),
            pl.BlockSpec(memory_space=pl.ANY),
            pl.BlockSpec((tm, D_MODEL), lambda i: (i, 0)),
            pl.BlockSpec((tm, TOP_K), lambda i: (i, 0)),
            pl.BlockSpec((1, 6, D_MODEL), lambda i: (i // per_seq, 0, 0)),
            pl.BlockSpec((1, D_MODEL), lambda i: (0, 0)),
        ],
        out_specs=pl.BlockSpec((tm, D_MODEL), lambda i: (i, 0)),
        scratch_shapes=[pltpu.VMEM((2, TOP_K, tm * ROW_SPLIT, LANES), jnp.uint32),
                        pltpu.SemaphoreType.DMA((2,))],
        compiler_params=_params(1, 56),
        name="comb",
    )(dest_flat, dest_flat, ys, hs2, gates_nk, mod3, g_final.reshape(1, D_MODEL))


def _layer(x, mod3, g_norm_mix, w_in, sgu_ln_gain, sgu_ln_bias, w_spatial, b_spatial, g_out_attn,
           g_out_sgu, w_out, g_norm_ffn, w_router, b_router, w_exp_gate, w_exp_up, w_exp_down,
           w_sh_gate, w_sh_up, w_sh_down, g_final):
    B, S, _ = x.shape
    N = B * S

    q, k, v, u, z = _inproj(x, mod3, g_norm_mix, w_in.astype(BF16))
    head_of_lane = jnp.arange(ATTN_WIDTH) // HEAD_DIM
    slopes = jnp.exp2(-8.0 * jnp.arange(1, ATTN_HEADS + 1, dtype=F32) / ATTN_HEADS)
    slopes_lane = slopes[head_of_lane].reshape(ATTN_WIDTH // LANES, 1, LANES)
    attn = _attention(q, k, v, slopes_lane)

    hs, n2p, logits_t = _mix(
        attn, u, z, x, mod3, g_out_attn, g_out_sgu, sgu_ln_gain, sgu_ln_bias, w_spatial, b_spatial.T,
        w_out.astype(BF16), g_norm_ffn, w_router.T.astype(BF16),
        jnp.concatenate([w_sh_gate, w_sh_up], axis=1).astype(BF16), w_sh_down.astype(BF16))

    idx_t, gates_t, rank_t, counts = _route(logits_t, b_router)

    n_blocks = (N * TOP_K + N_EXPERTS * (ROW_BLOCK - 1)) // ROW_BLOCK
    cnt = counts[:, 0].astype(jnp.int32)
    padded = (cnt + ROW_BLOCK - 1) // ROW_BLOCK * ROW_BLOCK
    pends = jnp.cumsum(padded)
    pstart = pends - padded
    block_row = jnp.arange(n_blocks, dtype=jnp.int32) * ROW_BLOCK
    block_e = jnp.sum(pends[None, :] <= block_row[:, None], axis=1, dtype=jnp.int32)
    block_e = jnp.minimum(block_e, N_EXPERTS - 1)
    own = block_e[:, None] == jnp.arange(N_EXPERTS, dtype=jnp.int32)[None, :]
    live_end = jnp.sum(jnp.where(own, (pstart + cnt)[None, :], 0), axis=1)
    block_valid = jnp.clip(live_end - block_row, 0, ROW_BLOCK).astype(jnp.int32)
    n_used = (pends[-1:] // ROW_BLOCK).astype(jnp.int32)
    expert_block = jnp.concatenate([pstart // ROW_BLOCK, n_used]).astype(jnp.int32)

    last_block = jnp.where(cnt > 0, pends // ROW_BLOCK - 1, -1)
    tail_block = n_used[0] + jnp.arange(n_blocks - N * TOP_K // ROW_BLOCK)
    tail_block = jnp.where(tail_block < n_blocks, tail_block, -1)
    zero_blocks = jnp.concatenate([last_block, tail_block]).astype(jnp.int32)

    dest_flat = _dest(idx_t, rank_t, pstart).T.reshape(N * TOP_K)
    xs = _dispatch(zero_blocks, dest_flat, n2p, n_blocks * ROW_BLOCK)
    ys = _experts(expert_block, block_valid, xs, w_exp_gate, w_exp_up, w_exp_down)
    out = _combine(dest_flat, ys, hs.reshape(N, D_MODEL), gates_t.T, mod3, g_final, S)
    return out.reshape(B, S, D_MODEL)


def kernel(x, c, w_ada, b_ada, g_norm_mix, w_in, sgu_ln_gain, sgu_ln_bias, w_spatial, b_spatial, g_out_attn, g_out_sgu, w_out, g_norm_ffn, w_router, b_router, w_exp_gate, w_exp_up, w_exp_down, w_sh_gate, w_sh_up, w_sh_down, g_final):
    assert w_ada.shape[0] == 1, "single-layer stack"
    B = x.shape[0]
    mod3 = _ada(c, w_ada[0], b_ada[0]).reshape(B, 6, D_MODEL)
    return _layer(x, mod3, g_norm_mix[0], w_in[0], sgu_ln_gain[0], sgu_ln_bias[0], w_spatial[0],
                  b_spatial[0], g_out_attn[0], g_out_sgu[0], w_out[0], g_norm_ffn[0], w_router[0],
                  b_router[0], w_exp_gate[0], w_exp_up[0], w_exp_down[0], w_sh_gate[0], w_sh_up[0],
                  w_sh_down[0], g_final)
```

```python
import functools

import jax
import jax.numpy as jnp
from jax import lax
from jax.experimental import pallas as pl
from jax.experimental.pallas import tpu as pltpu

F32 = jnp.float32
BF16 = jnp.bfloat16

D_MODEL = 1024
ATTN_WIDTH = 512
ATTN_HEADS = 8
HEAD_DIM = 64
SGU_WIDTH = 512
SGU_GROUPS = 4
SGU_GROUP_DIM = 128
CHUNK = 128
DILATED_BRANCHES = ((128, 1), (512, 4), (2048, 16))
ATTN_BLOCK = 128
N_EXPERTS = 256
TOP_K = 8
N_EXPERT_GROUPS = 8
GROUP_SIZE = N_EXPERTS // N_EXPERT_GROUPS
TOPK_GROUPS = 4
EXPERT_DIM = 256
ROUTED_SCALE = 2.5
EPS = 1e-6

LANES = 128
HALF = D_MODEL // 2
ROW_BLOCK = 512
NEG_BIG = -1e30

TM_PROJ = 1024
TM_ROUTE = 512
TM_MOVE = 512
DISP_RING = 3
COMB_CHUNK = 128
ATTN_UNROLL = 16
RING = 8


def _dot(a, b):
    return jnp.dot(a, b, preferred_element_type=F32)


def _dot_nt(a, b):
    return lax.dot_general(a, b, (((1,), (1,)), ((), ())), preferred_element_type=F32)


def _rms(x, g):
    return x * lax.rsqrt(jnp.mean(x * x, axis=-1, keepdims=True) + EPS) * g


def _pack_halves(x):
    return pltpu.pack_elementwise([x[:, :HALF], x[:, HALF:]], packed_dtype=BF16)


def _unpack_halves(w):
    lo = pltpu.unpack_elementwise(w, index=0, packed_dtype=BF16, unpacked_dtype=F32)
    hi = pltpu.unpack_elementwise(w, index=1, packed_dtype=BF16, unpacked_dtype=F32)
    return lo, hi


ROW_SPLIT = HALF // LANES
BLOCK_SUBROWS = ROW_BLOCK * ROW_SPLIT


def _store_rows(ref, packed):
    rows = packed.shape[0]
    for c in range(ROW_SPLIT):
        ref[pl.ds(c, rows, stride=ROW_SPLIT), :] = packed[:, c * LANES:(c + 1) * LANES]


def _load_rows(ref, rows):
    return jnp.concatenate([ref[pl.ds(c, rows, stride=ROW_SPLIT), :] for c in range(ROW_SPLIT)], axis=1)


def _params(n_axes, vmem_mib):
    return pltpu.CompilerParams(dimension_semantics=("arbitrary",) * n_axes,
                                vmem_limit_bytes=vmem_mib * 1024 * 1024)


def _ada_kernel(c_ref, w_ref, b_ref, o_ref):
    c = c_ref[...]
    cond = c * jax.nn.sigmoid(c)
    ch = cond.astype(BF16)
    cl = (cond - ch.astype(F32)).astype(BF16)
    w = w_ref[...]
    wh = w.astype(BF16)
    wl = (w - wh.astype(F32)).astype(BF16)
    o_ref[...] = _dot(ch, wh) + _dot(cl, wh) + _dot(ch, wl) + b_ref[...]


def _ada(c, w_ada, b_ada):
    B = c.shape[0]
    n_out = w_ada.shape[1]
    tn = D_MODEL
    return pl.pallas_call(
        _ada_kernel,
        out_shape=jax.ShapeDtypeStruct((B, n_out), F32),
        grid=(n_out // tn,),
        in_specs=[
            pl.BlockSpec((B, D_MODEL), lambda j: (0, 0)),
            pl.BlockSpec((D_MODEL, tn), lambda j: (0, j)),
            pl.BlockSpec((1, tn), lambda j: (0, j)),
        ],
        out_specs=pl.BlockSpec((B, tn), lambda j: (0, j)),
        compiler_params=_params(1, 16),
        name="ada",
    )(c, w_ada, b_ada.reshape(1, n_out))


def _inproj_kernel(x_ref, mod_ref, g_ref, w_ref, q_ref, k_ref, v_ref, u_ref, z_ref):
    x = x_ref[0]
    shift = mod_ref[0, 0:1, :]
    scale = mod_ref[0, 1:2, :]
    n = _rms(x, g_ref[...]) * (1.0 + scale) + shift
    p = _dot(n.astype(BF16), w_ref[...])
    q_ref[0] = (p[:, 0:ATTN_WIDTH] * (HEAD_DIM ** -0.5)).astype(BF16)
    k_ref[0] = p[:, ATTN_WIDTH:2 * ATTN_WIDTH].astype(BF16)
    v_ref[0] = p[:, 2 * ATTN_WIDTH:3 * ATTN_WIDTH].astype(BF16)
    u_ref[0] = p[:, 3 * ATTN_WIDTH:3 * ATTN_WIDTH + SGU_WIDTH].astype(BF16)
    z_ref[0] = p[:, 3 * ATTN_WIDTH + SGU_WIDTH:].astype(BF16)


def _inproj(x, mod3, g_norm, w_in_bf):
    B, S, _ = x.shape
    tm = TM_PROJ
    n_in = w_in_bf.shape[1]
    out = jax.ShapeDtypeStruct((B, S, ATTN_WIDTH), BF16)
    tile = pl.BlockSpec((1, tm, ATTN_WIDTH), lambda b, i: (b, i, 0))
    return pl.pallas_call(
        _inproj_kernel,
        out_shape=(out,) * 5,
        grid=(B, S // tm),
        in_specs=[
            pl.BlockSpec((1, tm, D_MODEL), lambda b, i: (b, i, 0)),
            pl.BlockSpec((1, 6, D_MODEL), lambda b, i: (b, 0, 0)),
            pl.BlockSpec((1, D_MODEL), lambda b, i: (0, 0)),
            pl.BlockSpec((D_MODEL, n_in), lambda b, i: (0, 0)),
        ],
        out_specs=(tile,) * 5,
        compiler_params=_params(2, 32),
        name="inproj",
    )(x, mod3, g_norm.reshape(1, D_MODEL), w_in_bf)


def _attn_kernel(slope_ref, q_ref, k_ref, v_ref, o_ref, qf, kf, vf, oacc, lacc, tbl, *, seq):
    qf[...] = q_ref[0].astype(F32)
    kf[...] = k_ref[0].astype(F32)
    vf[...] = v_ref[0].astype(F32)

    lane = lax.broadcasted_iota(jnp.int32, (1, LANES), 1)
    head0 = lane < HEAD_DIM
    slopes = slope_ref[0]
    slope_h = (slopes[:, 0:1], slopes[:, HEAD_DIM:HEAD_DIM + 1])

    whole_class = [seq // dil == 2 * ATTN_BLOCK for _, dil in DILATED_BRANCHES]
    for bi, (window, dil) in enumerate(DILATED_BRANCHES):
        steps = window // dil
        if whole_class[bi]:
            back = (lax.broadcasted_iota(jnp.int32, (2 * ATTN_BLOCK, 2 * ATTN_BLOCK), 0)
                    - lax.broadcasted_iota(jnp.int32, (2 * ATTN_BLOCK, 2 * ATTN_BLOCK), 1))
            valid = (back >= 0) & (back <= steps)
            dist = (back * dil).astype(F32)
            for hh in range(2):
                tbl[bi, hh] = jnp.where(valid, -slope_h[hh] * dist, NEG_BIG)
            continue
        qr = lax.broadcasted_iota(jnp.int32, (ATTN_BLOCK, 2 * ATTN_BLOCK), 0)
        kc = lax.broadcasted_iota(jnp.int32, (ATTN_BLOCK, 2 * ATTN_BLOCK), 1)
        for var in range(2):
            back = qr - kc + var * ATTN_BLOCK
            valid = (back >= 0) & (back <= steps)
            dist = (back * dil).astype(F32)
            for hh in range(2):
                tbl[bi, var, hh * ATTN_BLOCK:(hh + 1) * ATTN_BLOCK, :] = jnp.where(
                    valid, -slope_h[hh] * dist, NEG_BIG)

    for bi, (window, dil) in enumerate(DILATED_BRANCHES):
        cls_len = seq // dil
        nb = cls_len // ATTN_BLOCK
        nb_shift = nb.bit_length() - 1
        whole = whole_class[bi]
        qn = 2 * ATTN_BLOCK if whole else ATTN_BLOCK
        n_units = dil if whole else dil * nb
        per_step = ATTN_UNROLL * ATTN_BLOCK // qn

        def rows(start, size, dil=dil):
            if dil == 1:
                return pl.ds(pl.multiple_of(start, ATTN_BLOCK), size)
            return pl.ds(start, size, stride=dil)

        def block(it, bi=bi, dil=dil, nb=nb, nb_shift=nb_shift, rows=rows, whole=whole, qn=qn):
            if whole:
                q_rows = rows(it, qn)
                k_rows = q_rows
                table = jnp.concatenate([tbl[bi, 0], tbl[bi, 1]], axis=0)
            else:
                r = lax.shift_right_logical(it, nb_shift)
                i = it & (nb - 1)
                var = jnp.minimum(i, 1)
                q_rows = rows(i * ATTN_BLOCK * dil + r, qn)
                k_rows = rows((i - var) * ATTN_BLOCK * dil + r, 2 * ATTN_BLOCK)
                table = tbl[bi, var]
            q2 = qf[q_rows, :]
            kb = kf[k_rows, :].astype(BF16)
            v2 = vf[k_rows, :]
            qs = jnp.concatenate([jnp.where(head0, q2, 0.0), jnp.where(head0, 0.0, q2)], axis=0)
            s = _dot_nt(qs.astype(BF16), kb) + table
            m = jnp.max(s, axis=-1, keepdims=True)
            p = jnp.exp(s - m)
            den = jnp.sum(p, axis=-1, keepdims=True)
            pb = p.astype(BF16)
            vs = jnp.concatenate([jnp.where(head0, v2, 0.0), jnp.where(head0, 0.0, v2)], axis=0)
            o = _dot(jnp.concatenate([pb[:qn], pb[qn:]], axis=1), vs.astype(BF16))
            den2 = jnp.where(head0, den[:qn], den[qn:])
            lse = jnp.where(head0, m[:qn], m[qn:]) + jnp.log(den2)
            return q_rows, o / den2, lse

        def body(step, carry, bi=bi, block=block, per_step=per_step):
            done = [block(step * per_step + j) for j in range(per_step)]
            if bi > 0:
                merged = []
                for q_rows, o, lse in done:
                    l_old = lacc[q_rows, :]
                    m2 = jnp.maximum(l_old, lse)
                    a = jnp.exp(l_old - m2)
                    b = jnp.exp(lse - m2)
                    tot = a + b
                    merged.append((q_rows, (oacc[q_rows, :] * a + o * b) / tot, m2 + jnp.log(tot)))
                done = merged
            for q_rows, o, lse in done:
                oacc[q_rows, :] = o
                lacc[q_rows, :] = lse
            return carry

        assert n_units % per_step == 0
        lax.fori_loop(0, n_units // per_step, body, 0)

    o_ref[0] = oacc[...].astype(BF16)


def _attention(q, k, v, slopes_lane):
    B, S, _ = q.shape
    for window, dil in DILATED_BRANCHES:
        assert window // dil <= ATTN_BLOCK
        cls_len = S // dil
        assert S % dil == 0 and cls_len % ATTN_BLOCK == 0 and cls_len >= 2 * ATTN_BLOCK
        assert (cls_len // ATTN_BLOCK) & (cls_len // ATTN_BLOCK - 1) == 0
    n_pairs = ATTN_WIDTH // LANES
    tile = pl.BlockSpec((1, S, LANES), lambda b, p: (b, 0, p))
    return pl.pallas_call(
        functools.partial(_attn_kernel, seq=S),
        out_shape=jax.ShapeDtypeStruct((B, S, ATTN_WIDTH), BF16),
        grid=(B, n_pairs),
        in_specs=[pl.BlockSpec((1, 1, LANES), lambda b, p: (p, 0, 0)), tile, tile, tile],
        out_specs=tile,
        scratch_shapes=[pltpu.VMEM((S, LANES), F32)] * 5
        + [pltpu.VMEM((len(DILATED_BRANCHES), 2, 2 * ATTN_BLOCK, 2 * ATTN_BLOCK), F32)],
        compiler_params=_params(2, 32),
        name="attn",
    )(slopes_lane, q, k, v)


def _mix_kernel(attn_ref, u_ref, z_ref, x_ref, mod_ref, ga_ref, gs_ref, lng_ref, lnb_ref, wsp_ref,
                bsp_ref, wout_ref, gffn_ref, wrt_ref, wsgu_ref, wsd_ref,
                hs_ref, n2p_ref, lg_ref):
    tm = x_ref.shape[1]
    nc = tm // CHUNK
    a_n = _rms(attn_ref[0].astype(F32), ga_ref[...])

    ug = jax.nn.gelu(u_ref[0].astype(F32))
    zg = jax.nn.gelu(z_ref[0].astype(F32))
    mu = jnp.mean(zg, axis=-1, keepdims=True)
    zc = zg - mu
    var = jnp.mean(zc * zc, axis=-1, keepdims=True)
    zb = (zc * lax.rsqrt(var + EPS) * lng_ref[...] + lnb_ref[...]).astype(BF16)

    row = lax.broadcasted_iota(jnp.int32, (CHUNK, CHUNK), 0)
    col = lax.broadcasted_iota(jnp.int32, (CHUNK, CHUNK), 1)
    per_group = []
    for g in range(SGU_GROUPS):
        wc = jnp.where(row >= col, wsp_ref[g], 0.0).astype(BF16)
        lanes = slice(g * SGU_GROUP_DIM, (g + 1) * SGU_GROUP_DIM)
        zcat = jnp.concatenate([zb[c * CHUNK:(c + 1) * CHUNK, lanes] for c in range(nc)], axis=1)
        per_group.append(_dot(wc, zcat) + bsp_ref[:, g:g + 1])
    mixed = jnp.concatenate(
        [jnp.concatenate([per_group[g][:, c * CHUNK:(c + 1) * CHUNK] for g in range(SGU_GROUPS)], axis=1)
         for c in range(nc)], axis=0)
    s_n = _rms(ug * mixed, gs_ref[...])

    mix = (_dot(a_n.astype(BF16), wout_ref[0:ATTN_WIDTH, :])
           + _dot(s_n.astype(BF16), wout_ref[ATTN_WIDTH:, :]))
    gate1 = mod_ref[0, 2:3, :]
    shift2 = mod_ref[0, 3:4, :]
    scale2 = mod_ref[0, 4:5, :]
    gate2 = mod_ref[0, 5:6, :]
    h1 = x_ref[0] + gate1 * mix
    n2 = _rms(h1, gffn_ref[...]) * (1.0 + scale2) + shift2
    n2b = n2.astype(BF16)

    lg_ref[...] = _dot_nt(wrt_ref[...], n2b)
    gu = _dot(n2b, wsgu_ref[...])
    gsh = gu[:, :EXPERT_DIM]
    act = (gsh * jax.nn.sigmoid(gsh)) * gu[:, EXPERT_DIM:]
    shared = _dot(act.astype(BF16), wsd_ref[...])
    hs_ref[0] = h1 + gate2 * shared
    _store_rows(n2p_ref, _pack_halves(n2))


def _mix(attn, u, z, x, mod3, g_out_attn, g_out_sgu, ln_g, ln_b, w_spatial, b_spatial_t, w_out_bf,
         g_norm_ffn, w_router_t_bf, w_sh_gu_bf, w_sh_d_bf):
    B, S, _ = x.shape
    tm = TM_PROJ
    nt = S // tm
    N = B * S
    half_tile = pl.BlockSpec((1, tm, ATTN_WIDTH), lambda b, i: (b, i, 0))
    full_tile = pl.BlockSpec((1, tm, D_MODEL), lambda b, i: (b, i, 0))

    def const(shape):
        return pl.BlockSpec(shape, lambda b, i: (0,) * len(shape))

    return pl.pallas_call(
        _mix_kernel,
        out_shape=(
            jax.ShapeDtypeStruct((B, S, D_MODEL), F32),
            jax.ShapeDtypeStruct((N * ROW_SPLIT, LANES), jnp.uint32),
            jax.ShapeDtypeStruct((N_EXPERTS, N), F32),
        ),
        grid=(B, nt),
        in_specs=[
            half_tile, half_tile, half_tile, full_tile,
            pl.BlockSpec((1, 6, D_MODEL), lambda b, i: (b, 0, 0)),
            const((1, ATTN_WIDTH)), const((1, SGU_WIDTH)), const((1, SGU_WIDTH)), const((1, SGU_WIDTH)),
            const((SGU_GROUPS, CHUNK, CHUNK)), const((CHUNK, SGU_GROUPS)),
            const((ATTN_WIDTH + SGU_WIDTH, D_MODEL)), const((1, D_MODEL)),
            const((N_EXPERTS, D_MODEL)), const((D_MODEL, 2 * EXPERT_DIM)), const((EXPERT_DIM, D_MODEL)),
        ],
        out_specs=(
            full_tile,
            pl.BlockSpec((tm * ROW_SPLIT, LANES), lambda b, i: (b * nt + i, 0)),
            pl.BlockSpec((N_EXPERTS, tm), lambda b, i: (0, b * nt + i)),
        ),
        compiler_params=_params(2, 48),
        name="mix",
    )(attn, u, z, x, mod3, g_out_attn.reshape(1, -1), g_out_sgu.reshape(1, -1), ln_g.reshape(1, -1),
      ln_b.reshape(1, -1), w_spatial, b_spatial_t, w_out_bf, g_norm_ffn.reshape(1, -1), w_router_t_bf,
      w_sh_gu_bf, w_sh_d_bf)


def _first_max(v, iota, size):
    mx = jnp.max(v, axis=0, keepdims=True)
    am = jnp.min(jnp.where(v == mx, iota, size), axis=0, keepdims=True)
    return mx, am


def _route_kernel(lg_ref, br_ref, idx_ref, gate_ref, rank_ref, cnt_ref, carry, tri):
    step = pl.program_id(0)
    tr = lg_ref.shape[1]

    @pl.when(step == 0)
    def _():
        carry[...] = jnp.zeros_like(carry)
        before = (lax.broadcasted_iota(jnp.int32, (tr, tr), 0)
                  < lax.broadcasted_iota(jnp.int32, (tr, tr), 1))
        tri[...] = jnp.where(before, 1.0, 0.0).astype(BF16)

    scores = jax.nn.sigmoid(lg_ref[...])
    choice = scores + br_ref[...]

    iota_g = lax.broadcasted_iota(jnp.int32, (GROUP_SIZE, tr), 0)
    gs = []
    for g in range(N_EXPERT_GROUPS):
        cg = choice[g * GROUP_SIZE:(g + 1) * GROUP_SIZE, :]
        m1, am = _first_max(cg, iota_g, GROUP_SIZE)
        m2 = jnp.max(jnp.where(iota_g == am, -jnp.inf, cg), axis=0, keepdims=True)
        gs.append(m1 + m2)
    gscore = jnp.concatenate(gs, axis=0)

    iota_n = lax.broadcasted_iota(jnp.int32, (N_EXPERT_GROUPS, tr), 0)
    t = gscore
    for _ in range(TOPK_GROUPS - 1):
        _, am = _first_max(t, iota_n, N_EXPERT_GROUPS)
        t = jnp.where(iota_n == am, -jnp.inf, t)
    kth = jnp.max(t, axis=0, keepdims=True)
    keep = gscore >= kth

    v = jnp.concatenate(
        [jnp.where(keep[g:g + 1, :], choice[g * GROUP_SIZE:(g + 1) * GROUP_SIZE, :], -jnp.inf)
         for g in range(N_EXPERT_GROUPS)], axis=0)
    iota_e = lax.broadcasted_iota(jnp.int32, (N_EXPERTS, tr), 0)
    idxs, sels = [], []
    chosen = jnp.zeros((N_EXPERTS, tr), F32)
    for _ in range(TOP_K):
        _, am = _first_max(v, iota_e, N_EXPERTS)
        hit = iota_e == am
        idxs.append(am)
        sels.append(jnp.sum(jnp.where(hit, scores, 0.0), axis=0, keepdims=True))
        chosen = jnp.where(hit, 1.0, chosen)
        v = jnp.where(hit, -jnp.inf, v)
    sel = jnp.concatenate(sels, axis=0)
    idx_ref[...] = jnp.concatenate(idxs, axis=0)
    gate_ref[...] = sel / jnp.sum(sel, axis=0, keepdims=True) * ROUTED_SCALE

    earlier = _dot(chosen.astype(BF16), tri[...]) + carry[...]
    rank_ref[...] = jnp.concatenate(
        [jnp.sum(jnp.where(iota_e == am, earlier, 0.0), axis=0, keepdims=True) for am in idxs],
        axis=0).astype(jnp.int32)
    carry[...] = carry[...] + jnp.sum(chosen, axis=1, keepdims=True)
    cnt_ref[...] = jnp.broadcast_to(carry[...], cnt_ref.shape)


def _route(logits_t, b_router):
    N = logits_t.shape[1]
    tr = TM_ROUTE
    kt = pl.BlockSpec((TOP_K, tr), lambda i: (0, i))
    return pl.pallas_call(
        _route_kernel,
        out_shape=(
            jax.ShapeDtypeStruct((TOP_K, N), jnp.int32),
            jax.ShapeDtypeStruct((TOP_K, N), F32),
            jax.ShapeDtypeStruct((TOP_K, N), jnp.int32),
            jax.ShapeDtypeStruct((N_EXPERTS, LANES), F32),
        ),
        grid=(N // tr,),
        in_specs=[
            pl.BlockSpec((N_EXPERTS, tr), lambda i: (0, i)),
            pl.BlockSpec((N_EXPERTS, 1), lambda i: (0, 0)),
        ],
        out_specs=(kt, kt, kt, pl.BlockSpec((N_EXPERTS, LANES), lambda i: (0, 0))),
        scratch_shapes=[pltpu.VMEM((N_EXPERTS, 1), F32), pltpu.VMEM((tr, tr), BF16)],
        compiler_params=_params(1, 16),
        name="route",
    )(logits_t, b_router.reshape(N_EXPERTS, 1))


def _dest_kernel(idx_ref, rank_ref, ps_ref, dest_ref):
    tr = idx_ref.shape[1]
    iota_e = lax.broadcasted_iota(jnp.int32, (N_EXPERTS, tr), 0)
    idx = idx_ref[...]
    start = jnp.concatenate(
        [jnp.sum(jnp.where(iota_e == idx[k:k + 1, :], ps_ref[...], 0.0), axis=0, keepdims=True)
         for k in range(TOP_K)], axis=0)
    dest_ref[...] = (start.astype(jnp.int32) + rank_ref[...]) * ROW_SPLIT


def _dest(idx_t, rank_t, pstart):
    N = idx_t.shape[1]
    tr = TM_ROUTE
    kt = pl.BlockSpec((TOP_K, tr), lambda i: (0, i))
    return pl.pallas_call(
        _dest_kernel,
        out_shape=jax.ShapeDtypeStruct((TOP_K, N), jnp.int32),
        grid=(N // tr,),
        in_specs=[kt, kt, pl.BlockSpec((N_EXPERTS, 1), lambda i: (0, 0))],
        out_specs=kt,
        compiler_params=_params(1, 8),
        name="dest",
    )(idx_t, rank_t, pstart.astype(F32).reshape(N_EXPERTS, 1))


def _row_copy(src, dst, sem):
    return pltpu.make_async_copy(src, dst, sem)


def _token_rows(start):
    return pl.ds(pl.multiple_of(start, ROW_SPLIT), ROW_SPLIT)


def _block_rows(block):
    return pl.ds(pl.multiple_of(block * BLOCK_SUBROWS, BLOCK_SUBROWS), BLOCK_SUBROWS)


def _disp_kernel(zb_ref, dest_ref, x_ref, xs_ref, xbuf, zbuf, sem_in, sem_out):
    i = pl.program_id(0)
    n = pl.num_programs(0)
    tile_rows = xbuf.shape[1]
    tm = tile_rows // ROW_SPLIT
    slot = lax.rem(i, DISP_RING)
    ahead = lax.rem(i + 1, DISP_RING)

    def load(tile, into):
        start = pl.multiple_of(tile * tile_rows, tile_rows)
        return _row_copy(x_ref.at[pl.ds(start, tile_rows)], xbuf.at[into], sem_in.at[into])

    def drain(of):
        for k in range(TOP_K):
            _row_copy(xbuf.at[of], xs_ref.at[pl.ds(0, tile_rows)], sem_out.at[of]).wait()

    @pl.when(i == 0)
    def _():
        zbuf[...] = jnp.zeros_like(zbuf)

        def zero(j, started):
            blk = zb_ref[j]

            @pl.when(blk >= 0)
            def _():
                _row_copy(zbuf, xs_ref.at[_block_rows(blk)], sem_out.at[0]).start()

            return started + jnp.where(blk >= 0, 1, 0)

        def done(j, carry):
            _row_copy(zbuf, xs_ref.at[_block_rows(0)], sem_out.at[0]).wait()
            return carry

        started = lax.fori_loop(0, zb_ref.shape[0], zero, 0)
        lax.fori_loop(0, started, done, 0)
        load(0, 0).start()

    @pl.when(i + 1 < n)
    def _():
        @pl.when(i + 1 >= DISP_RING)
        def _():
            drain(ahead)

        load(i + 1, ahead).start()

    load(i, slot).wait()

    def issue(t, carry):
        src = xbuf.at[slot, _token_rows(t * ROW_SPLIT)]
        base = t * TOP_K
        for k in range(TOP_K):
            _row_copy(src, xs_ref.at[_token_rows(dest_ref[base + k])], sem_out.at[slot]).start(priority=k % 2)
        return carry

    lax.fori_loop(0, tm, issue, 0)

    @pl.when(i == n - 1)
    def _():
        for s in range(DISP_RING):
            drain(s)


def _dispatch(zero_blocks, dest_flat, n2p, n_rows):
    N = n2p.shape[0] // ROW_SPLIT
    tm = TM_MOVE
    assert N // tm >= DISP_RING
    grid_spec = pltpu.PrefetchScalarGridSpec(
        num_scalar_prefetch=1,
        grid=(N // tm,),
        in_specs=[
            pl.BlockSpec((tm * TOP_K,), lambda i, zb: (i,), memory_space=pltpu.SMEM),
            pl.BlockSpec(memory_space=pl.ANY),
        ],
        out_specs=pl.BlockSpec(memory_space=pl.ANY),
        scratch_shapes=[
            pltpu.VMEM((DISP_RING, tm * ROW_SPLIT, LANES), jnp.uint32),
            pltpu.VMEM((BLOCK_SUBROWS, LANES), jnp.uint32),
            pltpu.SemaphoreType.DMA((DISP_RING,)),
            pltpu.SemaphoreType.DMA((DISP_RING,)),
        ],
    )
    return pl.pallas_call(
        _disp_kernel,
        out_shape=jax.ShapeDtypeStruct((n_rows * ROW_SPLIT, LANES), jnp.uint32),
        grid_spec=grid_spec,
        compiler_params=_params(1, 8),
        name="disp",
    )(zero_blocks, dest_flat, n2p)


def _experts_kernel(eb_ref, bv_ref, xs_ref, wg_ref, wu_ref, wd_ref, ys_ref,
                    xbuf, ybuf, wgb, wub, wdb, sem_in, sem_out):
    e = pl.program_id(0)
    n_used = eb_ref[N_EXPERTS]
    n_blocks = ys_ref.shape[0] // BLOCK_SUBROWS
    first = eb_ref[e]
    end = eb_ref[e + 1]

    def in_copy(g, slot):
        return _row_copy(xs_ref.at[_block_rows(g)], xbuf.at[slot], sem_in.at[slot])

    def out_copy(g, slot):
        return _row_copy(ybuf.at[slot], ys_ref.at[_block_rows(g)], sem_out.at[slot])

    @pl.when(e == 0)
    def _():
        for g in range(RING - 1):
            @pl.when(g < n_used)
            def _(g=g):
                in_copy(g, g).start()

    @pl.when(end > first)
    def _():
        wgb[...] = wg_ref[...].astype(BF16)
        wub[...] = wu_ref[...].astype(BF16)
        wdb[...] = wd_ref[...].astype(BF16)

        def block(g, carry):
            slot = g & (RING - 1)
            in_copy(g, slot).wait()

            @pl.when(g + RING - 1 < n_used)
            def _():
                in_copy(g + RING - 1, (g + RING - 1) & (RING - 1)).start()

            @pl.when(g >= RING)
            def _():
                out_copy(g - RING, slot).wait()

            lo, hi = _unpack_halves(_load_rows(xbuf.at[slot], ROW_BLOCK))
            live = lax.broadcasted_iota(jnp.int32, lo.shape, 0) < bv_ref[g]
            lo = jnp.where(live, lo, 0.0).astype(BF16)
            hi = jnp.where(live, hi, 0.0).astype(BF16)
            gate = _dot(lo, wgb[0:HALF, :]) + _dot(hi, wgb[HALF:, :])
            up = _dot(lo, wub[0:HALF, :]) + _dot(hi, wub[HALF:, :])
            act = (gate * jax.nn.sigmoid(gate)) * up
            _store_rows(ybuf.at[slot], _pack_halves(_dot(act.astype(BF16), wdb[...])))
            out_copy(g, slot).start()
            return carry

        lax.fori_loop(first, end, block, 0)

    @pl.when(e == N_EXPERTS - 1)
    def _():
        for back in range(1, RING + 1):
            @pl.when(n_used >= back)
            def _(back=back):
                out_copy(n_used - back, (n_used - back) & (RING - 1)).wait()

        xbuf[0] = jnp.zeros((BLOCK_SUBROWS, LANES), jnp.uint32)

        def zero(g, carry):
            _row_copy(xbuf.at[0], ys_ref.at[_block_rows(g)], sem_out.at[0]).start()
            return carry

        def done(g, carry):
            _row_copy(xbuf.at[0], ys_ref.at[_block_rows(g)], sem_out.at[0]).wait()
            return carry

        lax.fori_loop(n_used, n_blocks, zero, 0)
        lax.fori_loop(n_used, n_blocks, done, 0)


def _experts(expert_block, block_valid, xs, w_gate, w_up, w_down):
    def w_map(e, eb, bv):
        return (e, 0, 0)

    grid_spec = pltpu.PrefetchScalarGridSpec(
        num_scalar_prefetch=2,
        grid=(N_EXPERTS,),
        in_specs=[
            pl.BlockSpec(memory_space=pl.ANY),
            pl.BlockSpec((None, D_MODEL, EXPERT_DIM), w_map),
            pl.BlockSpec((None, D_MODEL, EXPERT_DIM), w_map),
            pl.BlockSpec((None, EXPERT_DIM, D_MODEL), w_map),
        ],
        out_specs=pl.BlockSpec(memory_space=pl.ANY),
        scratch_shapes=[
            pltpu.VMEM((RING, BLOCK_SUBROWS, LANES), jnp.uint32),
            pltpu.VMEM((RING, BLOCK_SUBROWS, LANES), jnp.uint32),
            pltpu.VMEM((D_MODEL, EXPERT_DIM), BF16),
            pltpu.VMEM((D_MODEL, EXPERT_DIM), BF16),
            pltpu.VMEM((EXPERT_DIM, D_MODEL), BF16),
            pltpu.SemaphoreType.DMA((RING,)),
            pltpu.SemaphoreType.DMA((RING,)),
        ],
    )
    return pl.pallas_call(
        _experts_kernel,
        out_shape=jax.ShapeDtypeStruct(xs.shape, jnp.uint32),
        grid_spec=grid_spec,
        compiler_params=_params(1, 32),
        name="experts",
    )(expert_block, block_valid, xs, w_gate, w_up, w_down)


def _comb_kernel(dest_ref, next_ref, ys_ref, hs_ref, gate_ref, mod_ref, gf_ref, o_ref, buf, sem):
    i = pl.program_id(0)
    tm = hs_ref.shape[0]
    slot = i & 1

    def gather(rows_ref, into):
        def issue(t, carry):
            base = t * TOP_K
            for k in range(TOP_K):
                _row_copy(ys_ref.at[_token_rows(rows_ref[base + k])],
                          buf.at[into, k, _token_rows(t * ROW_SPLIT)], sem.at[into]).start(priority=k % 2)
            return carry

        lax.fori_loop(0, tm, issue, 0)

    @pl.when(i == 0)
    def _():
        gather(dest_ref, 0)

    for into in range(2):
        @pl.when(jnp.logical_and(i + 1 < pl.num_programs(0), slot == 1 - into))
        def _(into=into):
            gather(next_ref, into)

    for k in range(TOP_K):
        _row_copy(ys_ref.at[pl.ds(0, tm * ROW_SPLIT)], buf.at[slot, k], sem.at[slot]).wait()

    for c in range(tm // COMB_CHUNK):
        tok = slice(c * COMB_CHUNK, (c + 1) * COMB_CHUNK)
        gates = gate_ref[tok, :]
        lo = jnp.zeros((COMB_CHUNK, HALF), F32)
        hi = jnp.zeros((COMB_CHUNK, HALF), F32)
        for k in range(TOP_K):
            piece = buf.at[slot, k, pl.ds(c * COMB_CHUNK * ROW_SPLIT, COMB_CHUNK * ROW_SPLIT)]
            lo_k, hi_k = _unpack_halves(_load_rows(piece, COMB_CHUNK))
            gk = gates[:, k:k + 1]
            lo = lo + gk * lo_k
            hi = hi + gk * hi_k
        routed = jnp.concatenate([lo, hi], axis=1)
        h2 = hs_ref[tok, :] + mod_ref[0, 5:6, :] * routed
        o_ref[tok, :] = _rms(h2, gf_ref[...])


def _combine(dest_flat, ys, hs2, gates_nk, mod3, g_final, seq):
    N = hs2.shape[0]
    tm = TM_MOVE
    per_seq = seq // tm
    n_tiles = N // tm
    return pl.pallas_call(
        _comb_kernel,
        out_shape=jax.ShapeDtypeStruct((N, D_MODEL), F32),
        grid=(n_tiles,),
        in_specs=[
            pl.BlockSpec((tm * TOP_K,), lambda i: (i,), memory_space=pltpu.SMEM),
            pl.BlockSpec((tm * TOP_K,), lambda i: (jnp.minimum(i + 1, n_tiles - 1),), memory_space=pltpu.SMEM),
            pl.BlockSpec(memory_space=pl.ANY),
            pl.BlockSpec((tm, D_MODEL), lambda i: (i, 0)),
            pl.BlockSpec((tm, TOP_K), lambda i: (i, 0)),
            pl.BlockSpec((1, 6, D_MODEL), lambda i: (i // per_seq, 0, 0)),
            pl.BlockSpec((1, D_MODEL), lambda i: (0, 0)),
        ],
        out_specs=pl.BlockSpec((tm, D_MODEL), lambda i: (i, 0)),
        scratch_shapes=[pltpu.VMEM((2, TOP_K, tm * ROW_SPLIT, LANES), jnp.uint32),
                        pltpu.SemaphoreType.DMA((2,))],
        compiler_params=_params(1, 32),
        name="comb",
    )(dest_flat, dest_flat, ys, hs2, gates_nk, mod3, g_final.reshape(1, D_MODEL))


def _layer(x, mod3, g_norm_mix, w_in, sgu_ln_gain, sgu_ln_bias, w_spatial, b_spatial, g_out_attn,
           g_out_sgu, w_out, g_norm_ffn, w_router, b_router, w_exp_gate, w_exp_up, w_exp_down,
           w_sh_gate, w_sh_up, w_sh_down, g_final):
    B, S, _ = x.shape
    N = B * S

    q, k, v, u, z = _inproj(x, mod3, g_norm_mix, w_in.astype(BF16))
    head_of_lane = jnp.arange(ATTN_WIDTH) // HEAD_DIM
    slopes = jnp.exp2(-8.0 * jnp.arange(1, ATTN_HEADS + 1, dtype=F32) / ATTN_HEADS)
    slopes_lane = slopes[head_of_lane].reshape(ATTN_WIDTH // LANES, 1, LANES)
    attn = _attention(q, k, v, slopes_lane)

    hs, n2p, logits_t = _mix(
        attn, u, z, x, mod3, g_out_attn, g_out_sgu, sgu_ln_gain, sgu_ln_bias, w_spatial, b_spatial.T,
        w_out.astype(BF16), g_norm_ffn, w_router.T.astype(BF16),
        jnp.concatenate([w_sh_gate, w_sh_up], axis=1).astype(BF16), w_sh_down.astype(BF16))

    idx_t, gates_t, rank_t, counts = _route(logits_t, b_router)

    n_blocks = (N * TOP_K + N_EXPERTS * (ROW_BLOCK - 1)) // ROW_BLOCK
    cnt = counts[:, 0].astype(jnp.int32)
    padded = (cnt + ROW_BLOCK - 1) // ROW_BLOCK * ROW_BLOCK
    pends = jnp.cumsum(padded)
    pstart = pends - padded
    block_row = jnp.arange(n_blocks, dtype=jnp.int32) * ROW_BLOCK
    block_e = jnp.sum(pends[None, :] <= block_row[:, None], axis=1, dtype=jnp.int32)
    block_e = jnp.minimum(block_e, N_EXPERTS - 1)
    own = block_e[:, None] == jnp.arange(N_EXPERTS, dtype=jnp.int32)[None, :]
    live_end = jnp.sum(jnp.where(own, (pstart + cnt)[None, :], 0), axis=1)
    block_valid = jnp.clip(live_end - block_row, 0, ROW_BLOCK).astype(jnp.int32)
    n_used = (pends[-1:] // ROW_BLOCK).astype(jnp.int32)
    expert_block = jnp.concatenate([pstart // ROW_BLOCK, n_used]).astype(jnp.int32)

    last_block = jnp.where(cnt > 0, pends // ROW_BLOCK - 1, -1)
    tail_block = n_used[0] + jnp.arange(n_blocks - N * TOP_K // ROW_BLOCK)
    tail_block = jnp.where(tail_block < n_blocks, tail_block, -1)
    zero_blocks = jnp.concatenate([last_block, tail_block]).astype(jnp.int32)

    dest_flat = _dest(idx_t, rank_t, pstart).T.reshape(N * TOP_K)
    xs = _dispatch(zero_blocks, dest_flat, n2p, n_blocks * ROW_BLOCK)
    ys = _experts(expert_block, block_valid, xs, w_exp_gate, w_exp_up, w_exp_down)
    out = _combine(dest_flat, ys, hs.reshape(N, D_MODEL), gates_t.T, mod3, g_final, S)
    return out.reshape(B, S, D_MODEL)


def kernel(x, c, w_ada, b_ada, g_norm_mix, w_in, sgu_ln_gain, sgu_ln_bias, w_spatial, b_spatial, g_out_attn, g_out_sgu, w_out, g_norm_ffn, w_router, b_router, w_exp_gate, w_exp_up, w_exp_down, w_sh_gate, w_sh_up, w_sh_down, g_final):
    assert w_ada.shape[0] == 1, "single-layer stack"
    B = x.shape[0]
    mod3 = _ada(c, w_ada[0], b_ada[0]).reshape(B, 6, D_MODEL)
    return _layer(x, mod3, g_norm_mix[0], w_in[0], sgu_ln_gain[0], sgu_ln_bias[0], w_spatial[0],
                  b_spatial[0], g_out_attn[0], g_out_sgu[0], w_out[0], g_norm_ffn[0], w_router[0],
                  b_router[0], w_exp_gate[0], w_exp_up[0], w_exp_down[0], w_sh_gate[0], w_sh_up[0],
                  w_sh_down[0], g_final)
```

```python
import functools

import jax
import jax.numpy as jnp
from jax import lax
from jax.experimental import pallas as pl
from jax.experimental.pallas import tpu as pltpu

F32 = jnp.float32
BF16 = jnp.bfloat16

D_MODEL = 1024
ATTN_WIDTH = 512
ATTN_HEADS = 8
HEAD_DIM = 64
SGU_WIDTH = 512
SGU_GROUPS = 4
SGU_GROUP_DIM = 128
CHUNK = 128
DILATED_BRANCHES = ((128, 1), (512, 4), (2048, 16))
ATTN_BLOCK = 128
N_EXPERTS = 256
TOP_K = 8
N_EXPERT_GROUPS = 8
GROUP_SIZE = N_EXPERTS // N_EXPERT_GROUPS
TOPK_GROUPS = 4
EXPERT_DIM = 256
ROUTED_SCALE = 2.5
EPS = 1e-6

LANES = 128
HALF = D_MODEL // 2
ROW_BLOCK = 512
NEG_BIG = -1e30

TM_PROJ = 1024
TM_ROUTE = 512
TM_MOVE = 512
DISP_RING = 3
COMB_CHUNK = 128
ATTN_UNROLL = 16
RING = 4


def _dot(a, b):
    return jnp.dot(a, b, preferred_element_type=F32)


def _dot_nt(a, b):
    return lax.dot_general(a, b, (((1,), (1,)), ((), ())), preferred_element_type=F32)


def _rms(x, g):
    return x * lax.rsqrt(jnp.mean(x * x, axis=-1, keepdims=True) + EPS) * g


def _pack_halves(x):
    return pltpu.pack_elementwise([x[:, :HALF], x[:, HALF:]], packed_dtype=BF16)


def _unpack_halves(w):
    lo = pltpu.unpack_elementwise(w, index=0, packed_dtype=BF16, unpacked_dtype=F32)
    hi = pltpu.unpack_elementwise(w, index=1, packed_dtype=BF16, unpacked_dtype=F32)
    return lo, hi


ROW_SPLIT = HALF // LANES
BLOCK_SUBROWS = ROW_BLOCK * ROW_SPLIT


def _store_rows(ref, packed):
    rows = packed.shape[0]
    for c in range(ROW_SPLIT):
        ref[pl.ds(c, rows, stride=ROW_SPLIT), :] = packed[:, c * LANES:(c + 1) * LANES]


def _load_rows(ref, rows):
    return jnp.concatenate([ref[pl.ds(c, rows, stride=ROW_SPLIT), :] for c in range(ROW_SPLIT)], axis=1)


def _params(n_axes, vmem_mib):
    return pltpu.CompilerParams(dimension_semantics=("arbitrary",) * n_axes,
                                vmem_limit_bytes=vmem_mib * 1024 * 1024)


def _ada_kernel(c_ref, w_ref, b_ref, o_ref):
    c = c_ref[...]
    cond = c * jax.nn.sigmoid(c)
    ch = cond.astype(BF16)
    cl = (cond - ch.astype(F32)).astype(BF16)
    w = w_ref[...]
    wh = w.astype(BF16)
    wl = (w - wh.astype(F32)).astype(BF16)
    o_ref[...] = _dot(ch, wh) + _dot(cl, wh) + _dot(ch, wl) + b_ref[...]


def _ada(c, w_ada, b_ada):
    B = c.shape[0]
    n_out = w_ada.shape[1]
    tn = D_MODEL
    return pl.pallas_call(
        _ada_kernel,
        out_shape=jax.ShapeDtypeStruct((B, n_out), F32),
        grid=(n_out // tn,),
        in_specs=[
            pl.BlockSpec((B, D_MODEL), lambda j: (0, 0)),
            pl.BlockSpec((D_MODEL, tn), lambda j: (0, j)),
            pl.BlockSpec((1, tn), lambda j: (0, j)),
        ],
        out_specs=pl.BlockSpec((B, tn), lambda j: (0, j)),
        compiler_params=_params(1, 16),
        name="ada",
    )(c, w_ada, b_ada.reshape(1, n_out))


def _inproj_kernel(x_ref, mod_ref, g_ref, w_ref, q_ref, k_ref, v_ref, u_ref, z_ref):
    x = x_ref[0]
    shift = mod_ref[0, 0:1, :]
    scale = mod_ref[0, 1:2, :]
    n = _rms(x, g_ref[...]) * (1.0 + scale) + shift
    p = _dot(n.astype(BF16), w_ref[...])
    q_ref[0] = (p[:, 0:ATTN_WIDTH] * (HEAD_DIM ** -0.5)).astype(BF16)
    k_ref[0] = p[:, ATTN_WIDTH:2 * ATTN_WIDTH].astype(BF16)
    v_ref[0] = p[:, 2 * ATTN_WIDTH:3 * ATTN_WIDTH].astype(BF16)
    u_ref[0] = p[:, 3 * ATTN_WIDTH:3 * ATTN_WIDTH + SGU_WIDTH].astype(BF16)
    z_ref[0] = p[:, 3 * ATTN_WIDTH + SGU_WIDTH:].astype(BF16)


def _inproj(x, mod3, g_norm, w_in_bf):
    B, S, _ = x.shape
    tm = TM_PROJ
    n_in = w_in_bf.shape[1]
    out = jax.ShapeDtypeStruct((B, S, ATTN_WIDTH), BF16)
    tile = pl.BlockSpec((1, tm, ATTN_WIDTH), lambda b, i: (b, i, 0))
    return pl.pallas_call(
        _inproj_kernel,
        out_shape=(out,) * 5,
        grid=(B, S // tm),
        in_specs=[
            pl.BlockSpec((1, tm, D_MODEL), lambda b, i: (b, i, 0)),
            pl.BlockSpec((1, 6, D_MODEL), lambda b, i: (b, 0, 0)),
            pl.BlockSpec((1, D_MODEL), lambda b, i: (0, 0)),
            pl.BlockSpec((D_MODEL, n_in), lambda b, i: (0, 0)),
        ],
        out_specs=(tile,) * 5,
        compiler_params=_params(2, 32),
        name="inproj",
    )(x, mod3, g_norm.reshape(1, D_MODEL), w_in_bf)


def _attn_kernel(slope_ref, q_ref, k_ref, v_ref, o_ref, qf, kf, vf, oacc, lacc, tbl, *, seq):
    qf[...] = q_ref[0].astype(F32)
    kf[...] = k_ref[0].astype(F32)
    vf[...] = v_ref[0].astype(F32)

    lane = lax.broadcasted_iota(jnp.int32, (1, LANES), 1)
    head0 = lane < HEAD_DIM
    slopes = slope_ref[0]
    slope_h = (slopes[:, 0:1], slopes[:, HEAD_DIM:HEAD_DIM + 1])

    whole_class = [seq // dil == 2 * ATTN_BLOCK for _, dil in DILATED_BRANCHES]
    for bi, (window, dil) in enumerate(DILATED_BRANCHES):
        steps = window // dil
        if whole_class[bi]:
            back = (lax.broadcasted_iota(jnp.int32, (2 * ATTN_BLOCK, 2 * ATTN_BLOCK), 0)
                    - lax.broadcasted_iota(jnp.int32, (2 * ATTN_BLOCK, 2 * ATTN_BLOCK), 1))
            valid = (back >= 0) & (back <= steps)
            dist = (back * dil).astype(F32)
            for hh in range(2):
                tbl[bi, hh] = jnp.where(valid, -slope_h[hh] * dist, NEG_BIG)
            continue
        qr = lax.broadcasted_iota(jnp.int32, (ATTN_BLOCK, 2 * ATTN_BLOCK), 0)
        kc = lax.broadcasted_iota(jnp.int32, (ATTN_BLOCK, 2 * ATTN_BLOCK), 1)
        for var in range(2):
            back = qr - kc + var * ATTN_BLOCK
            valid = (back >= 0) & (back <= steps)
            dist = (back * dil).astype(F32)
            for hh in range(2):
                tbl[bi, var, hh * ATTN_BLOCK:(hh + 1) * ATTN_BLOCK, :] = jnp.where(
                    valid, -slope_h[hh] * dist, NEG_BIG)

    for bi, (window, dil) in enumerate(DILATED_BRANCHES):
        cls_len = seq // dil
        nb = cls_len // ATTN_BLOCK
        nb_shift = nb.bit_length() - 1
        whole = whole_class[bi]
        qn = 2 * ATTN_BLOCK if whole else ATTN_BLOCK
        n_units = dil if whole else dil * nb
        per_step = ATTN_UNROLL * ATTN_BLOCK // qn

        def rows(start, size, dil=dil):
            if dil == 1:
                return pl.ds(pl.multiple_of(start, ATTN_BLOCK), size)
            return pl.ds(start, size, stride=dil)

        def block(it, bi=bi, dil=dil, nb=nb, nb_shift=nb_shift, rows=rows, whole=whole, qn=qn):
            if whole:
                q_rows = rows(it, qn)
                k_rows = q_rows
                table = jnp.concatenate([tbl[bi, 0], tbl[bi, 1]], axis=0)
            else:
                r = lax.shift_right_logical(it, nb_shift)
                i = it & (nb - 1)
                var = jnp.minimum(i, 1)
                q_rows = rows(i * ATTN_BLOCK * dil + r, qn)
                k_rows = rows((i - var) * ATTN_BLOCK * dil + r, 2 * ATTN_BLOCK)
                table = tbl[bi, var]
            q2 = qf[q_rows, :]
            kb = kf[k_rows, :].astype(BF16)
            v2 = vf[k_rows, :]
            qs = jnp.concatenate([jnp.where(head0, q2, 0.0), jnp.where(head0, 0.0, q2)], axis=0)
            s = _dot_nt(qs.astype(BF16), kb) + table
            m = jnp.max(s, axis=-1, keepdims=True)
            p = jnp.exp(s - m)
            den = jnp.sum(p, axis=-1, keepdims=True)
            pb = p.astype(BF16)
            vs = jnp.concatenate([jnp.where(head0, v2, 0.0), jnp.where(head0, 0.0, v2)], axis=0)
            o = _dot(jnp.concatenate([pb[:qn], pb[qn:]], axis=1), vs.astype(BF16))
            den2 = jnp.where(head0, den[:qn], den[qn:])
            lse = jnp.where(head0, m[:qn], m[qn:]) + jnp.log(den2)
            return q_rows, o / den2, lse

        def body(step, carry, bi=bi, block=block, per_step=per_step):
            done = [block(step * per_step + j) for j in range(per_step)]
            if bi > 0:
                merged = []
                for q_rows, o, lse in done:
                    l_old = lacc[q_rows, :]
                    m2 = jnp.maximum(l_old, lse)
                    a = jnp.exp(l_old - m2)
                    b = jnp.exp(lse - m2)
                    tot = a + b
                    merged.append((q_rows, (oacc[q_rows, :] * a + o * b) / tot, m2 + jnp.log(tot)))
                done = merged
            for q_rows, o, lse in done:
                oacc[q_rows, :] = o
                lacc[q_rows, :] = lse
            return carry

        assert n_units % per_step == 0
        lax.fori_loop(0, n_units // per_step, body, 0)

    o_ref[0] = oacc[...].astype(BF16)


def _attention(q, k, v, slopes_lane):
    B, S, _ = q.shape
    for window, dil in DILATED_BRANCHES:
        assert window // dil <= ATTN_BLOCK
        cls_len = S // dil
        assert S % dil == 0 and cls_len % ATTN_BLOCK == 0 and cls_len >= 2 * ATTN_BLOCK
        assert (cls_len // ATTN_BLOCK) & (cls_len // ATTN_BLOCK - 1) == 0
    n_pairs = ATTN_WIDTH // LANES
    tile = pl.BlockSpec((1, S, LANES), lambda b, p: (b, 0, p))
    return pl.pallas_call(
        functools.partial(_attn_kernel, seq=S),
        out_shape=jax.ShapeDtypeStruct((B, S, ATTN_WIDTH), BF16),
        grid=(B, n_pairs),
        in_specs=[pl.BlockSpec((1, 1, LANES), lambda b, p: (p, 0, 0)), tile, tile, tile],
        out_specs=tile,
        scratch_shapes=[pltpu.VMEM((S, LANES), F32)] * 5
        + [pltpu.VMEM((len(DILATED_BRANCHES), 2, 2 * ATTN_BLOCK, 2 * ATTN_BLOCK), F32)],
        compiler_params=_params(2, 32),
        name="attn",
    )(slopes_lane, q, k, v)


def _mix_kernel(attn_ref, u_ref, z_ref, x_ref, mod_ref, ga_ref, gs_ref, lng_ref, lnb_ref, wsp_ref,
                bsp_ref, wout_ref, gffn_ref, wrt_ref, wsgu_ref, wsd_ref,
                hs_ref, n2p_ref, lg_ref):
    tm = x_ref.shape[1]
    nc = tm // CHUNK
    a_n = _rms(attn_ref[0].astype(F32), ga_ref[...])

    ug = jax.nn.gelu(u_ref[0].astype(F32))
    zg = jax.nn.gelu(z_ref[0].astype(F32))
    mu = jnp.mean(zg, axis=-1, keepdims=True)
    zc = zg - mu
    var = jnp.mean(zc * zc, axis=-1, keepdims=True)
    zb = (zc * lax.rsqrt(var + EPS) * lng_ref[...] + lnb_ref[...]).astype(BF16)

    row = lax.broadcasted_iota(jnp.int32, (CHUNK, CHUNK), 0)
    col = lax.broadcasted_iota(jnp.int32, (CHUNK, CHUNK), 1)
    per_group = []
    for g in range(SGU_GROUPS):
        wc = jnp.where(row >= col, wsp_ref[g], 0.0).astype(BF16)
        lanes = slice(g * SGU_GROUP_DIM, (g + 1) * SGU_GROUP_DIM)
        zcat = jnp.concatenate([zb[c * CHUNK:(c + 1) * CHUNK, lanes] for c in range(nc)], axis=1)
        per_group.append(_dot(wc, zcat) + bsp_ref[:, g:g + 1])
    mixed = jnp.concatenate(
        [jnp.concatenate([per_group[g][:, c * CHUNK:(c + 1) * CHUNK] for g in range(SGU_GROUPS)], axis=1)
         for c in range(nc)], axis=0)
    s_n = _rms(ug * mixed, gs_ref[...])

    mix = (_dot(a_n.astype(BF16), wout_ref[0:ATTN_WIDTH, :])
           + _dot(s_n.astype(BF16), wout_ref[ATTN_WIDTH:, :]))
    gate1 = mod_ref[0, 2:3, :]
    shift2 = mod_ref[0, 3:4, :]
    scale2 = mod_ref[0, 4:5, :]
    gate2 = mod_ref[0, 5:6, :]
    h1 = x_ref[0] + gate1 * mix
    n2 = _rms(h1, gffn_ref[...]) * (1.0 + scale2) + shift2
    n2b = n2.astype(BF16)

    lg_ref[...] = _dot_nt(wrt_ref[...], n2b)
    gu = _dot(n2b, wsgu_ref[...])
    gsh = gu[:, :EXPERT_DIM]
    act = (gsh * jax.nn.sigmoid(gsh)) * gu[:, EXPERT_DIM:]
    shared = _dot(act.astype(BF16), wsd_ref[...])
    hs_ref[0] = h1 + gate2 * shared
    _store_rows(n2p_ref, _pack_halves(n2))


def _mix(attn, u, z, x, mod3, g_out_attn, g_out_sgu, ln_g, ln_b, w_spatial, b_spatial_t, w_out_bf,
         g_norm_ffn, w_router_t_bf, w_sh_gu_bf, w_sh_d_bf):
    B, S, _ = x.shape
    tm = TM_PROJ
    nt = S // tm
    N = B * S
    half_tile = pl.BlockSpec((1, tm, ATTN_WIDTH), lambda b, i: (b, i, 0))
    full_tile = pl.BlockSpec((1, tm, D_MODEL), lambda b, i: (b, i, 0))

    def const(shape):
        return pl.BlockSpec(shape, lambda b, i: (0,) * len(shape))

    return pl.pallas_call(
        _mix_kernel,
        out_shape=(
            jax.ShapeDtypeStruct((B, S, D_MODEL), F32),
            jax.ShapeDtypeStruct((N * ROW_SPLIT, LANES), jnp.uint32),
            jax.ShapeDtypeStruct((N_EXPERTS, N), F32),
        ),
        grid=(B, nt),
        in_specs=[
            half_tile, half_tile, half_tile, full_tile,
            pl.BlockSpec((1, 6, D_MODEL), lambda b, i: (b, 0, 0)),
            const((1, ATTN_WIDTH)), const((1, SGU_WIDTH)), const((1, SGU_WIDTH)), const((1, SGU_WIDTH)),
            const((SGU_GROUPS, CHUNK, CHUNK)), const((CHUNK, SGU_GROUPS)),
            const((ATTN_WIDTH + SGU_WIDTH, D_MODEL)), const((1, D_MODEL)),
            const((N_EXPERTS, D_MODEL)), const((D_MODEL, 2 * EXPERT_DIM)), const((EXPERT_DIM, D_MODEL)),
        ],
        out_specs=(
            full_tile,
            pl.BlockSpec((tm * ROW_SPLIT, LANES), lambda b, i: (b * nt + i, 0)),
            pl.BlockSpec((N_EXPERTS, tm), lambda b, i: (0, b * nt + i)),
        ),
        compiler_params=_params(2, 48),
        name="mix",
    )(attn, u, z, x, mod3, g_out_attn.reshape(1, -1), g_out_sgu.reshape(1, -1), ln_g.reshape(1, -1),
      ln_b.reshape(1, -1), w_spatial, b_spatial_t, w_out_bf, g_norm_ffn.reshape(1, -1), w_router_t_bf,
      w_sh_gu_bf, w_sh_d_bf)


def _first_max(v, iota, size):
    mx = jnp.max(v, axis=0, keepdims=True)
    am = jnp.min(jnp.where(v == mx, iota, size), axis=0, keepdims=True)
    return mx, am


def _route_kernel(lg_ref, br_ref, idx_ref, gate_ref, rank_ref, cnt_ref, carry, tri):
    step = pl.program_id(0)
    tr = lg_ref.shape[1]

    @pl.when(step == 0)
    def _():
        carry[...] = jnp.zeros_like(carry)
        before = (lax.broadcasted_iota(jnp.int32, (tr, tr), 0)
                  < lax.broadcasted_iota(jnp.int32, (tr, tr), 1))
        tri[...] = jnp.where(before, 1.0, 0.0).astype(BF16)

    scores = jax.nn.sigmoid(lg_ref[...])
    choice = scores + br_ref[...]

    iota_g = lax.broadcasted_iota(jnp.int32, (GROUP_SIZE, tr), 0)
    gs = []
    for g in range(N_EXPERT_GROUPS):
        cg = choice[g * GROUP_SIZE:(g + 1) * GROUP_SIZE, :]
        m1, am = _first_max(cg, iota_g, GROUP_SIZE)
        m2 = jnp.max(jnp.where(iota_g == am, -jnp.inf, cg), axis=0, keepdims=True)
        gs.append(m1 + m2)
    gscore = jnp.concatenate(gs, axis=0)

    iota_n = lax.broadcasted_iota(jnp.int32, (N_EXPERT_GROUPS, tr), 0)
    t = gscore
    for _ in range(TOPK_GROUPS - 1):
        _, am = _first_max(t, iota_n, N_EXPERT_GROUPS)
        t = jnp.where(iota_n == am, -jnp.inf, t)
    kth = jnp.max(t, axis=0, keepdims=True)
    keep = gscore >= kth

    v = jnp.concatenate(
        [jnp.where(keep[g:g + 1, :], choice[g * GROUP_SIZE:(g + 1) * GROUP_SIZE, :], -jnp.inf)
         for g in range(N_EXPERT_GROUPS)], axis=0)
    iota_e = lax.broadcasted_iota(jnp.int32, (N_EXPERTS, tr), 0)
    idxs, sels = [], []
    chosen = jnp.zeros((N_EXPERTS, tr), F32)
    for _ in range(TOP_K):
        _, am = _first_max(v, iota_e, N_EXPERTS)
        hit = iota_e == am
        idxs.append(am)
        sels.append(jnp.sum(jnp.where(hit, scores, 0.0), axis=0, keepdims=True))
        chosen = jnp.where(hit, 1.0, chosen)
        v = jnp.where(hit, -jnp.inf, v)
    sel = jnp.concatenate(sels, axis=0)
    idx_ref[...] = jnp.concatenate(idxs, axis=0)
    gate_ref[...] = sel / jnp.sum(sel, axis=0, keepdims=True) * ROUTED_SCALE

    earlier = _dot(chosen.astype(BF16), tri[...]) + carry[...]
    rank_ref[...] = jnp.concatenate(
        [jnp.sum(jnp.where(iota_e == am, earlier, 0.0), axis=0, keepdims=True) for am in idxs],
        axis=0).astype(jnp.int32)
    carry[...] = carry[...] + jnp.sum(chosen, axis=1, keepdims=True)
    cnt_ref[...] = jnp.broadcast_to(carry[...], cnt_ref.shape)


def _route(logits_t, b_router):
    N = logits_t.shape[1]
    tr = TM_ROUTE
    kt = pl.BlockSpec((TOP_K, tr), lambda i: (0, i))
    return pl.pallas_call(
        _route_kernel,
        out_shape=(
            jax.ShapeDtypeStruct((TOP_K, N), jnp.int32),
            jax.ShapeDtypeStruct((TOP_K, N), F32),
            jax.ShapeDtypeStruct((TOP_K, N), jnp.int32),
            jax.ShapeDtypeStruct((N_EXPERTS, LANES), F32),
        ),
        grid=(N // tr,),
        in_specs=[
            pl.BlockSpec((N_EXPERTS, tr), lambda i: (0, i)),
            pl.BlockSpec((N_EXPERTS, 1), lambda i: (0, 0)),
        ],
        out_specs=(kt, kt, kt, pl.BlockSpec((N_EXPERTS, LANES), lambda i: (0, 0))),
        scratch_shapes=[pltpu.VMEM((N_EXPERTS, 1), F32), pltpu.VMEM((tr, tr), BF16)],
        compiler_params=_params(1, 16),
        name="route",
    )(logits_t, b_router.reshape(N_EXPERTS, 1))


def _dest_kernel(idx_ref, rank_ref, ps_ref, dest_ref):
    tr = idx_ref.shape[1]
    iota_e = lax.broadcasted_iota(jnp.int32, (N_EXPERTS, tr), 0)
    idx = idx_ref[...]
    start = jnp.concatenate(
        [jnp.sum(jnp.where(iota_e == idx[k:k + 1, :], ps_ref[...], 0.0), axis=0, keepdims=True)
         for k in range(TOP_K)], axis=0)
    dest_ref[...] = (start.astype(jnp.int32) + rank_ref[...]) * ROW_SPLIT


def _dest(idx_t, rank_t, pstart):
    N = idx_t.shape[1]
    tr = TM_ROUTE
    kt = pl.BlockSpec((TOP_K, tr), lambda i: (0, i))
    return pl.pallas_call(
        _dest_kernel,
        out_shape=jax.ShapeDtypeStruct((TOP_K, N), jnp.int32),
        grid=(N // tr,),
        in_specs=[kt, kt, pl.BlockSpec((N_EXPERTS, 1), lambda i: (0, 0))],
        out_specs=kt,
        compiler_params=_params(1, 8),
        name="dest",
    )(idx_t, rank_t, pstart.astype(F32).reshape(N_EXPERTS, 1))


def _row_copy(src, dst, sem):
    return pltpu.make_async_copy(src, dst, sem)


def _token_rows(start):
    return pl.ds(pl.multiple_of(start, ROW_SPLIT), ROW_SPLIT)


def _block_rows(block):
    return pl.ds(pl.multiple_of(block * BLOCK_SUBROWS, BLOCK_SUBROWS), BLOCK_SUBROWS)


def _disp_kernel(zb_ref, dest_ref, x_ref, xs_ref, xbuf, zbuf, sem_in, sem_out):
    i = pl.program_id(0)
    n = pl.num_programs(0)
    tile_rows = xbuf.shape[1]
    tm = tile_rows // ROW_SPLIT
    slot = lax.rem(i, DISP_RING)
    ahead = lax.rem(i + 1, DISP_RING)

    def load(tile, into):
        start = pl.multiple_of(tile * tile_rows, tile_rows)
        return _row_copy(x_ref.at[pl.ds(start, tile_rows)], xbuf.at[into], sem_in.at[into])

    def drain(of):
        for k in range(TOP_K):
            _row_copy(xbuf.at[of], xs_ref.at[pl.ds(0, tile_rows)], sem_out.at[of]).wait()

    @pl.when(i == 0)
    def _():
        zbuf[...] = jnp.zeros_like(zbuf)

        def zero(j, started):
            blk = zb_ref[j]

            @pl.when(blk >= 0)
            def _():
                _row_copy(zbuf, xs_ref.at[_block_rows(blk)], sem_out.at[0]).start()

            return started + jnp.where(blk >= 0, 1, 0)

        def done(j, carry):
            _row_copy(zbuf, xs_ref.at[_block_rows(0)], sem_out.at[0]).wait()
            return carry

        started = lax.fori_loop(0, zb_ref.shape[0], zero, 0)
        lax.fori_loop(0, started, done, 0)
        load(0, 0).start()

    @pl.when(i + 1 < n)
    def _():
        @pl.when(i + 1 >= DISP_RING)
        def _():
            drain(ahead)

        load(i + 1, ahead).start()

    load(i, slot).wait()

    def issue(t, carry):
        src = xbuf.at[slot, _token_rows(t * ROW_SPLIT)]
        base = t * TOP_K
        for k in range(TOP_K):
            _row_copy(src, xs_ref.at[_token_rows(dest_ref[base + k])], sem_out.at[slot]).start(priority=k % 2)
        return carry

    lax.fori_loop(0, tm, issue, 0)

    @pl.when(i == n - 1)
    def _():
        for s in range(DISP_RING):
            drain(s)


def _dispatch(zero_blocks, dest_flat, n2p, n_rows):
    N = n2p.shape[0] // ROW_SPLIT
    tm = TM_MOVE
    assert N // tm >= DISP_RING
    grid_spec = pltpu.PrefetchScalarGridSpec(
        num_scalar_prefetch=1,
        grid=(N // tm,),
        in_specs=[
            pl.BlockSpec((tm * TOP_K,), lambda i, zb: (i,), memory_space=pltpu.SMEM),
            pl.BlockSpec(memory_space=pl.ANY),
        ],
        out_specs=pl.BlockSpec(memory_space=pl.ANY),
        scratch_shapes=[
            pltpu.VMEM((DISP_RING, tm * ROW_SPLIT, LANES), jnp.uint32),
            pltpu.VMEM((BLOCK_SUBROWS, LANES), jnp.uint32),
            pltpu.SemaphoreType.DMA((DISP_RING,)),
            pltpu.SemaphoreType.DMA((DISP_RING,)),
        ],
    )
    return pl.pallas_call(
        _disp_kernel,
        out_shape=jax.ShapeDtypeStruct((n_rows * ROW_SPLIT, LANES), jnp.uint32),
        grid_spec=grid_spec,
        compiler_params=_params(1, 8),
        name="disp",
    )(zero_blocks, dest_flat, n2p)


def _experts_kernel(eb_ref, bv_ref, xs_ref, wg_ref, wu_ref, wd_ref, ys_ref,
                    xbuf, ybuf, wgb, wub, wdb, sem_in, sem_out):
    e = pl.program_id(0)
    n_used = eb_ref[N_EXPERTS]
    n_blocks = ys_ref.shape[0] // BLOCK_SUBROWS
    first = eb_ref[e]
    end = eb_ref[e + 1]

    def in_copy(g, slot):
        return _row_copy(xs_ref.at[_block_rows(g)], xbuf.at[slot], sem_in.at[slot])

    def out_copy(g, slot):
        return _row_copy(ybuf.at[slot], ys_ref.at[_block_rows(g)], sem_out.at[slot])

    @pl.when(e == 0)
    def _():
        for g in range(RING - 1):
            @pl.when(g < n_used)
            def _(g=g):
                in_copy(g, g).start()

    @pl.when(end > first)
    def _():
        wgb[...] = wg_ref[...].astype(BF16)
        wub[...] = wu_ref[...].astype(BF16)
        wdb[...] = wd_ref[...].astype(BF16)

        def block(g, carry):
            slot = g & (RING - 1)
            in_copy(g, slot).wait()

            @pl.when(g + RING - 1 < n_used)
            def _():
                in_copy(g + RING - 1, (g + RING - 1) & (RING - 1)).start()

            @pl.when(g >= RING)
            def _():
                out_copy(g - RING, slot).wait()

            lo, hi = _unpack_halves(_load_rows(xbuf.at[slot], ROW_BLOCK))
            live = lax.broadcasted_iota(jnp.int32, lo.shape, 0) < bv_ref[g]
            lo = jnp.where(live, lo, 0.0).astype(BF16)
            hi = jnp.where(live, hi, 0.0).astype(BF16)
            gate = _dot(lo, wgb[0:HALF, :]) + _dot(hi, wgb[HALF:, :])
            up = _dot(lo, wub[0:HALF, :]) + _dot(hi, wub[HALF:, :])
            act = (gate * jax.nn.sigmoid(gate)) * up
            _store_rows(ybuf.at[slot], _pack_halves(_dot(act.astype(BF16), wdb[...])))
            out_copy(g, slot).start()
            return carry

        lax.fori_loop(first, end, block, 0)

    @pl.when(e == N_EXPERTS - 1)
    def _():
        for back in range(1, RING + 1):
            @pl.when(n_used >= back)
            def _(back=back):
                out_copy(n_used - back, (n_used - back) & (RING - 1)).wait()

        xbuf[0] = jnp.zeros((BLOCK_SUBROWS, LANES), jnp.uint32)

        def zero(g, carry):
            _row_copy(xbuf.at[0], ys_ref.at[_block_rows(g)], sem_out.at[0]).start()
            return carry

        def done(g, carry):
            _row_copy(xbuf.at[0], ys_ref.at[_block_rows(g)], sem_out.at[0]).wait()
            return carry

        lax.fori_loop(n_used, n_blocks, zero, 0)
        lax.fori_loop(n_used, n_blocks, done, 0)


def _experts(expert_block, block_valid, xs, w_gate, w_up, w_down):
    def w_map(e, eb, bv):
        return (e, 0, 0)

    grid_spec = pltpu.PrefetchScalarGridSpec(
        num_scalar_prefetch=2,
        grid=(N_EXPERTS,),
        in_specs=[
            pl.BlockSpec(memory_space=pl.ANY),
            pl.BlockSpec((None, D_MODEL, EXPERT_DIM), w_map),
            pl.BlockSpec((None, D_MODEL, EXPERT_DIM), w_map),
            pl.BlockSpec((None, EXPERT_DIM, D_MODEL), w_map),
        ],
        out_specs=pl.BlockSpec(memory_space=pl.ANY),
        scratch_shapes=[
            pltpu.VMEM((RING, BLOCK_SUBROWS, LANES), jnp.uint32),
            pltpu.VMEM((RING, BLOCK_SUBROWS, LANES), jnp.uint32),
            pltpu.VMEM((D_MODEL, EXPERT_DIM), BF16),
            pltpu.VMEM((D_MODEL, EXPERT_DIM), BF16),
            pltpu.VMEM((EXPERT_DIM, D_MODEL), BF16),
            pltpu.SemaphoreType.DMA((RING,)),
            pltpu.SemaphoreType.DMA((RING,)),
        ],
    )
    return pl.pallas_call(
        _experts_kernel,
        out_shape=jax.ShapeDtypeStruct(xs.shape, jnp.uint32),
        grid_spec=grid_spec,
        compiler_params=_params(1, 24),
        name="experts",
    )(expert_block, block_valid, xs, w_gate, w_up, w_down)


def _comb_kernel(dest_ref, next_ref, ys_ref, hs_ref, gate_ref, mod_ref, gf_ref, o_ref, buf, sem):
    i = pl.program_id(0)
    tm = hs_ref.shape[0]
    slot = i & 1

    def gather(rows_ref, into):
        def issue(t, carry):
            base = t * TOP_K
            for k in range(TOP_K):
                _row_copy(ys_ref.at[_token_rows(rows_ref[base + k])],
                          buf.at[into, k, _token_rows(t * ROW_SPLIT)], sem.at[into]).start()
            return carry

        lax.fori_loop(0, tm, issue, 0)

    @pl.when(i == 0)
    def _():
        gather(dest_ref, 0)

    for into in range(2):
        @pl.when(jnp.logical_and(i + 1 < pl.num_programs(0), slot == 1 - into))
        def _(into=into):
            gather(next_ref, into)

    for k in range(TOP_K):
        _row_copy(ys_ref.at[pl.ds(0, tm * ROW_SPLIT)], buf.at[slot, k], sem.at[slot]).wait()

    for c in range(tm // COMB_CHUNK):
        tok = slice(c * COMB_CHUNK, (c + 1) * COMB_CHUNK)
        gates = gate_ref[tok, :]
        lo = jnp.zeros((COMB_CHUNK, HALF), F32)
        hi = jnp.zeros((COMB_CHUNK, HALF), F32)
        for k in range(TOP_K):
            piece = buf.at[slot, k, pl.ds(c * COMB_CHUNK * ROW_SPLIT, COMB_CHUNK * ROW_SPLIT)]
            lo_k, hi_k = _unpack_halves(_load_rows(piece, COMB_CHUNK))
            gk = gates[:, k:k + 1]
            lo = lo + gk * lo_k
            hi = hi + gk * hi_k
        routed = jnp.concatenate([lo, hi], axis=1)
        h2 = hs_ref[tok, :] + mod_ref[0, 5:6, :] * routed
        o_ref[tok, :] = _rms(h2, gf_ref[...])


def _combine(dest_flat, ys, hs2, gates_nk, mod3, g_final, seq):
    N = hs2.shape[0]
    tm = TM_MOVE
    per_seq = seq // tm
    n_tiles = N // tm
    return pl.pallas_call(
        _comb_kernel,
        out_shape=jax.ShapeDtypeStruct((N, D_MODEL), F32),
        grid=(n_tiles,),
        in_specs=[
            pl.BlockSpec((tm * TOP_K,), lambda i: (i,), memory_space=pltpu.SMEM),
            pl.BlockSpec((tm * TOP_K,), lambda i: (jnp.minimum(i + 1, n_tiles - 1),), memory_space=pltpu.SMEM),
            pl.BlockSpec(memory_space=pl.ANY),
            pl.BlockSpec((tm, D_MODEL), lambda i: (i, 0)),
            pl.BlockSpec((tm, TOP_K), lambda i: (i, 0)),
            pl.BlockSpec((1, 6, D_MODEL), lambda i: (i // per_seq, 0, 0)),
            pl.BlockSpec((1, D_MODEL), lambda i: (0, 0)),
        ],
        out_specs=pl.BlockSpec((tm, D_MODEL), lambda i: (i, 0)),
        scratch_shapes=[pltpu.VMEM((2, TOP_K, tm * ROW_SPLIT, LANES), jnp.uint32),
                        pltpu.SemaphoreType.DMA((2,))],
        compiler_params=_params(1, 32),
        name="comb",
    )(dest_flat, dest_flat, ys, hs2, gates_nk, mod3, g_final.reshape(1, D_MODEL))


def _layer(x, mod3, g_norm_mix, w_in, sgu_ln_gain, sgu_ln_bias, w_spatial, b_spatial, g_out_attn,
           g_out_sgu, w_out, g_norm_ffn, w_router, b_router, w_exp_gate, w_exp_up, w_exp_down,
           w_sh_gate, w_sh_up, w_sh_down, g_final):
    B, S, _ = x.shape
    N = B * S

    q, k, v, u, z = _inproj(x, mod3, g_norm_mix, w_in.astype(BF16))
    head_of_lane = jnp.arange(ATTN_WIDTH) // HEAD_DIM
    slopes = jnp.exp2(-8.0 * jnp.arange(1, ATTN_HEADS + 1, dtype=F32) / ATTN_HEADS)
    slopes_lane = slopes[head_of_lane].reshape(ATTN_WIDTH // LANES, 1, LANES)
    attn = _attention(q, k, v, slopes_lane)

    hs, n2p, logits_t = _mix(
        attn, u, z, x, mod3, g_out_attn, g_out_sgu, sgu_ln_gain, sgu_ln_bias, w_spatial, b_spatial.T,
        w_out.astype(BF16), g_norm_ffn, w_router.T.astype(BF16),
        jnp.concatenate([w_sh_gate, w_sh_up], axis=1).astype(BF16), w_sh_down.astype(BF16))

    idx_t, gates_t, rank_t, counts = _route(logits_t, b_router)

    n_blocks = (N * TOP_K + N_EXPERTS * (ROW_BLOCK - 1)) // ROW_BLOCK
    cnt = counts[:, 0].astype(jnp.int32)
    padded = (cnt + ROW_BLOCK - 1) // ROW_BLOCK * ROW_BLOCK
    pends = jnp.cumsum(padded)
    pstart = pends - padded
    block_row = jnp.arange(n_blocks, dtype=jnp.int32) * ROW_BLOCK
    block_e = jnp.sum(pends[None, :] <= block_row[:, None], axis=1, dtype=jnp.int32)
    block_e = jnp.minimum(block_e, N_EXPERTS - 1)
    own = block_e[:, None] == jnp.arange(N_EXPERTS, dtype=jnp.int32)[None, :]
    live_end = jnp.sum(jnp.where(own, (pstart + cnt)[None, :], 0), axis=1)
    block_valid = jnp.clip(live_end - block_row, 0, ROW_BLOCK).astype(jnp.int32)
    n_used = (pends[-1:] // ROW_BLOCK).astype(jnp.int32)
    expert_block = jnp.concatenate([pstart // ROW_BLOCK, n_used]).astype(jnp.int32)

    last_block = jnp.where(cnt > 0, pends // ROW_BLOCK - 1, -1)
    tail_block = n_used[0] + jnp.arange(n_blocks - N * TOP_K // ROW_BLOCK)
    tail_block = jnp.where(tail_block < n_blocks, tail_block, -1)
    zero_blocks = jnp.concatenate([last_block, tail_block]).astype(jnp.int32)

    dest_flat = _dest(idx_t, rank_t, pstart).T.reshape(N * TOP_K)
    xs = _dispatch(zero_blocks, dest_flat, n2p, n_blocks * ROW_BLOCK)
    ys = _experts(expert_block, block_valid, xs, w_exp_gate, w_exp_up, w_exp_down)
    out = _combine(dest_flat, ys, hs.reshape(N, D_MODEL), gates_t.T, mod3, g_final, S)
    return out.reshape(B, S, D_MODEL)


def kernel(x, c, w_ada, b_ada, g_norm_mix, w_in, sgu_ln_gain, sgu_ln_bias, w_spatial, b_spatial, g_out_attn, g_out_sgu, w_out, g_norm_ffn, w_router, b_router, w_exp_gate, w_exp_up, w_exp_down, w_sh_gate, w_sh_up, w_sh_down, g_final):
    assert w_ada.shape[0] == 1, "single-layer stack"
    B = x.shape[0]
    mod3 = _ada(c, w_ada[0], b_ada[0]).reshape(B, 6, D_MODEL)
    return _layer(x, mod3, g_norm_mix[0], w_in[0], sgu_ln_gain[0], sgu_ln_bias[0], w_spatial[0],
                  b_spatial[0], g_out_attn[0], g_out_sgu[0], w_out[0], g_norm_ffn[0], w_router[0],
                  b_router[0], w_exp_gate[0], w_exp_up[0], w_exp_down[0], w_sh_gate[0], w_sh_up[0],
                  w_sh_down[0], g_final)
```

```python
import functools

import jax
import jax.numpy as jnp
from jax import lax
from jax.experimental import pallas as pl
from jax.experimental.pallas import tpu as pltpu

F32 = jnp.float32
BF16 = jnp.bfloat16

D_MODEL = 1024
ATTN_WIDTH = 512
ATTN_HEADS = 8
HEAD_DIM = 64
SGU_WIDTH = 512
SGU_GROUPS = 4
SGU_GROUP_DIM = 128
CHUNK = 128
DILATED_BRANCHES = ((128, 1), (512, 4), (2048, 16))
ATTN_BLOCK = 128
N_EXPERTS = 256
TOP_K = 8
N_EXPERT_GROUPS = 8
GROUP_SIZE = N_EXPERTS // N_EXPERT_GROUPS
TOPK_GROUPS = 4
EXPERT_DIM = 256
ROUTED_SCALE = 2.5
EPS = 1e-6

LANES = 128
HALF = D_MODEL // 2
ROW_BLOCK = 512
NEG_BIG = -1e30

TM_PROJ = 1024
TM_ROUTE = 512
TM_MOVE = 512
DISP_RING = 3
COMB_CHUNK = 128
ATTN_UNROLL = 16
RING = 4


def _dot(a, b):
    return jnp.dot(a, b, preferred_element_type=F32)


def _dot_nt(a, b):
    return lax.dot_general(a, b, (((1,), (1,)), ((), ())), preferred_element_type=F32)


def _rms(x, g):
    return x * lax.rsqrt(jnp.mean(x * x, axis=-1, keepdims=True) + EPS) * g


def _pack_halves(x):
    return pltpu.pack_elementwise([x[:, :HALF], x[:, HALF:]], packed_dtype=BF16)


def _unpack_halves(w):
    lo = pltpu.unpack_elementwise(w, index=0, packed_dtype=BF16, unpacked_dtype=F32)
    hi = pltpu.unpack_elementwise(w, index=1, packed_dtype=BF16, unpacked_dtype=F32)
    return lo, hi


ROW_SPLIT = HALF // LANES
BLOCK_SUBROWS = ROW_BLOCK * ROW_SPLIT


def _store_rows(ref, packed):
    rows = packed.shape[0]
    for c in range(ROW_SPLIT):
        ref[pl.ds(c, rows, stride=ROW_SPLIT), :] = packed[:, c * LANES:(c + 1) * LANES]


def _load_rows(ref, rows):
    return jnp.concatenate([ref[pl.ds(c, rows, stride=ROW_SPLIT), :] for c in range(ROW_SPLIT)], axis=1)


def _params(n_axes, vmem_mib):
    return pltpu.CompilerParams(dimension_semantics=("arbitrary",) * n_axes,
                                vmem_limit_bytes=vmem_mib * 1024 * 1024)


def _ada_kernel(c_ref, w_ref, b_ref, o_ref):
    c = c_ref[...]
    cond = c * jax.nn.sigmoid(c)
    ch = cond.astype(BF16)
    cl = (cond - ch.astype(F32)).astype(BF16)
    w = w_ref[...]
    wh = w.astype(BF16)
    wl = (w - wh.astype(F32)).astype(BF16)
    o_ref[...] = _dot(ch, wh) + _dot(cl, wh) + _dot(ch, wl) + b_ref[...]


def _ada(c, w_ada, b_ada):
    B = c.shape[0]
    n_out = w_ada.shape[1]
    tn = D_MODEL
    return pl.pallas_call(
        _ada_kernel,
        out_shape=jax.ShapeDtypeStruct((B, n_out), F32),
        grid=(n_out // tn,),
        in_specs=[
            pl.BlockSpec((B, D_MODEL), lambda j: (0, 0)),
            pl.BlockSpec((D_MODEL, tn), lambda j: (0, j)),
            pl.BlockSpec((1, tn), lambda j: (0, j)),
        ],
        out_specs=pl.BlockSpec((B, tn), lambda j: (0, j)),
        compiler_params=_params(1, 16),
        name="ada",
    )(c, w_ada, b_ada.reshape(1, n_out))


def _inproj_kernel(x_ref, mod_ref, g_ref, w_ref, q_ref, k_ref, v_ref, u_ref, z_ref):
    x = x_ref[0]
    shift = mod_ref[0, 0:1, :]
    scale = mod_ref[0, 1:2, :]
    n = _rms(x, g_ref[...]) * (1.0 + scale) + shift
    p = _dot(n.astype(BF16), w_ref[...])
    q_ref[0] = (p[:, 0:ATTN_WIDTH] * (HEAD_DIM ** -0.5)).astype(BF16)
    k_ref[0] = p[:, ATTN_WIDTH:2 * ATTN_WIDTH].astype(BF16)
    v_ref[0] = p[:, 2 * ATTN_WIDTH:3 * ATTN_WIDTH].astype(BF16)
    u_ref[0] = p[:, 3 * ATTN_WIDTH:3 * ATTN_WIDTH + SGU_WIDTH].astype(BF16)
    z_ref[0] = p[:, 3 * ATTN_WIDTH + SGU_WIDTH:].astype(BF16)


def _inproj(x, mod3, g_norm, w_in_bf):
    B, S, _ = x.shape
    tm = TM_PROJ
    n_in = w_in_bf.shape[1]
    out = jax.ShapeDtypeStruct((B, S, ATTN_WIDTH), BF16)
    tile = pl.BlockSpec((1, tm, ATTN_WIDTH), lambda b, i: (b, i, 0))
    return pl.pallas_call(
        _inproj_kernel,
        out_shape=(out,) * 5,
        grid=(B, S // tm),
        in_specs=[
            pl.BlockSpec((1, tm, D_MODEL), lambda b, i: (b, i, 0)),
            pl.BlockSpec((1, 6, D_MODEL), lambda b, i: (b, 0, 0)),
            pl.BlockSpec((1, D_MODEL), lambda b, i: (0, 0)),
            pl.BlockSpec((D_MODEL, n_in), lambda b, i: (0, 0)),
        ],
        out_specs=(tile,) * 5,
        compiler_params=_params(2, 32),
        name="inproj",
    )(x, mod3, g_norm.reshape(1, D_MODEL), w_in_bf)


def _attn_kernel(slope_ref, q_ref, k_ref, v_ref, o_ref, qf, kf, vf, oacc, lacc, tbl, *, seq):
    qf[...] = q_ref[0].astype(F32)
    kf[...] = k_ref[0].astype(F32)
    vf[...] = v_ref[0].astype(F32)

    lane = lax.broadcasted_iota(jnp.int32, (1, LANES), 1)
    head0 = lane < HEAD_DIM
    slopes = slope_ref[0]
    slope_h = (slopes[:, 0:1], slopes[:, HEAD_DIM:HEAD_DIM + 1])

    whole_class = [seq // dil == 2 * ATTN_BLOCK for _, dil in DILATED_BRANCHES]
    def build_tables(bi, window, dil):
        steps = window // dil
        if whole_class[bi]:
            back = (lax.broadcasted_iota(jnp.int32, (2 * ATTN_BLOCK, 2 * ATTN_BLOCK), 0)
                    - lax.broadcasted_iota(jnp.int32, (2 * ATTN_BLOCK, 2 * ATTN_BLOCK), 1))
            valid = (back >= 0) & (back <= steps)
            dist = (back * dil).astype(F32)
            for hh in range(2):
                tbl[bi, hh] = jnp.where(valid, -slope_h[hh] * dist, NEG_BIG)
            return
        qr = lax.broadcasted_iota(jnp.int32, (ATTN_BLOCK, 2 * ATTN_BLOCK), 0)
        kc = lax.broadcasted_iota(jnp.int32, (ATTN_BLOCK, 2 * ATTN_BLOCK), 1)
        for var in range(2):
            back = qr - kc + var * ATTN_BLOCK
            valid = (back >= 0) & (back <= steps)
            dist = (back * dil).astype(F32)
            for hh in range(2):
                tbl[bi, var, hh * ATTN_BLOCK:(hh + 1) * ATTN_BLOCK, :] = jnp.where(
                    valid, -slope_h[hh] * dist, NEG_BIG)

    @pl.when(pl.program_id(1) == 0)
    def _():
        for bi, (window, dil) in enumerate(DILATED_BRANCHES):
            build_tables(bi, window, dil)

    for bi, (window, dil) in enumerate(DILATED_BRANCHES):
        cls_len = seq // dil
        nb = cls_len // ATTN_BLOCK
        nb_shift = nb.bit_length() - 1
        whole = whole_class[bi]
        qn = 2 * ATTN_BLOCK if whole else ATTN_BLOCK
        n_units = dil if whole else dil * nb
        per_step = ATTN_UNROLL * ATTN_BLOCK // qn

        def rows(start, size, dil=dil):
            if dil == 1:
                return pl.ds(pl.multiple_of(start, ATTN_BLOCK), size)
            return pl.ds(start, size, stride=dil)

        def block(it, bi=bi, dil=dil, nb=nb, nb_shift=nb_shift, rows=rows, whole=whole, qn=qn):
            if whole:
                q_rows = rows(it, qn)
                k_rows = q_rows
                table = jnp.concatenate([tbl[bi, 0], tbl[bi, 1]], axis=0)
            else:
                r = lax.shift_right_logical(it, nb_shift)
                i = it & (nb - 1)
                var = jnp.minimum(i, 1)
                q_rows = rows(i * ATTN_BLOCK * dil + r, qn)
                k_rows = rows((i - var) * ATTN_BLOCK * dil + r, 2 * ATTN_BLOCK)
                table = tbl[bi, var]
            q2 = qf[q_rows, :]
            kb = kf[k_rows, :].astype(BF16)
            v2 = vf[k_rows, :]
            qs = jnp.concatenate([jnp.where(head0, q2, 0.0), jnp.where(head0, 0.0, q2)], axis=0)
            s = _dot_nt(qs.astype(BF16), kb) + table
            m = jnp.max(s, axis=-1, keepdims=True)
            p = jnp.exp(s - m)
            den = jnp.sum(p, axis=-1, keepdims=True)
            pb = p.astype(BF16)
            vs = jnp.concatenate([jnp.where(head0, v2, 0.0), jnp.where(head0, 0.0, v2)], axis=0)
            o = _dot(jnp.concatenate([pb[:qn], pb[qn:]], axis=1), vs.astype(BF16))
            den2 = jnp.where(head0, den[:qn], den[qn:])
            lse = jnp.where(head0, m[:qn], m[qn:]) + jnp.log(den2)
            return q_rows, o / den2, lse

        def body(step, carry, bi=bi, block=block, per_step=per_step):
            done = [block(step * per_step + j) for j in range(per_step)]
            if bi > 0:
                merged = []
                for q_rows, o, lse in done:
                    l_old = lacc[q_rows, :]
                    m2 = jnp.maximum(l_old, lse)
                    a = jnp.exp(l_old - m2)
                    b = jnp.exp(lse - m2)
                    tot = a + b
                    merged.append((q_rows, (oacc[q_rows, :] * a + o * b) / tot, m2 + jnp.log(tot)))
                done = merged
            for q_rows, o, lse in done:
                oacc[q_rows, :] = o
                lacc[q_rows, :] = lse
            return carry

        assert n_units % per_step == 0
        lax.fori_loop(0, n_units // per_step, body, 0)

    o_ref[0] = oacc[...].astype(BF16)


def _attention(q, k, v, slopes_lane):
    B, S, _ = q.shape
    for window, dil in DILATED_BRANCHES:
        assert window // dil <= ATTN_BLOCK
        cls_len = S // dil
        assert S % dil == 0 and cls_len % ATTN_BLOCK == 0 and cls_len >= 2 * ATTN_BLOCK
        assert (cls_len // ATTN_BLOCK) & (cls_len // ATTN_BLOCK - 1) == 0
    n_pairs = ATTN_WIDTH // LANES
    tile = pl.BlockSpec((1, S, LANES), lambda p, b: (b, 0, p))
    return pl.pallas_call(
        functools.partial(_attn_kernel, seq=S),
        out_shape=jax.ShapeDtypeStruct((B, S, ATTN_WIDTH), BF16),
        grid=(n_pairs, B),
        in_specs=[pl.BlockSpec((1, 1, LANES), lambda p, b: (p, 0, 0)), tile, tile, tile],
        out_specs=tile,
        scratch_shapes=[pltpu.VMEM((S, LANES), F32)] * 5
        + [pltpu.VMEM((len(DILATED_BRANCHES), 2, 2 * ATTN_BLOCK, 2 * ATTN_BLOCK), F32)],
        compiler_params=_params(2, 32),
        name="attn",
    )(slopes_lane, q, k, v)


def _mix_kernel(attn_ref, u_ref, z_ref, x_ref, mod_ref, ga_ref, gs_ref, lng_ref, lnb_ref, wsp_ref,
                bsp_ref, wout_ref, gffn_ref, wrt_ref, wsgu_ref, wsd_ref,
                hs_ref, n2p_ref, lg_ref):
    tm = x_ref.shape[1]
    nc = tm // CHUNK
    a_n = _rms(attn_ref[0].astype(F32), ga_ref[...])

    ug = jax.nn.gelu(u_ref[0].astype(F32))
    zg = jax.nn.gelu(z_ref[0].astype(F32))
    mu = jnp.mean(zg, axis=-1, keepdims=True)
    zc = zg - mu
    var = jnp.mean(zc * zc, axis=-1, keepdims=True)
    zb = (zc * lax.rsqrt(var + EPS) * lng_ref[...] + lnb_ref[...]).astype(BF16)

    row = lax.broadcasted_iota(jnp.int32, (CHUNK, CHUNK), 0)
    col = lax.broadcasted_iota(jnp.int32, (CHUNK, CHUNK), 1)
    per_group = []
    for g in range(SGU_GROUPS):
        wc = jnp.where(row >= col, wsp_ref[g], 0.0).astype(BF16)
        lanes = slice(g * SGU_GROUP_DIM, (g + 1) * SGU_GROUP_DIM)
        zcat = jnp.concatenate([zb[c * CHUNK:(c + 1) * CHUNK, lanes] for c in range(nc)], axis=1)
        per_group.append(_dot(wc, zcat) + bsp_ref[:, g:g + 1])
    mixed = jnp.concatenate(
        [jnp.concatenate([per_group[g][:, c * CHUNK:(c + 1) * CHUNK] for g in range(SGU_GROUPS)], axis=1)
         for c in range(nc)], axis=0)
    s_n = _rms(ug * mixed, gs_ref[...])

    mix = (_dot(a_n.astype(BF16), wout_ref[0:ATTN_WIDTH, :])
           + _dot(s_n.astype(BF16), wout_ref[ATTN_WIDTH:, :]))
    gate1 = mod_ref[0, 2:3, :]
    shift2 = mod_ref[0, 3:4, :]
    scale2 = mod_ref[0, 4:5, :]
    gate2 = mod_ref[0, 5:6, :]
    h1 = x_ref[0] + gate1 * mix
    n2 = _rms(h1, gffn_ref[...]) * (1.0 + scale2) + shift2
    n2b = n2.astype(BF16)

    lg_ref[...] = _dot_nt(wrt_ref[...], n2b)
    gu = _dot(n2b, wsgu_ref[...])
    gsh = gu[:, :EXPERT_DIM]
    act = (gsh * jax.nn.sigmoid(gsh)) * gu[:, EXPERT_DIM:]
    shared = _dot(act.astype(BF16), wsd_ref[...])
    hs_ref[0] = h1 + gate2 * shared
    _store_rows(n2p_ref, _pack_halves(n2))


def _mix(attn, u, z, x, mod3, g_out_attn, g_out_sgu, ln_g, ln_b, w_spatial, b_spatial_t, w_out_bf,
         g_norm_ffn, w_router_t_bf, w_sh_gu_bf, w_sh_d_bf):
    B, S, _ = x.shape
    tm = TM_PROJ
    nt = S // tm
    N = B * S
    half_tile = pl.BlockSpec((1, tm, ATTN_WIDTH), lambda b, i: (b, i, 0))
    full_tile = pl.BlockSpec((1, tm, D_MODEL), lambda b, i: (b, i, 0))

    def const(shape):
        return pl.BlockSpec(shape, lambda b, i: (0,) * len(shape))

    return pl.pallas_call(
        _mix_kernel,
        out_shape=(
            jax.ShapeDtypeStruct((B, S, D_MODEL), F32),
            jax.ShapeDtypeStruct((N * ROW_SPLIT, LANES), jnp.uint32),
            jax.ShapeDtypeStruct((N_EXPERTS, N), F32),
        ),
        grid=(B, nt),
        in_specs=[
            half_tile, half_tile, half_tile, full_tile,
            pl.BlockSpec((1, 6, D_MODEL), lambda b, i: (b, 0, 0)),
            const((1, ATTN_WIDTH)), const((1, SGU_WIDTH)), const((1, SGU_WIDTH)), const((1, SGU_WIDTH)),
            const((SGU_GROUPS, CHUNK, CHUNK)), const((CHUNK, SGU_GROUPS)),
            const((ATTN_WIDTH + SGU_WIDTH, D_MODEL)), const((1, D_MODEL)),
            const((N_EXPERTS, D_MODEL)), const((D_MODEL, 2 * EXPERT_DIM)), const((EXPERT_DIM, D_MODEL)),
        ],
        out_specs=(
            full_tile,
            pl.BlockSpec((tm * ROW_SPLIT, LANES), lambda b, i: (b * nt + i, 0)),
            pl.BlockSpec((N_EXPERTS, tm), lambda b, i: (0, b * nt + i)),
        ),
        compiler_params=_params(2, 48),
        name="mix",
    )(attn, u, z, x, mod3, g_out_attn.reshape(1, -1), g_out_sgu.reshape(1, -1), ln_g.reshape(1, -1),
      ln_b.reshape(1, -1), w_spatial, b_spatial_t, w_out_bf, g_norm_ffn.reshape(1, -1), w_router_t_bf,
      w_sh_gu_bf, w_sh_d_bf)


def _first_max(v, iota, size):
    mx = jnp.max(v, axis=0, keepdims=True)
    am = jnp.min(jnp.where(v == mx, iota, size), axis=0, keepdims=True)
    return mx, am


def _route_kernel(lg_ref, br_ref, idx_ref, gate_ref, rank_ref, cnt_ref, carry, tri):
    step = pl.program_id(0)
    tr = lg_ref.shape[1]

    @pl.when(step == 0)
    def _():
        carry[...] = jnp.zeros_like(carry)
        before = (lax.broadcasted_iota(jnp.int32, (tr, tr), 0)
                  < lax.broadcasted_iota(jnp.int32, (tr, tr), 1))
        tri[...] = jnp.where(before, 1.0, 0.0).astype(BF16)

    scores = jax.nn.sigmoid(lg_ref[...])
    choice = scores + br_ref[...]

    iota_g = lax.broadcasted_iota(jnp.int32, (GROUP_SIZE, tr), 0)
    gs = []
    for g in range(N_EXPERT_GROUPS):
        cg = choice[g * GROUP_SIZE:(g + 1) * GROUP_SIZE, :]
        m1, am = _first_max(cg, iota_g, GROUP_SIZE)
        m2 = jnp.max(jnp.where(iota_g == am, -jnp.inf, cg), axis=0, keepdims=True)
        gs.append(m1 + m2)
    gscore = jnp.concatenate(gs, axis=0)

    iota_n = lax.broadcasted_iota(jnp.int32, (N_EXPERT_GROUPS, tr), 0)
    t = gscore
    for _ in range(TOPK_GROUPS - 1):
        _, am = _first_max(t, iota_n, N_EXPERT_GROUPS)
        t = jnp.where(iota_n == am, -jnp.inf, t)
    kth = jnp.max(t, axis=0, keepdims=True)
    keep = gscore >= kth

    v = jnp.concatenate(
        [jnp.where(keep[g:g + 1, :], choice[g * GROUP_SIZE:(g + 1) * GROUP_SIZE, :], -jnp.inf)
         for g in range(N_EXPERT_GROUPS)], axis=0)
    iota_e = lax.broadcasted_iota(jnp.int32, (N_EXPERTS, tr), 0)
    idxs, sels = [], []
    chosen = jnp.zeros((N_EXPERTS, tr), F32)
    for _ in range(TOP_K):
        _, am = _first_max(v, iota_e, N_EXPERTS)
        hit = iota_e == am
        idxs.append(am)
        sels.append(jnp.sum(jnp.where(hit, scores, 0.0), axis=0, keepdims=True))
        chosen = jnp.where(hit, 1.0, chosen)
        v = jnp.where(hit, -jnp.inf, v)
    sel = jnp.concatenate(sels, axis=0)
    idx_ref[...] = jnp.concatenate(idxs, axis=0)
    gate_ref[...] = sel / jnp.sum(sel, axis=0, keepdims=True) * ROUTED_SCALE

    earlier = _dot(chosen.astype(BF16), tri[...]) + carry[...]
    rank_ref[...] = jnp.concatenate(
        [jnp.sum(jnp.where(iota_e == am, earlier, 0.0), axis=0, keepdims=True) for am in idxs],
        axis=0).astype(jnp.int32)
    carry[...] = carry[...] + jnp.sum(chosen, axis=1, keepdims=True)
    cnt_ref[...] = jnp.broadcast_to(carry[...], cnt_ref.shape)


def _route(logits_t, b_router):
    N = logits_t.shape[1]
    tr = TM_ROUTE
    kt = pl.BlockSpec((TOP_K, tr), lambda i: (0, i))
    return pl.pallas_call(
        _route_kernel,
        out_shape=(
            jax.ShapeDtypeStruct((TOP_K, N), jnp.int32),
            jax.ShapeDtypeStruct((TOP_K, N), F32),
            jax.ShapeDtypeStruct((TOP_K, N), jnp.int32),
            jax.ShapeDtypeStruct((N_EXPERTS, LANES), F32),
        ),
        grid=(N // tr,),
        in_specs=[
            pl.BlockSpec((N_EXPERTS, tr), lambda i: (0, i)),
            pl.BlockSpec((N_EXPERTS, 1), lambda i: (0, 0)),
        ],
        out_specs=(kt, kt, kt, pl.BlockSpec((N_EXPERTS, LANES), lambda i: (0, 0))),
        scratch_shapes=[pltpu.VMEM((N_EXPERTS, 1), F32), pltpu.VMEM((tr, tr), BF16)],
        compiler_params=_params(1, 16),
        name="route",
    )(logits_t, b_router.reshape(N_EXPERTS, 1))


def _dest_kernel(idx_ref, rank_ref, ps_ref, dest_ref):
    tr = idx_ref.shape[1]
    iota_e = lax.broadcasted_iota(jnp.int32, (N_EXPERTS, tr), 0)
    idx = idx_ref[...]
    start = jnp.concatenate(
        [jnp.sum(jnp.where(iota_e == idx[k:k + 1, :], ps_ref[...], 0.0), axis=0, keepdims=True)
         for k in range(TOP_K)], axis=0)
    dest_ref[...] = (start.astype(jnp.int32) + rank_ref[...]) * ROW_SPLIT


def _dest(idx_t, rank_t, pstart):
    N = idx_t.shape[1]
    tr = TM_ROUTE
    kt = pl.BlockSpec((TOP_K, tr), lambda i: (0, i))
    return pl.pallas_call(
        _dest_kernel,
        out_shape=jax.ShapeDtypeStruct((TOP_K, N), jnp.int32),
        grid=(N // tr,),
        in_specs=[kt, kt, pl.BlockSpec((N_EXPERTS, 1), lambda i: (0, 0))],
        out_specs=kt,
        compiler_params=_params(1, 8),
        name="dest",
    )(idx_t, rank_t, pstart.astype(F32).reshape(N_EXPERTS, 1))


def _row_copy(src, dst, sem):
    return pltpu.make_async_copy(src, dst, sem)


def _token_rows(start):
    return pl.ds(pl.multiple_of(start, ROW_SPLIT), ROW_SPLIT)


def _block_rows(block):
    return pl.ds(pl.multiple_of(block * BLOCK_SUBROWS, BLOCK_SUBROWS), BLOCK_SUBROWS)


def _disp_kernel(zb_ref, dest_ref, x_ref, xs_ref, xbuf, zbuf, sem_in, sem_out):
    i = pl.program_id(0)
    n = pl.num_programs(0)
    tile_rows = xbuf.shape[1]
    tm = tile_rows // ROW_SPLIT
    slot = lax.rem(i, DISP_RING)
    ahead = lax.rem(i + 1, DISP_RING)

    def load(tile, into):
        start = pl.multiple_of(tile * tile_rows, tile_rows)
        return _row_copy(x_ref.at[pl.ds(start, tile_rows)], xbuf.at[into], sem_in.at[into])

    def drain(of):
        for k in range(TOP_K):
            _row_copy(xbuf.at[of], xs_ref.at[pl.ds(0, tile_rows)], sem_out.at[of]).wait()

    @pl.when(i == 0)
    def _():
        zbuf[...] = jnp.zeros_like(zbuf)

        def zero(j, started):
            blk = zb_ref[j]

            @pl.when(blk >= 0)
            def _():
                _row_copy(zbuf, xs_ref.at[_block_rows(blk)], sem_out.at[0]).start()

            return started + jnp.where(blk >= 0, 1, 0)

        def done(j, carry):
            _row_copy(zbuf, xs_ref.at[_block_rows(0)], sem_out.at[0]).wait()
            return carry

        started = lax.fori_loop(0, zb_ref.shape[0], zero, 0)
        lax.fori_loop(0, started, done, 0)
        load(0, 0).start()

    @pl.when(i + 1 < n)
    def _():
        @pl.when(i + 1 >= DISP_RING)
        def _():
            drain(ahead)

        load(i + 1, ahead).start()

    load(i, slot).wait()

    def issue(t, carry):
        src = xbuf.at[slot, _token_rows(t * ROW_SPLIT)]
        base = t * TOP_K
        for k in range(TOP_K):
            _row_copy(src, xs_ref.at[_token_rows(dest_ref[base + k])], sem_out.at[slot]).start(priority=k % 2)
        return carry

    lax.fori_loop(0, tm, issue, 0)

    @pl.when(i == n - 1)
    def _():
        for s in range(DISP_RING):
            drain(s)


def _dispatch(zero_blocks, dest_flat, n2p, n_rows):
    N = n2p.shape[0] // ROW_SPLIT
    tm = TM_MOVE
    assert N // tm >= DISP_RING
    grid_spec = pltpu.PrefetchScalarGridSpec(
        num_scalar_prefetch=1,
        grid=(N // tm,),
        in_specs=[
            pl.BlockSpec((tm * TOP_K,), lambda i, zb: (i,), memory_space=pltpu.SMEM),
            pl.BlockSpec(memory_space=pl.ANY),
        ],
        out_specs=pl.BlockSpec(memory_space=pl.ANY),
        scratch_shapes=[
            pltpu.VMEM((DISP_RING, tm * ROW_SPLIT, LANES), jnp.uint32),
            pltpu.VMEM((BLOCK_SUBROWS, LANES), jnp.uint32),
            pltpu.SemaphoreType.DMA((DISP_RING,)),
            pltpu.SemaphoreType.DMA((DISP_RING,)),
        ],
    )
    return pl.pallas_call(
        _disp_kernel,
        out_shape=jax.ShapeDtypeStruct((n_rows * ROW_SPLIT, LANES), jnp.uint32),
        grid_spec=grid_spec,
        compiler_params=_params(1, 8),
        name="disp",
    )(zero_blocks, dest_flat, n2p)


def _experts_kernel(eb_ref, bv_ref, xs_ref, wg_ref, wu_ref, wd_ref, ys_ref,
                    xbuf, ybuf, wgb, wub, wdb, sem_in, sem_out):
    e = pl.program_id(0)
    n_used = eb_ref[N_EXPERTS]
    n_blocks = ys_ref.shape[0] // BLOCK_SUBROWS
    first = eb_ref[e]
    end = eb_ref[e + 1]

    def in_copy(g, slot):
        return _row_copy(xs_ref.at[_block_rows(g)], xbuf.at[slot], sem_in.at[slot])

    def out_copy(g, slot):
        return _row_copy(ybuf.at[slot], ys_ref.at[_block_rows(g)], sem_out.at[slot])

    @pl.when(e == 0)
    def _():
        for g in range(RING - 1):
            @pl.when(g < n_used)
            def _(g=g):
                in_copy(g, g).start()

    @pl.when(end > first)
    def _():
        wgb[...] = wg_ref[...].astype(BF16)
        wub[...] = wu_ref[...].astype(BF16)
        wdb[...] = wd_ref[...].astype(BF16)

        def block(g, carry):
            slot = g & (RING - 1)
            in_copy(g, slot).wait()

            @pl.when(g + RING - 1 < n_used)
            def _():
                in_copy(g + RING - 1, (g + RING - 1) & (RING - 1)).start()

            @pl.when(g >= RING)
            def _():
                out_copy(g - RING, slot).wait()

            lo, hi = _unpack_halves(_load_rows(xbuf.at[slot], ROW_BLOCK))
            live = lax.broadcasted_iota(jnp.int32, lo.shape, 0) < bv_ref[g]
            lo = jnp.where(live, lo, 0.0).astype(BF16)
            hi = jnp.where(live, hi, 0.0).astype(BF16)
            gate = _dot(lo, wgb[0:HALF, :]) + _dot(hi, wgb[HALF:, :])
            up = _dot(lo, wub[0:HALF, :]) + _dot(hi, wub[HALF:, :])
            act = (gate * jax.nn.sigmoid(gate)) * up
            _store_rows(ybuf.at[slot], _pack_halves(_dot(act.astype(BF16), wdb[...])))
            out_copy(g, slot).start()
            return carry

        lax.fori_loop(first, end, block, 0)

    @pl.when(e == N_EXPERTS - 1)
    def _():
        for back in range(1, RING + 1):
            @pl.when(n_used >= back)
            def _(back=back):
                out_copy(n_used - back, (n_used - back) & (RING - 1)).wait()

        xbuf[0] = jnp.zeros((BLOCK_SUBROWS, LANES), jnp.uint32)

        def zero(g, carry):
            _row_copy(xbuf.at[0], ys_ref.at[_block_rows(g)], sem_out.at[0]).start()
            return carry

        def done(g, carry):
            _row_copy(xbuf.at[0], ys_ref.at[_block_rows(g)], sem_out.at[0]).wait()
            return carry

        lax.fori_loop(n_used, n_blocks, zero, 0)
        lax.fori_loop(n_used, n_blocks, done, 0)


def _experts(expert_block, block_valid, xs, w_gate, w_up, w_down):
    def w_map(e, eb, bv):
        return (e, 0, 0)

    grid_spec = pltpu.PrefetchScalarGridSpec(
        num_scalar_prefetch=2,
        grid=(N_EXPERTS,),
        in_specs=[
            pl.BlockSpec(memory_space=pl.ANY),
            pl.BlockSpec((None, D_MODEL, EXPERT_DIM), w_map),
            pl.BlockSpec((None, D_MODEL, EXPERT_DIM), w_map),
            pl.BlockSpec((None, EXPERT_DIM, D_MODEL), w_map),
        ],
        out_specs=pl.BlockSpec(memory_space=pl.ANY),
        scratch_shapes=[
            pltpu.VMEM((RING, BLOCK_SUBROWS, LANES), jnp.uint32),
            pltpu.VMEM((RING, BLOCK_SUBROWS, LANES), jnp.uint32),
            pltpu.VMEM((D_MODEL, EXPERT_DIM), BF16),
            pltpu.VMEM((D_MODEL, EXPERT_DIM), BF16),
            pltpu.VMEM((EXPERT_DIM, D_MODEL), BF16),
            pltpu.SemaphoreType.DMA((RING,)),
            pltpu.SemaphoreType.DMA((RING,)),
        ],
    )
    return pl.pallas_call(
        _experts_kernel,
        out_shape=jax.ShapeDtypeStruct(xs.shape, jnp.uint32),
        grid_spec=grid_spec,
        compiler_params=_params(1, 24),
        name="experts",
    )(expert_block, block_valid, xs, w_gate, w_up, w_down)


def _comb_kernel(dest_ref, next_ref, ys_ref, hs_ref, gate_ref, mod_ref, gf_ref, o_ref, buf, sem):
    i = pl.program_id(0)
    tm = hs_ref.shape[0]
    slot = i & 1

    def gather(rows_ref, into):
        def issue(t, carry):
            base = t * TOP_K
            for k in range(TOP_K):
                _row_copy(ys_ref.at[_token_rows(rows_ref[base + k])],
                          buf.at[into, k, _token_rows(t * ROW_SPLIT)], sem.at[into]).start(priority=k % 2)
            return carry

        lax.fori_loop(0, tm, issue, 0)

    @pl.when(i == 0)
    def _():
        gather(dest_ref, 0)

    for into in range(2):
        @pl.when(jnp.logical_and(i + 1 < pl.num_programs(0), slot == 1 - into))
        def _(into=into):
            gather(next_ref, into)

    for k in range(TOP_K):
        _row_copy(ys_ref.at[pl.ds(0, tm * ROW_SPLIT)], buf.at[slot, k], sem.at[slot]).wait()

    for c in range(tm // COMB_CHUNK):
        tok = slice(c * COMB_CHUNK, (c + 1) * COMB_CHUNK)
        gates = gate_ref[tok, :]
        lo = jnp.zeros((COMB_CHUNK, HALF), F32)
        hi = jnp.zeros((COMB_CHUNK, HALF), F32)
        for k in range(TOP_K):
            piece = buf.at[slot, k, pl.ds(c * COMB_CHUNK * ROW_SPLIT, COMB_CHUNK * ROW_SPLIT)]
            lo_k, hi_k = _unpack_halves(_load_rows(piece, COMB_CHUNK))
            gk = gates[:, k:k + 1]
            lo = lo + gk * lo_k
            hi = hi + gk * hi_k
        routed = jnp.concatenate([lo, hi], axis=1)
        h2 = hs_ref[tok, :] + mod_ref[0, 5:6, :] * routed
        o_ref[tok, :] = _rms(h2, gf_ref[...])


def _combine(dest_flat, ys, hs2, gates_nk, mod3, g_final, seq):
    N = hs2.shape[0]
    tm = TM_MOVE
    per_seq = seq // tm
    n_tiles = N // tm
    return pl.pallas_call(
        _comb_kernel,
        out_shape=jax.ShapeDtypeStruct((N, D_MODEL), F32),
        grid=(n_tiles,),
        in_specs=[
            pl.BlockSpec((tm * TOP_K,), lambda i: (i,), memory_space=pltpu.SMEM),
            pl.BlockSpec((tm * TOP_K,), lambda i: (jnp.minimum(i + 1, n_tiles - 1),), memory_space=pltpu.SMEM),
            pl.BlockSpec(memory_space=pl.ANY),
            pl.BlockSpec((tm, D_MODEL), lambda i: (i, 0)),
            pl.BlockSpec((tm, TOP_K), lambda i: (i, 0)),
            pl.BlockSpec((1, 6, D_MODEL), lambda i: (i // per_seq, 0, 0)),
            pl.BlockSpec((1, D_MODEL), lambda i: (0, 0)),
        ],
        out_specs=pl.BlockSpec((tm, D_MODEL), lambda i: (i, 0)),
        scratch_shapes=[pltpu.VMEM((2, TOP_K, tm * ROW_SPLIT, LANES), jnp.uint32),
                        pltpu.SemaphoreType.DMA((2,))],
        compiler_params=_params(1, 32),
        name="comb",
    )(dest_flat, dest_flat, ys, hs2, gates_nk, mod3, g_final.reshape(1, D_MODEL))


def _layer(x, mod3, g_norm_mix, w_in, sgu_ln_gain, sgu_ln_bias, w_spatial, b_spatial, g_out_attn,
           g_out_sgu, w_out, g_norm_ffn, w_router, b_router, w_exp_gate, w_exp_up, w_exp_down,
           w_sh_gate, w_sh_up, w_sh_down, g_final):
    B, S, _ = x.shape
    N = B * S

    q, k, v, u, z = _inproj(x, mod3, g_norm_mix, w_in.astype(BF16))
    head_of_lane = jnp.arange(ATTN_WIDTH) // HEAD_DIM
    slopes = jnp.exp2(-8.0 * jnp.arange(1, ATTN_HEADS + 1, dtype=F32) / ATTN_HEADS)
    slopes_lane = slopes[head_of_lane].reshape(ATTN_WIDTH // LANES, 1, LANES)
    attn = _attention(q, k, v, slopes_lane)

    hs, n2p, logits_t = _mix(
        attn, u, z, x, mod3, g_out_attn, g_out_sgu, sgu_ln_gain, sgu_ln_bias, w_spatial, b_spatial.T,
        w_out.astype(BF16), g_norm_ffn, w_router.T.astype(BF16),
        jnp.concatenate([w_sh_gate, w_sh_up], axis=1).astype(BF16), w_sh_down.astype(BF16))

    idx_t, gates_t, rank_t, counts = _route(logits_t, b_router)

    n_blocks = (N * TOP_K + N_EXPERTS * (ROW_BLOCK - 1)) // ROW_BLOCK
    cnt = counts[:, 0].astype(jnp.int32)
    padded = (cnt + ROW_BLOCK - 1) // ROW_BLOCK * ROW_BLOCK
    pends = jnp.cumsum(padded)
    pstart = pends - padded
    block_row = jnp.arange(n_blocks, dtype=jnp.int32) * ROW_BLOCK
    block_e = jnp.sum(pends[None, :] <= block_row[:, None], axis=1, dtype=jnp.int32)
    block_e = jnp.minimum(block_e, N_EXPERTS - 1)
    own = block_e[:, None] == jnp.arange(N_EXPERTS, dtype=jnp.int32)[None, :]
    live_end = jnp.sum(jnp.where(own, (pstart + cnt)[None, :], 0), axis=1)
    block_valid = jnp.clip(live_end - block_row, 0, ROW_BLOCK).astype(jnp.int32)
    n_used = (pends[-1:] // ROW_BLOCK).astype(jnp.int32)
    expert_block = jnp.concatenate([pstart // ROW_BLOCK, n_used]).astype(jnp.int32)

    last_block = jnp.where(cnt > 0, pends // ROW_BLOCK - 1, -1)
    tail_block = n_used[0] + jnp.arange(n_blocks - N * TOP_K // ROW_BLOCK)
    tail_block = jnp.where(tail_block < n_blocks, tail_block, -1)
    zero_blocks = jnp.concatenate([last_block, tail_block]).astype(jnp.int32)

    dest_flat = _dest(idx_t, rank_t, pstart).T.reshape(N * TOP_K)
    xs = _dispatch(zero_blocks, dest_flat, n2p, n_blocks * ROW_BLOCK)
    ys = _experts(expert_block, block_valid, xs, w_exp_gate, w_exp_up, w_exp_down)
    out = _combine(dest_flat, ys, hs.reshape(N, D_MODEL), gates_t.T, mod3, g_final, S)
    return out.reshape(B, S, D_MODEL)


def kernel(x, c, w_ada, b_ada, g_norm_mix, w_in, sgu_ln_gain, sgu_ln_bias, w_spatial, b_spatial, g_out_attn, g_out_sgu, w_out, g_norm_ffn, w_router, b_router, w_exp_gate, w_exp_up, w_exp_down, w_sh_gate, w_sh_up, w_sh_down, g_final):
    assert w_ada.shape[0] == 1, "single-layer stack"
    B = x.shape[0]
    mod3 = _ada(c, w_ada[0], b_ada[0]).reshape(B, 6, D_MODEL)
    return _layer(x, mod3, g_norm_mix[0], w_in[0], sgu_ln_gain[0], sgu_ln_bias[0], w_spatial[0],
                  b_spatial[0], g_out_attn[0], g_out_sgu[0], w_out[0], g_norm_ffn[0], w_router[0],
                  b_router[0], w_exp_gate[0], w_exp_up[0], w_exp_down[0], w_sh_gate[0], w_sh_up[0],
                  w_sh_down[0], g_final)
```
